```python
import jax, jax.numpy as jnp
from jax import lax
import numpy as np

D_MODEL = 1024
BATCH = 1
SEQ = 16384
DEPTH = 1
DEC_BATCH = 16
DEC_SEQ = 16
PAST_LEN = 2048

CHUNK = 64
D_MIX = D_MODEL
D_A = D_MIX // 2
D_B = D_MIX - D_A
A_HEAD = 64
A_HEADS = D_A // A_HEAD
A_W_RANK = 32
A_A_RANK = 32
A_G_RANK = 64
B_HEADS = 4
B_DK = D_B // (2 * B_HEADS)
B_DV = D_B // B_HEADS
B_GATE_RANK = 16
B_TAU = 16.0
D_FF = 2816
EPS = 1e-6
LNX_EPS = 64e-5

A_COLS = 3 * D_A + A_W_RANK + A_A_RANK + A_G_RANK
B_COLS = 2 * B_HEADS * B_DK + D_B + B_GATE_RANK + D_B
IN_COLS = A_COLS + B_COLS

kernel_name = "rwkv7_gla_parallel_macaron_stream_step"


def _rmsnorm(x, g):
    xf = x.astype(jnp.float32)
    y = xf * lax.rsqrt(jnp.mean(xf * xf, axis=-1, keepdims=True) + EPS)
    return (y * g.astype(jnp.float32)).astype(x.dtype)


def _swiglu(h, wg, wu, wd):
    return (jax.nn.silu(h @ wg) * (h @ wu)) @ wd


def _rwkv7_step(S, inp):
    r_t, w_t, k_t, v_t, kk_t, kka_t = inp
    sa = jnp.einsum('bhvk,bhk->bhv', S, kk_t)
    S = S * w_t[:, :, None, :] - sa[..., None] * kka_t[:, :, None, :] + v_t[..., None] * k_t[:, :, None, :]
    o = jnp.einsum('bhvk,bhk->bhv', S, r_t)
    return S, o


def _rwkv7(p_a, prev_row, S0, mu, w0, w_up, a0, a_up, g_up, k_k, k_a, r_k, lnx_w, lnx_b):
    B, T, _ = p_a.shape
    f32 = jnp.float32
    shifted = jnp.concatenate([prev_row.astype(p_a.dtype), p_a[:, :-1]], axis=1)
    xs = p_a + (shifted - p_a) * mu
    idx = [D_A, 2 * D_A, 3 * D_A, 3 * D_A + A_W_RANK, 3 * D_A + A_W_RANK + A_A_RANK]
    r, k, v, xw, xa, xg = jnp.split(xs, idx, axis=-1)
    w_log = -jax.nn.softplus(-(w0 + jnp.tanh(xw) @ w_up).astype(f32)) - 0.5
    decay = jnp.exp(-jnp.exp(w_log))
    a = jax.nn.sigmoid((a0 + xa @ a_up).astype(f32))
    g = (jax.nn.sigmoid(xg) @ g_up).astype(f32)
    hd = lambda t: t.reshape(B, T, A_HEADS, A_HEAD)
    kf = hd(k.astype(f32))
    kk = hd((k * k_k).astype(f32))
    kk = kk / jnp.maximum(jnp.sqrt(jnp.sum(kk * kk, axis=-1, keepdims=True)), 1e-12)
    a_h = hd(a)
    kmod = kf * (1.0 + (a_h - 1.0) * k_a.astype(f32).reshape(A_HEADS, A_HEAD))
    rf = hd(r.astype(f32))
    vf = hd(v.astype(f32))
    seq = tuple(t.transpose(1, 0, 2, 3) for t in (rf, hd(decay), kmod, vf, kk, kk * a_h))
    S, o = lax.scan(_rwkv7_step, S0.astype(f32), seq)
    o = o.transpose(1, 0, 2, 3)
    mean = jnp.mean(o, axis=-1, keepdims=True)
    var = jnp.mean((o - mean) ** 2, axis=-1, keepdims=True)
    on = ((o - mean) * lax.rsqrt(var + LNX_EPS)).reshape(B, T, D_A) * lnx_w.astype(f32) + lnx_b.astype(f32)
    bonus = jnp.sum(rf * kmod * r_k.astype(f32), axis=-1, keepdims=True) * vf
    out = (on + bonus.reshape(B, T, D_A)) * g
    return out.astype(p_a.dtype), p_a[:, -1:], S


def _gla_chunked(q, k, v, log_a, S0):
    B, T, H, DK = q.shape
    DV = v.shape[-1]
    blk = CHUNK if T % CHUNK == 0 else T
    n = T // blk
    to_c = lambda t: t.reshape(B, n, blk, H, t.shape[-1]).transpose(1, 0, 3, 2, 4)
    mask = jnp.tril(jnp.ones((blk, blk), dtype=bool))[:, :, None]

    def step(S, inp):
        qc, kc, vc, gc = inp
        b = jnp.cumsum(gc, axis=-2)
        o_inter = jnp.einsum('bhtk,bhkv->bhtv', qc * jnp.exp(b), S)
        diff = b[:, :, :, None, :] - b[:, :, None, :, :]
        dec = jnp.exp(jnp.where(mask, diff, -jnp.inf))
        A = jnp.einsum('bhtk,bhsk,bhtsk->bhts', qc, kc, dec)
        o_intra = jnp.einsum('bhts,bhsv->bhtv', A, vc)
        b_last = b[:, :, -1:, :]
        S = jnp.exp(b_last[:, :, 0, :])[..., None] * S + jnp.einsum('bhsk,bhsv->bhkv', kc * jnp.exp(b_last - b), vc)
        return S, o_inter + o_intra

    S, o = lax.scan(step, S0.astype(jnp.float32), (to_c(q), to_c(k), to_c(v), to_c(log_a)))
    o = o.transpose(1, 0, 3, 2, 4).reshape(B, T, H, DV)
    return o, S


def _gla(p_b, S0, gk_up, gk_b, norm_g):
    B, T, _ = p_b.shape
    f32 = jnp.float32
    nk = B_HEADS * B_DK
    idx = [nk, 2 * nk, 2 * nk + D_B, 2 * nk + D_B + B_GATE_RANK]
    q, k, v, xgk, og = jnp.split(p_b, idx, axis=-1)
    log_a = jax.nn.log_sigmoid((xgk @ gk_up + gk_b).astype(f32)) / B_TAU
    q = q.astype(f32) * (B_DK ** -0.5)
    hk = lambda t: t.reshape(B, T, B_HEADS, B_DK)
    o, S = _gla_chunked(hk(q), hk(k.astype(f32)), v.astype(f32).reshape(B, T, B_HEADS, B_DV), hk(log_a), S0)
    o = o * lax.rsqrt(jnp.mean(o * o, axis=-1, keepdims=True) + EPS) * norm_g.astype(f32)
    out = o.reshape(B, T, D_B) * jax.nn.silu(og.astype(f32))
    return out.astype(p_b.dtype), S


def _trunk(x, shift, wkv, gla, params):
    (ln1_g, ffn1_wg, ffn1_wu, ffn1_wd, ln_mix_g, w_in, mu_shift, w0, w_lora_up, a0, a_lora_up,
     g_lora_up, k_k, k_a, r_k, lnx_w, lnx_b, gk_up, gk_b, gla_norm_g, w_out,
     ln2_g, ffn2_wg, ffn2_wu, ffn2_wd) = params
    new_shift, new_wkv, new_gla = [], [], []
    for l in range(DEPTH):
        h = _rmsnorm(x, ln1_g[l])
        x = x + 0.5 * _swiglu(h, ffn1_wg[l], ffn1_wu[l], ffn1_wd[l])
        h = _rmsnorm(x, ln_mix_g[l])
        p = h @ w_in[l]
        p_a, p_b = p[..., :A_COLS], p[..., A_COLS:]
        o_a, row_a, S_a = _rwkv7(p_a, shift[l], wkv[l], mu_shift[l], w0[l], w_lora_up[l], a0[l],
                                 a_lora_up[l], g_lora_up[l], k_k[l], k_a[l], r_k[l], lnx_w[l], lnx_b[l])
        o_b, S_b = _gla(p_b, gla[l], gk_up[l], gk_b[l], gla_norm_g[l])
        x = x + jnp.concatenate([o_a, o_b], axis=-1) @ w_out[l]
        h = _rmsnorm(x, ln2_g[l])
        x = x + 0.5 * _swiglu(h, ffn2_wg[l], ffn2_wu[l], ffn2_wd[l])
        new_shift.append(row_a)
        new_wkv.append(S_a)
        new_gla.append(S_b)
    return x, jnp.stack(new_shift), jnp.stack(new_wkv), jnp.stack(new_gla)


def setup_inputs(seed: int = 0) -> dict:
    key = jax.random.key(seed)
    ks = jax.random.split(key, 40)
    nrm = lambda k, shape, s: jax.random.normal(k, shape, jnp.float32) * s
    L = DEPTH
    d = {}
    d["x_prompt"] = nrm(ks[0], (BATCH, SEQ, D_MODEL), 1.0)
    d["x_sample"] = nrm(ks[1], (DEC_BATCH, DEC_SEQ, D_MODEL), 1.0)
    d["state_shift"] = nrm(ks[2], (L, DEC_BATCH, 1, A_COLS), 1.0)
    d["state_wkv"] = nrm(ks[3], (L, DEC_BATCH, A_HEADS, A_HEAD, A_HEAD), 0.5)
    d["state_gla"] = nrm(ks[4], (L, DEC_BATCH, B_HEADS, B_DK, B_DV), 0.5)
    d["ln1_g"] = 1.0 + nrm(ks[5], (L, D_MODEL), 0.02)
    d["ffn1_wg"] = nrm(ks[6], (L, D_MODEL, D_FF), D_MODEL ** -0.5)
    d["ffn1_wu"] = nrm(ks[7], (L, D_MODEL, D_FF), D_MODEL ** -0.5)
    d["ffn1_wd"] = nrm(ks[8], (L, D_FF, D_MODEL), D_FF ** -0.5)
    d["ln_mix_g"] = 1.0 + nrm(ks[9], (L, D_MODEL), 0.02)
    d["w_in"] = nrm(ks[10], (L, D_MODEL, IN_COLS), D_MODEL ** -0.5)
    d["mu_shift"] = jax.random.uniform(ks[11], (L, A_COLS), jnp.float32)
    d["w0"] = jax.random.uniform(ks[12], (L, D_A), jnp.float32, -3.0, 1.0)
    d["w_lora_up"] = nrm(ks[13], (L, A_W_RANK, D_A), 0.5 * A_W_RANK ** -0.5)
    d["a0"] = nrm(ks[14], (L, D_A), 0.1)
    d["a_lora_up"] = nrm(ks[15], (L, A_A_RANK, D_A), 0.5 * A_A_RANK ** -0.5)
    d["g_lora_up"] = nrm(ks[16], (L, A_G_RANK, D_A), A_G_RANK ** -0.5)
    d["k_k"] = 0.85 + nrm(ks[17], (L, D_A), 0.05)
    d["k_a"] = 1.0 + nrm(ks[18], (L, D_A), 0.05)
    d["r_k"] = nrm(ks[19], (L, A_HEADS, A_HEAD), 0.1)
    d["lnx_w"] = 1.0 + nrm(ks[20], (L, D_A), 0.02)
    d["lnx_b"] = nrm(ks[21], (L, D_A), 0.02)
    d["gk_up"] = nrm(ks[22], (L, B_GATE_RANK, B_HEADS * B_DK), B_GATE_RANK ** -0.5)
    d["gk_b"] = nrm(ks[23], (L, B_HEADS * B_DK), 0.1)
    d["gla_norm_g"] = 1.0 + nrm(ks[24], (L, B_DV), 0.02)
    d["w_out"] = nrm(ks[25], (L, D_MIX, D_MODEL), D_MIX ** -0.5)
    d["ln2_g"] = 1.0 + nrm(ks[26], (L, D_MODEL), 0.02)
    d["ffn2_wg"] = nrm(ks[27], (L, D_MODEL, D_FF), D_MODEL ** -0.5)
    d["ffn2_wu"] = nrm(ks[28], (L, D_MODEL, D_FF), D_MODEL ** -0.5)
    d["ffn2_wd"] = nrm(ks[29], (L, D_FF, D_MODEL), D_FF ** -0.5)
    d["ln_f_g"] = 1.0 + nrm(ks[30], (D_MODEL,), 0.02)
    return d


def reference(x_prompt, x_sample, state_shift, state_wkv, state_gla, ln1_g, ffn1_wg, ffn1_wu, ffn1_wd,
              ln_mix_g, w_in, mu_shift, w0, w_lora_up, a0, a_lora_up, g_lora_up, k_k, k_a, r_k,
              lnx_w, lnx_b, gk_up, gk_b, gla_norm_g, w_out, ln2_g, ffn2_wg, ffn2_wu, ffn2_wd, ln_f_g):
    params = (ln1_g, ffn1_wg, ffn1_wu, ffn1_wd, ln_mix_g, w_in, mu_shift, w0, w_lora_up, a0, a_lora_up,
              g_lora_up, k_k, k_a, r_k, lnx_w, lnx_b, gk_up, gk_b, gla_norm_g, w_out,
              ln2_g, ffn2_wg, ffn2_wu, ffn2_wd)
    B = x_prompt.shape[0]
    shift0 = jnp.zeros((DEPTH, B, 1, A_COLS), x_prompt.dtype)
    wkv0 = jnp.zeros((DEPTH, B, A_HEADS, A_HEAD, A_HEAD), jnp.float32)
    gla0 = jnp.zeros((DEPTH, B, B_HEADS, B_DK, B_DV), jnp.float32)
    xp, shift_p, wkv_p, gla_p = _trunk(x_prompt, shift0, wkv0, gla0, params)
    y_prompt = _rmsnorm(xp, ln_f_g)
    xs, shift_s, wkv_s, gla_s = _trunk(x_sample, state_shift, state_wkv, state_gla, params)
    y_sample = _rmsnorm(xs, ln_f_g)
    return (y_prompt, y_sample, shift_p, wkv_p, gla_p, shift_s, wkv_s, gla_s)
```

```python
import functools
import math

import jax
import jax.numpy as jnp
from jax import lax
from jax.experimental import pallas as pl
from jax.experimental.pallas import tpu as pltpu

F32 = jnp.float32
BF16 = jnp.bfloat16
HIGHEST = lax.Precision.HIGHEST

EPS = 1e-6
LNX_EPS = 64e-5
A_HEAD = 64
A_W_RANK = 32
A_A_RANK = 32
A_G_RANK = 64
B_HEADS = 4
B_GATE_RANK = 16
B_TAU = 16.0
GLA_CHUNK = 64
SUB = 16
LANE = 128
VMEM_LIMIT = 56 * 1024 * 1024
TOKEN_TILE = 256


def _dot(a, b, precision=None):
    return jnp.dot(a, b, preferred_element_type=F32, precision=precision)


def _dot_nt(a, b, precision=None):
    return lax.dot_general(a, b, (((1,), (1,)), ((), ())), preferred_element_type=F32, precision=precision)


def _dot_tn(a, b, precision=None):
    return lax.dot_general(a, b, (((0,), (0,)), ((), ())), preferred_element_type=F32, precision=precision)


def _split3(x):
    hi = x.astype(BF16)
    r1 = x - hi.astype(F32)
    mid = r1.astype(BF16)
    lo = (r1 - mid.astype(F32)).astype(BF16)
    return hi, mid, lo


def _dot_ones_lhs(ones_bf16, x):
    hi, mid, lo = _split3(x)
    return _dot(ones_bf16, hi) + _dot(ones_bf16, mid) + _dot(ones_bf16, lo)


def _dot_ones_rhs(x, ones_bf16):
    hi, mid, lo = _split3(x)
    return _dot(hi, ones_bf16) + _dot(mid, ones_bf16) + _dot(lo, ones_bf16)


def _softplus(x):
    return jnp.maximum(x, 0.0) + jnp.log(1.0 + jnp.exp(-jnp.abs(x)))


def _rmsnorm(x, g):
    return x * lax.rsqrt(jnp.mean(x * x, axis=-1, keepdims=True) + EPS) * g


def _swiglu_half_step(x, ln_g, wg_ref, wu_ref, wd_ref):
    h = _rmsnorm(x, ln_g).astype(BF16)
    g = _dot(h, wg_ref[...])
    u = _dot(h, wu_ref[...])
    act = (g * jax.nn.sigmoid(g) * u).astype(BF16)
    return x + 0.5 * _dot(act, wd_ref[...])


def _pre_kernel(x_ref, ln1_ref, wg_ref, wu_ref, wd_ref, lnm_ref, wia_ref, wib_ref, x1_ref, pa_ref, pb_ref):
    x1 = _swiglu_half_step(x_ref[...], ln1_ref[...], wg_ref, wu_ref, wd_ref)
    x1_ref[...] = x1
    h = _rmsnorm(x1, lnm_ref[...]).astype(BF16)
    pa_ref[...] = _dot(h, wia_ref[...])
    pb_ref[...] = _dot(h, wib_ref[...])


def _resident(shape):
    return pl.BlockSpec(shape, lambda *_: (0,) * len(shape), pipeline_mode=pl.Buffered(1))


def _token_tile(n):
    tm = min(TOKEN_TILE, n)
    assert n % tm == 0
    return tm


def _pre_call(x, ln1, wg, wu, wd, lnm, wia, wib):
    n, d = x.shape
    tm = _token_tile(n)
    tok = lambda w: pl.BlockSpec((tm, w), lambda i: (i, 0))
    return pl.pallas_call(
        _pre_kernel,
        grid=(n // tm,),
        in_specs=[tok(d), _resident(ln1.shape), _resident(wg.shape), _resident(wu.shape), _resident(wd.shape),
                  _resident(lnm.shape), _resident(wia.shape), _resident(wib.shape)],
        out_specs=[tok(d), tok(wia.shape[1]), tok(wib.shape[1])],
        out_shape=[jax.ShapeDtypeStruct((n, d), F32), jax.ShapeDtypeStruct((n, wia.shape[1]), F32),
                   jax.ShapeDtypeStruct((n, wib.shape[1]), F32)],
        compiler_params=pltpu.CompilerParams(dimension_semantics=("arbitrary",), vmem_limit_bytes=VMEM_LIMIT),
        name="pre_ffn_inproj",
    )(x, ln1, wg, wu, wd, lnm, wia, wib)


def _post_kernel(final_norm, x1_ref, oa_ref, ob_ref, woa_ref, wob_ref, ln2_ref, wg_ref, wu_ref, wd_ref, lnf_ref,
                 y_ref):
    x2 = x1_ref[...] + _dot(oa_ref[...].astype(BF16), woa_ref[...]) + _dot(ob_ref[...].astype(BF16), wob_ref[...])
    x3 = _swiglu_half_step(x2, ln2_ref[...], wg_ref, wu_ref, wd_ref)
    y_ref[...] = _rmsnorm(x3, lnf_ref[...]) if final_norm else x3


def _post_call(x1, oa, ob, woa, wob, ln2, wg, wu, wd, lnf, final_norm):
    n, d = x1.shape
    tm = _token_tile(n)
    tok = lambda w: pl.BlockSpec((tm, w), lambda i: (i, 0))
    return pl.pallas_call(
        functools.partial(_post_kernel, final_norm),
        grid=(n // tm,),
        in_specs=[tok(d), tok(oa.shape[1]), tok(ob.shape[1]), _resident(woa.shape), _resident(wob.shape),
                  _resident(ln2.shape), _resident(wg.shape), _resident(wu.shape), _resident(wd.shape),
                  _resident(lnf.shape)],
        out_specs=tok(d),
        out_shape=jax.ShapeDtypeStruct((n, d), F32),
        compiler_params=pltpu.CompilerParams(dimension_semantics=("arbitrary",), vmem_limit_bytes=VMEM_LIMIT),
        name="post_outproj_ffn",
    )(x1, oa, ob, woa, wob, ln2, wg, wu, wd, lnf)


def _tri_masks(c):
    row = lax.broadcasted_iota(jnp.int32, (c, c), 0)
    col = lax.broadcasted_iota(jnp.int32, (c, c), 1)
    return row, col


def _rwkv_kernel(c_len, pa_ref, shift0_ref, wkv0_ref, mu_ref, w0_ref, wup_ref, a0_ref, aup_ref, gup_ref,
                 kk_ref, ka_ref, rk_ref, lnw_ref, lnb_ref, eseg_ref, o_ref, wkv_ref, prev_ref):
    chunk = pl.program_id(1)
    n_heads = wkv_ref.shape[1]
    d_a = n_heads * A_HEAD

    @pl.when(chunk == 0)
    def _():
        wkv_ref[...] = wkv0_ref[...]
        prev_ref[...] = shift0_ref[0]

    p = pa_ref[0]
    row1 = lax.broadcasted_iota(jnp.int32, (c_len, 1), 0)
    shifted = jnp.where(row1 == 0, prev_ref[...], pltpu.roll(p, 1, axis=0))
    prev_ref[...] = p[c_len - 1:c_len, :]
    xs = p + (shifted - p) * mu_ref[...]
    r = xs[:, 0:d_a]
    k = xs[:, d_a:2 * d_a]
    v = xs[:, 2 * d_a:3 * d_a]
    lora_in = xs[:, 3 * d_a:3 * d_a + LANE]

    lw = _dot(jnp.tanh(lora_in).astype(BF16), wup_ref[...])
    la = _dot(lora_in.astype(BF16), aup_ref[...])
    g = _dot(jax.nn.sigmoid(lora_in).astype(BF16), gup_ref[...])
    w_log = -_softplus(-(w0_ref[...] + lw)) - 0.5
    logw = -jnp.exp(w_log)
    a = jax.nn.sigmoid(a0_ref[...] + la)

    eseg = eseg_ref[...]
    kk = k * kk_ref[...]
    kk_norm = jnp.sqrt(_dot_ones_rhs(kk * kk, eseg))
    kk = kk / jnp.maximum(kk_norm, 1e-12)
    kmod = k * (1.0 + (a - 1.0) * ka_ref[...])
    kka = kk * a

    row, col = _tri_masks(c_len)
    lower = row >= col
    strict = row > col
    tri = jnp.where(lower, 1.0, 0.0).astype(BF16)
    eye = jnp.where(row == col, 1.0, 0.0).astype(F32)
    gc = _dot_ones_lhs(tri, logw)
    gc_last = gc[c_len - 1:c_len, :]
    a_hat = -kk * jnp.exp(gc - logw)
    r_til = r * jnp.exp(gc)
    e_inv = jnp.exp(-gc)
    k_hat = kmod * e_inv
    b_hat = kka * e_inv
    e_end = jnp.exp(gc_last - gc)
    k_bar = kmod * e_end
    b_bar = kka * e_end
    w_end = jnp.exp(gc_last)

    r64, c64 = _tri_masks(A_HEAD)
    eye_k = jnp.where(r64 == c64, 1.0, 0.0).astype(F32)
    n_double = int(math.log2(c_len))
    assert 2 ** n_double == c_len

    outs = []
    for h in range(n_heads):
        sl = slice(h * A_HEAD, (h + 1) * A_HEAD)
        ah, rh, kh, bh, vh = a_hat[:, sl], r_til[:, sl], k_hat[:, sl], b_hat[:, sl], v[:, sl]
        ar = jnp.concatenate([ah, rh], axis=0)
        xb = _dot_nt(ar, bh, HIGHEST)
        xk = _dot_nt(ar, kh, HIGHEST)
        l_mat = jnp.where(strict, xb[:c_len], 0.0)
        a_ak = jnp.where(strict, xk[:c_len], 0.0)
        a_rb = jnp.where(lower, xb[c_len:], 0.0)
        a_rk = jnp.where(lower, xk[c_len:], 0.0)
        t_inv = eye + l_mat
        pw = l_mat
        for _ in range(1, n_double):
            pw = _dot(pw, pw, HIGHEST)
            t_inv = t_inv + _dot(t_inv, pw, HIGHEST)
        p_mat = _dot(t_inv, ah, HIGHEST)
        q_mat = _dot(t_inv, _dot(a_ak, vh, HIGHEST), HIGHEST)
        g_mat = rh + _dot(a_rb, p_mat, HIGHEST)
        h_mat = _dot(a_rk, vh, HIGHEST) + _dot(a_rb, q_mat, HIGHEST)
        m_c = eye_k * w_end[:, sl] + _dot_tn(p_mat, b_bar[:, sl], HIGHEST)
        n_c = _dot_tn(vh, k_bar[:, sl], HIGHEST) + _dot_tn(q_mat, b_bar[:, sl], HIGHEST)
        s0 = wkv_ref[0, h]
        outs.append(_dot_nt(g_mat, s0, HIGHEST) + h_mat)
        wkv_ref[0, h] = _dot(s0, m_c, HIGHEST) + n_c
    o = jnp.concatenate(outs, axis=-1)

    inv_n = 1.0 / A_HEAD
    mean = _dot_ones_rhs(o, eseg) * inv_n
    cen = o - mean
    var = _dot_ones_rhs(cen * cen, eseg) * inv_n
    on = cen * lax.rsqrt(var + LNX_EPS) * lnw_ref[...] + lnb_ref[...]
    bonus = _dot_ones_rhs(r * kmod * rk_ref[...], eseg) * v
    o_ref[0] = (on + bonus) * g


def _rwkv_call(pa, shift0, wkv0, c_len, mu, w0, wup, a0, aup, gup, k_k, k_a, r_k, lnw, lnb, eseg):
    b, t, a_cols = pa.shape
    n_heads = wkv0.shape[1]
    d_a = n_heads * A_HEAD
    seq = lambda w: pl.BlockSpec((1, c_len, w), lambda i, j: (i, j, 0))
    per_b = lambda shp: pl.BlockSpec((1,) + shp, lambda i, j: (i,) + (0,) * len(shp))
    consts = [mu, w0, wup, a0, aup, gup, k_k, k_a, r_k, lnw, lnb, eseg]
    return pl.pallas_call(
        functools.partial(_rwkv_kernel, c_len),
        grid=(b, t // c_len),
        in_specs=[seq(a_cols), per_b((1, a_cols)), per_b(wkv0.shape[1:])] + [_resident(w.shape) for w in consts],
        out_specs=[seq(d_a), per_b(wkv0.shape[1:])],
        out_shape=[jax.ShapeDtypeStruct((b, t, d_a), F32), jax.ShapeDtypeStruct(wkv0.shape, F32)],
        scratch_shapes=[pltpu.VMEM((1, a_cols), F32)],
        compiler_params=pltpu.CompilerParams(dimension_semantics=("arbitrary", "arbitrary"),
                                             vmem_limit_bytes=VMEM_LIMIT),
        name="rwkv7_mixer",
    )(pa, shift0, wkv0, *consts)


def _gla_kernel(c_len, pb_ref, gla0_ref, gkup_ref, gkb_ref, ng_ref, o_ref, st_ref):
    chunk = pl.program_id(1)
    n_heads = st_ref.shape[1]
    dk, dv = st_ref.shape[2], st_ref.shape[3]
    nk = n_heads * dk
    nv = n_heads * dv

    @pl.when(chunk == 0)
    def _():
        st_ref[...] = gla0_ref[...]

    p = pb_ref[0]
    q = p[:, 0:nk] * (dk ** -0.5)
    k = p[:, nk:2 * nk]
    v = p[:, 2 * nk:2 * nk + nv]
    og = p[:, 2 * nk + nv:2 * nk + 2 * nv]
    xgk = p[:, 2 * nk + 2 * nv:2 * nk + 2 * nv + LANE]
    z = _dot(xgk.astype(BF16), gkup_ref[...]) + gkb_ref[...]
    log_a = -_softplus(-z) / B_TAU

    row, col = _tri_masks(c_len)
    sub = min(SUB, c_len)
    n_sub = c_len // sub
    tri = jnp.where(row >= col, 1.0, 0.0).astype(BF16)
    b = _dot_ones_lhs(tri, log_a)
    if n_sub > 1:
        tri_start = jnp.where(col < (row // sub) * sub, 1.0, 0.0).astype(BF16)
        b_start = _dot_ones_lhs(tri_start, log_a)
    else:
        b_start = jnp.zeros_like(b)
    b_last = b[c_len - 1:c_len, :]
    q_inter = q * jnp.exp(b)
    q_local = q * jnp.exp(b - b_start)
    k_state = k * jnp.exp(b_last - b)
    row1 = lax.broadcasted_iota(jnp.int32, (c_len, 1), 0)
    k_local = []
    for i in range(n_sub):
        ref_i = b_start[i * sub:i * sub + 1, :]
        expo = jnp.where(row1 < (i + 1) * sub, ref_i - b, -1e30)
        k_local.append(k * jnp.exp(expo))

    log_a_pieces = _split3(log_a)
    ones_cv = jnp.ones((c_len, dv), BF16)
    outs = []
    for h in range(n_heads):
        ks = slice(h * dk, (h + 1) * dk)
        vh = v[:, h * dv:(h + 1) * dv]
        blocks = []
        for i in range(n_sub):
            a_i = _dot_nt(q_local[i * sub:(i + 1) * sub, ks], k_local[i][:, ks], HIGHEST)
            blocks.append(a_i)
        a_mat = jnp.concatenate(blocks, axis=0) if n_sub > 1 else blocks[0]
        a_mat = jnp.where(row >= col, a_mat, 0.0)
        s0 = st_ref[0, h]
        o_h = _dot(a_mat, vh, HIGHEST) + _dot(q_inter[:, ks], s0, HIGHEST)
        dec = jnp.exp(sum(_dot_tn(piece[:, ks], ones_cv) for piece in log_a_pieces))
        st_ref[0, h] = dec * s0 + _dot_tn(k_state[:, ks], vh, HIGHEST)
        o_h = o_h * lax.rsqrt(jnp.mean(o_h * o_h, axis=-1, keepdims=True) + EPS) * ng_ref[...]
        og_h = og[:, h * dv:(h + 1) * dv]
        outs.append(o_h * (og_h * jax.nn.sigmoid(og_h)))
    o_ref[0] = jnp.concatenate(outs, axis=-1)


def _gla_call(pb, gla0, c_len, gkup, gkb, ng):
    b, t, cols = pb.shape
    n_heads, dk, dv = gla0.shape[1:]
    seq = lambda w: pl.BlockSpec((1, c_len, w), lambda i, j: (i, j, 0))
    per_b = lambda shp: pl.BlockSpec((1,) + shp, lambda i, j: (i,) + (0,) * len(shp))
    consts = [gkup, gkb, ng]
    return pl.pallas_call(
        functools.partial(_gla_kernel, c_len),
        grid=(b, t // c_len),
        in_specs=[seq(cols), per_b(gla0.shape[1:])] + [_resident(w.shape) for w in consts],
        out_specs=[seq(n_heads * dv), per_b(gla0.shape[1:])],
        out_shape=[jax.ShapeDtypeStruct((b, t, n_heads * dv), F32), jax.ShapeDtypeStruct(gla0.shape, F32)],
        compiler_params=pltpu.CompilerParams(dimension_semantics=("arbitrary", "arbitrary"),
                                             vmem_limit_bytes=VMEM_LIMIT),
        name="gla_mixer",
    )(pb, gla0, *consts)


def _chunk_len(t):
    return GLA_CHUNK if t % GLA_CHUNK == 0 else t


def _pad_rows(w, start, total):
    return jnp.zeros((total, w.shape[1]), w.dtype).at[start:start + w.shape[0]].set(w)


def _prep_layer(l, ln1_g, ffn1_wg, ffn1_wu, ffn1_wd, ln_mix_g, w_in, mu_shift, w0, w_lora_up, a0, a_lora_up,
                g_lora_up, k_k, k_a, r_k, lnx_w, lnx_b, gk_up, gk_b, gla_norm_g, w_out, ln2_g, ffn2_wg, ffn2_wu,
                ffn2_wd):
    d_a = w0.shape[1]
    a_cols = mu_shift.shape[1]
    nk = gk_b.shape[1]
    d_b = w_out.shape[1] - d_a
    row = lambda x: x[l].reshape(1, -1)
    wi = w_in[l]
    wi_b = wi[:, a_cols:]
    q_k_v = wi_b[:, :2 * nk + d_b]
    xgk_w = wi_b[:, 2 * nk + d_b:2 * nk + d_b + B_GATE_RANK]
    og_w = wi_b[:, 2 * nk + d_b + B_GATE_RANK:]
    pad = jnp.zeros((wi.shape[0], LANE - B_GATE_RANK), wi.dtype)
    wib = jnp.concatenate([q_k_v, og_w, xgk_w, pad], axis=1)
    eseg = jnp.kron(jnp.eye(d_a // A_HEAD, dtype=F32), jnp.ones((A_HEAD, A_HEAD), F32)).astype(BF16)
    return dict(
        pre=(row(ln1_g), ffn1_wg[l].astype(BF16), ffn1_wu[l].astype(BF16), ffn1_wd[l].astype(BF16), row(ln_mix_g),
             wi[:, :a_cols].astype(BF16), wib.astype(BF16)),
        rwkv=(row(mu_shift), row(w0), _pad_rows(w_lora_up[l], 0, LANE).astype(BF16), row(a0),
              _pad_rows(a_lora_up[l], A_W_RANK, LANE).astype(BF16),
              _pad_rows(g_lora_up[l], A_W_RANK + A_A_RANK, LANE).astype(BF16),
              row(k_k), row(k_a), row(r_k), row(lnx_w), row(lnx_b), eseg),
        gla=(_pad_rows(gk_up[l], 0, LANE).astype(BF16), row(gk_b), row(gla_norm_g)),
        post=(w_out[l, :d_a].astype(BF16), w_out[l, d_a:].astype(BF16), row(ln2_g), ffn2_wg[l].astype(BF16),
              ffn2_wu[l].astype(BF16), ffn2_wd[l].astype(BF16)),
    )


def _trunk(x, shift, wkv, gla, layers, ln_f):
    b, t, d = x.shape
    c_len = _chunk_len(t)
    xf = x.reshape(b * t, d)
    new_shift, new_wkv, new_gla = [], [], []
    for l, lw in enumerate(layers):
        x1, pa, pb = _pre_call(xf, *lw["pre"])
        pa3 = pa.reshape(b, t, -1)
        oa, s_a = _rwkv_call(pa3, shift[l], wkv[l], c_len, *lw["rwkv"])
        ob, s_b = _gla_call(pb.reshape(b, t, -1), gla[l], c_len, *lw["gla"])
        xf = _post_call(x1, oa.reshape(b * t, -1), ob.reshape(b * t, -1), *lw["post"], ln_f,
                        final_norm=(l == len(layers) - 1))
        new_shift.append(pa3[:, -1:, :])
        new_wkv.append(s_a)
        new_gla.append(s_b)
    return xf.reshape(b, t, d), jnp.stack(new_shift), jnp.stack(new_wkv), jnp.stack(new_gla)


def kernel(x_prompt, x_sample, state_shift, state_wkv, state_gla, ln1_g, ffn1_wg, ffn1_wu, ffn1_wd, ln_mix_g, w_in,
           mu_shift, w0, w_lora_up, a0, a_lora_up, g_lora_up, k_k, k_a, r_k, lnx_w, lnx_b, gk_up, gk_b, gla_norm_g,
           w_out, ln2_g, ffn2_wg, ffn2_wu, ffn2_wd, ln_f_g):
    depth = ln1_g.shape[0]
    per_layer = (ln1_g, ffn1_wg, ffn1_wu, ffn1_wd, ln_mix_g, w_in, mu_shift, w0, w_lora_up, a0, a_lora_up, g_lora_up,
                 k_k, k_a, r_k.reshape(depth, -1), lnx_w, lnx_b, gk_up, gk_b, gla_norm_g, w_out, ln2_g, ffn2_wg,
                 ffn2_wu, ffn2_wd)
    layers = [_prep_layer(l, *per_layer) for l in range(depth)]
    ln_f = ln_f_g.reshape(1, -1)
    bp = x_prompt.shape[0]
    shift0 = jnp.zeros((depth, bp) + state_shift.shape[2:], F32)
    wkv0 = jnp.zeros((depth, bp) + state_wkv.shape[2:], F32)
    gla0 = jnp.zeros((depth, bp) + state_gla.shape[2:], F32)
    y_p, shift_p, wkv_p, gla_p = _trunk(x_prompt, shift0, wkv0, gla0, layers, ln_f)
    y_s, shift_s, wkv_s, gla_s = _trunk(x_sample, state_shift, state_wkv, state_gla, layers, ln_f)
    return (y_p, y_s, shift_p, wkv_p, gla_p, shift_s, wkv_s, gla_s)
```

```python
import functools
import math

import jax
import jax.numpy as jnp
from jax import lax
from jax.experimental import pallas as pl
from jax.experimental.pallas import tpu as pltpu

F32 = jnp.float32
BF16 = jnp.bfloat16
HIGHEST = lax.Precision.HIGHEST

EPS = 1e-6
LNX_EPS = 64e-5
A_HEAD = 64
A_W_RANK = 32
A_A_RANK = 32
A_G_RANK = 64
B_HEADS = 4
B_GATE_RANK = 16
B_TAU = 16.0
GLA_CHUNK = 64
SUB = 16
LANE = 128
VMEM_LIMIT = 56 * 1024 * 1024
TOKEN_TILE = 256
RWKV_CHUNKS_PER_STEP = 4

NN = ((1,), (0,))
NT = ((1,), (1,))
TN = ((0,), (0,))

PASSES = dict(gram=1, inv=1, pq=1, gh=1, mn=1, chain=1, out=1, gla_a=1, gla_o=1, gla_s=1)


def _dot(a, b, precision=None):
    return jnp.dot(a, b, preferred_element_type=F32, precision=precision)


def _split2(x):
    hi = x.astype(BF16)
    return hi, (x - hi.astype(F32)).astype(BF16)


def _mm(a, b, passes, dims=NN):
    dg = lambda x, y, p=None: lax.dot_general(x, y, (dims, ((), ())), preferred_element_type=F32, precision=p)
    if passes == 6:
        return dg(a, b, HIGHEST)
    if passes == 1:
        return dg(a.astype(BF16), b.astype(BF16))
    a_hi, a_lo = _split2(a)
    b_hi, b_lo = _split2(b)
    return dg(a_hi, b_hi) + (dg(a_hi, b_lo) + dg(a_lo, b_hi))


def _split3(x):
    hi = x.astype(BF16)
    r1 = x - hi.astype(F32)
    mid = r1.astype(BF16)
    lo = (r1 - mid.astype(F32)).astype(BF16)
    return hi, mid, lo


def _dot_ones_lhs(ones_bf16, x):
    hi, lo = _split2(x)
    return _dot(ones_bf16, hi) + _dot(ones_bf16, lo)


def _dot_ones_rhs(x, ones_bf16):
    hi, lo = _split2(x)
    return _dot(hi, ones_bf16) + _dot(lo, ones_bf16)


def _transpose_sq(x, eye_bf16):
    dg = lambda piece: lax.dot_general(piece, eye_bf16, (TN, ((), ())), preferred_element_type=F32)
    hi, mid, lo = _split3(x)
    return dg(hi) + dg(mid) + dg(lo)


def _softplus(x):
    return jnp.maximum(x, 0.0) + jnp.log(1.0 + jnp.exp(-jnp.abs(x)))


def _rmsnorm(x, g):
    return x * lax.rsqrt(jnp.mean(x * x, axis=-1, keepdims=True) + EPS) * g


def _swiglu_half_step(x, ln_g, wg_ref, wu_ref, wd_ref):
    h = _rmsnorm(x, ln_g).astype(BF16)
    g = _dot(h, wg_ref[...])
    u = _dot(h, wu_ref[...])
    act = (g * jax.nn.sigmoid(g) * u).astype(BF16)
    return x + 0.5 * _dot(act, wd_ref[...])


def _pre_kernel(x_ref, ln1_ref, wg_ref, wu_ref, wd_ref, lnm_ref, wia_ref, wib_ref, x1_ref, pa_ref, pb_ref):
    x1 = _swiglu_half_step(x_ref[...], ln1_ref[...], wg_ref, wu_ref, wd_ref)
    x1_ref[...] = x1
    h = _rmsnorm(x1, lnm_ref[...]).astype(BF16)
    pa_ref[...] = _dot(h, wia_ref[...])
    pb_ref[...] = _dot(h, wib_ref[...])


def _resident(shape):
    return pl.BlockSpec(shape, lambda *_: (0,) * len(shape), pipeline_mode=pl.Buffered(1))


def _token_tile(n):
    tm = min(TOKEN_TILE, n)
    assert n % tm == 0
    return tm


def _pre_call(x, ln1, wg, wu, wd, lnm, wia, wib):
    n, d = x.shape
    tm = _token_tile(n)
    tok = lambda w: pl.BlockSpec((tm, w), lambda i: (i, 0))
    return pl.pallas_call(
        _pre_kernel,
        grid=(n // tm,),
        in_specs=[tok(d), _resident(ln1.shape), _resident(wg.shape), _resident(wu.shape), _resident(wd.shape),
                  _resident(lnm.shape), _resident(wia.shape), _resident(wib.shape)],
        out_specs=[tok(d), tok(wia.shape[1]), tok(wib.shape[1])],
        out_shape=[jax.ShapeDtypeStruct((n, d), F32), jax.ShapeDtypeStruct((n, wia.shape[1]), F32),
                   jax.ShapeDtypeStruct((n, wib.shape[1]), F32)],
        compiler_params=pltpu.CompilerParams(dimension_semantics=("arbitrary",), vmem_limit_bytes=VMEM_LIMIT),
        name="pre_ffn_inproj",
    )(x, ln1, wg, wu, wd, lnm, wia, wib)


def _post_kernel(final_norm, x1_ref, oa_ref, ob_ref, woa_ref, wob_ref, ln2_ref, wg_ref, wu_ref, wd_ref, lnf_ref,
                 y_ref):
    x2 = x1_ref[...] + _dot(oa_ref[...].astype(BF16), woa_ref[...]) + _dot(ob_ref[...].astype(BF16), wob_ref[...])
    x3 = _swiglu_half_step(x2, ln2_ref[...], wg_ref, wu_ref, wd_ref)
    y_ref[...] = _rmsnorm(x3, lnf_ref[...]) if final_norm else x3


def _post_call(x1, oa, ob, woa, wob, ln2, wg, wu, wd, lnf, final_norm):
    n, d = x1.shape
    tm = _token_tile(n)
    tok = lambda w: pl.BlockSpec((tm, w), lambda i: (i, 0))
    return pl.pallas_call(
        functools.partial(_post_kernel, final_norm),
        grid=(n // tm,),
        in_specs=[tok(d), tok(oa.shape[1]), tok(ob.shape[1]), _resident(woa.shape), _resident(wob.shape),
                  _resident(ln2.shape), _resident(wg.shape), _resident(wu.shape), _resident(wd.shape),
                  _resident(lnf.shape)],
        out_specs=tok(d),
        out_shape=jax.ShapeDtypeStruct((n, d), F32),
        compiler_params=pltpu.CompilerParams(dimension_semantics=("arbitrary",), vmem_limit_bytes=VMEM_LIMIT),
        name="post_outproj_ffn",
    )(x1, oa, ob, woa, wob, ln2, wg, wu, wd, lnf)


def _iota2(n, m):
    return lax.broadcasted_iota(jnp.int32, (n, m), 0), lax.broadcasted_iota(jnp.int32, (n, m), 1)


def _rwkv_kernel(c_len, n_sub, pa_ref, shift0_ref, wkv0_ref, mu_ref, w0_ref, wup_ref, a0_ref, aup_ref, gup_ref,
                 kk_ref, ka_ref, rk_ref, lnw_ref, lnb_ref, eseg_ref, o_ref, wkv_ref, prev_ref, st_ref):
    step = pl.program_id(1)
    n_heads = wkv_ref.shape[1]
    n_pairs = n_heads // 2
    d_a = n_heads * A_HEAD
    tb = c_len * n_sub
    c2 = 2 * c_len
    pp = PASSES

    r64, c64 = _iota2(A_HEAD, A_HEAD)
    eye64 = jnp.where(r64 == c64, 1.0, 0.0).astype(BF16)

    @pl.when(step == 0)
    def _():
        prev_ref[...] = shift0_ref[0]
        zero = jnp.zeros((A_HEAD, A_HEAD), F32)
        for j in range(n_pairs):
            even_t = _transpose_sq(wkv0_ref[0, 2 * j], eye64)
            odd_t = _transpose_sq(wkv0_ref[0, 2 * j + 1], eye64)
            st_ref[j] = jnp.concatenate([jnp.concatenate([even_t, zero], axis=1),
                                         jnp.concatenate([zero, odd_t], axis=1)], axis=0)

    p = pa_ref[0]
    row1 = lax.broadcasted_iota(jnp.int32, (tb, 1), 0)
    shifted = jnp.where(row1 == 0, prev_ref[...], pltpu.roll(p, 1, axis=0))
    prev_ref[...] = p[tb - 1:tb, :]
    xs = p + (shifted - p) * mu_ref[...]
    r = xs[:, 0:d_a]
    k = xs[:, d_a:2 * d_a]
    v = xs[:, 2 * d_a:3 * d_a]
    lora_in = xs[:, 3 * d_a:3 * d_a + LANE]

    lw = _dot(jnp.tanh(lora_in).astype(BF16), wup_ref[...])
    la = _dot(lora_in.astype(BF16), aup_ref[...])
    g = _dot(jax.nn.sigmoid(lora_in).astype(BF16), gup_ref[...])
    w_log = -_softplus(-(w0_ref[...] + lw)) - 0.5
    logw = -jnp.exp(w_log)
    a = jax.nn.sigmoid(a0_ref[...] + la)

    eseg = eseg_ref[...]
    kk = k * kk_ref[...]
    kk_norm = jnp.sqrt(_dot_ones_rhs(kk * kk, eseg))
    kk = kk / jnp.maximum(kk_norm, 1e-12)
    kmod = k * (1.0 + (a - 1.0) * ka_ref[...])
    kka = kk * a

    rt, ct = _iota2(tb, tb)
    tri = jnp.where((ct <= rt) & (ct >= (rt & -c_len)), 1.0, 0.0).astype(BF16)
    gc = _dot_ones_lhs(tri, logw)
    a_hat = -kk * jnp.exp(gc - logw)
    r_til = r * jnp.exp(gc)
    e_inv = jnp.exp(-gc)
    k_hat = kmod * e_inv
    b_hat = kka * e_inv

    rr, cc = _iota2(c2, c2)
    in_block = cc >= (rr & -c_len)
    strict = (cc < rr) & in_block
    lower = (cc <= rr) & in_block
    eye2 = jnp.where(rr == cc, 1.0, 0.0).astype(F32)
    r128, c128 = _iota2(LANE, LANE)
    eye128 = jnp.where(r128 == c128, 1.0, 0.0).astype(F32)
    half0 = lax.broadcasted_iota(jnp.int32, (1, LANE), 1) < A_HEAD
    same_head = (r128 < A_HEAD) == (c128 < A_HEAD)
    n_double = int(math.log2(c_len))
    assert 2 ** n_double == c_len and n_double >= 2

    def stack(x):
        return jnp.concatenate([jnp.where(half0, x, 0.0), jnp.where(half0, 0.0, x)], axis=0)

    def dup(x):
        return jnp.concatenate([x, x], axis=0)

    def cat(xs, axis):
        return jnp.concatenate(xs, axis=axis)

    chains = [(i, j) for i in range(n_sub) for j in range(n_pairs)]
    tile = lambda x, i, j: x[i * c_len:(i + 1) * c_len, j * LANE:(j + 1) * LANE]
    each = lambda f: [f(c) for c in range(len(chains))]
    at = lambda x: [tile(x, i, j) for i, j in chains]
    a_t, r_t, k_t, b_t, v_t = at(a_hat), at(r_til), at(k_hat), at(b_hat), at(v)

    gc_last = [gc[(i + 1) * c_len - 1:(i + 1) * c_len, :] for i in range(n_sub)]
    e_end = cat([jnp.exp(gc_last[i] - gc[i * c_len:(i + 1) * c_len]) for i in range(n_sub)], 0)
    kbar_t, bbar_t = at(kmod * e_end), at(kka * e_end)
    w_end = [jnp.exp(gc_last[i])[:, j * LANE:(j + 1) * LANE] for i, j in chains]

    a_s = each(lambda c: stack(a_t[c]))
    xa = each(lambda c: _mm(a_s[c], cat([dup(b_t[c]), dup(k_t[c])], 0), pp["gram"], NT))
    xr = each(lambda c: _mm(stack(r_t[c]), cat([dup(k_t[c]), dup(b_t[c])], 0), pp["gram"], NT))
    l_mat = each(lambda c: jnp.where(strict, xa[c][:, :c2], 0.0))
    a_ak = each(lambda c: jnp.where(strict, xa[c][:, c2:], 0.0))
    a_rk_rb = each(lambda c: cat([jnp.where(lower, xr[c][:, :c2], 0.0), jnp.where(lower, xr[c][:, c2:], 0.0)], 1))
    t_inv = each(lambda c: eye2 + l_mat[c])
    pw = each(lambda c: _mm(l_mat[c], l_mat[c], pp["inv"]))
    for _ in range(1, n_double - 1):
        y = each(lambda c: _mm(pw[c], cat([pw[c], t_inv[c]], 1), pp["inv"]))
        t_inv = each(lambda c: t_inv[c] + y[c][:, c2:])
        pw = each(lambda c: y[c][:, :c2])
    t_inv = each(lambda c: t_inv[c] + _mm(pw[c], t_inv[c], pp["inv"]))
    a_ak_v = each(lambda c: _mm(a_ak[c], dup(v_t[c]), pp["pq"]))
    pq = each(lambda c: _mm(t_inv[c], cat([a_s[c], a_ak_v[c]], 1), pp["pq"]))
    rhs = each(lambda c: cat([cat([dup(v_t[c]), jnp.zeros((c2, LANE), F32)], 1),
                              cat([pq[c][:, LANE:], pq[c][:, :LANE]], 1)], 0))
    hg = each(lambda c: _mm(a_rk_rb[c], rhs[c], pp["gh"]))
    nm = each(lambda c: _mm(cat([stack(kbar_t[c]), stack(bbar_t[c])], 0), rhs[c], pp["mn"], TN))
    g_pair = each(lambda c: r_t[c] + (hg[c][:c_len, LANE:] + hg[c][c_len:, LANE:]))
    h_pair = each(lambda c: jnp.where(half0, hg[c][:c_len, :LANE], hg[c][c_len:, :LANE]))
    n_t = each(lambda c: jnp.where(same_head, nm[c][:, :LANE], 0.0))
    m_t = each(lambda c: eye128 * w_end[c] + nm[c][:, LANE:])

    st = [st_ref[j] for j in range(n_pairs)]
    out_rows = []
    for i in range(n_sub):
        out_tiles = []
        for j in range(n_pairs):
            c = i * n_pairs + j
            both = _mm(cat([m_t[c], g_pair[c]], 0), st[j], pp["chain"])
            out_tiles.append(both[LANE:] + h_pair[c])
            st[j] = both[:LANE] + n_t[c]
        out_rows.append(cat(out_tiles, 1))
    for j in range(n_pairs):
        st_ref[j] = st[j]
    o = jnp.concatenate(out_rows, axis=0) if n_sub > 1 else out_rows[0]

    inv_n = 1.0 / A_HEAD
    mean = _dot_ones_rhs(o, eseg) * inv_n
    cen = o - mean
    var = _dot_ones_rhs(cen * cen, eseg) * inv_n
    on = cen * lax.rsqrt(var + LNX_EPS) * lnw_ref[...] + lnb_ref[...]
    bonus = _dot_ones_rhs(r * kmod * rk_ref[...], eseg) * v
    o_ref[0] = (on + bonus) * g

    @pl.when(step == pl.num_programs(1) - 1)
    def _():
        for j in range(n_pairs):
            st = st_ref[j]
            wkv_ref[0, 2 * j] = _transpose_sq(st[:A_HEAD, :A_HEAD], eye64)
            wkv_ref[0, 2 * j + 1] = _transpose_sq(st[A_HEAD:, A_HEAD:], eye64)


def _rwkv_call(pa, shift0, wkv0, c_len, mu, w0, wup, a0, aup, gup, k_k, k_a, r_k, lnw, lnb, eseg):
    b, t, a_cols = pa.shape
    n_heads = wkv0.shape[1]
    d_a = n_heads * A_HEAD
    n_chunks = t // c_len
    n_sub = math.gcd(n_chunks, RWKV_CHUNKS_PER_STEP)
    tb = c_len * n_sub
    seq = lambda w: pl.BlockSpec((1, tb, w), lambda i, j: (i, j, 0))
    per_b = lambda shp: pl.BlockSpec((1,) + shp, lambda i, j: (i,) + (0,) * len(shp))
    consts = [mu, w0, wup, a0, aup, gup, k_k, k_a, r_k, lnw, lnb, eseg]
    return pl.pallas_call(
        functools.partial(_rwkv_kernel, c_len, n_sub),
        grid=(b, n_chunks // n_sub),
        in_specs=[seq(a_cols), per_b((1, a_cols)), per_b(wkv0.shape[1:])] + [_resident(w.shape) for w in consts],
        out_specs=[seq(d_a), per_b(wkv0.shape[1:])],
        out_shape=[jax.ShapeDtypeStruct((b, t, d_a), F32), jax.ShapeDtypeStruct(wkv0.shape, F32)],
        scratch_shapes=[pltpu.VMEM((1, a_cols), F32), pltpu.VMEM((n_heads // 2, LANE, LANE), F32)],
        compiler_params=pltpu.CompilerParams(dimension_semantics=("arbitrary", "arbitrary"),
                                             vmem_limit_bytes=VMEM_LIMIT),
        name="rwkv7_mixer",
    )(pa, shift0, wkv0, *consts)


def _gla_kernel(c_len, pb_ref, gla0_ref, gkup_ref, gkb_ref, ng_ref, o_ref, st_ref):
    chunk = pl.program_id(1)
    n_heads = st_ref.shape[1]
    dk, dv = st_ref.shape[2], st_ref.shape[3]
    nk = n_heads * dk
    nv = n_heads * dv
    pp = PASSES

    @pl.when(chunk == 0)
    def _():
        st_ref[...] = gla0_ref[...]

    p = pb_ref[0]
    q = p[:, 0:nk] * (dk ** -0.5)
    k = p[:, nk:2 * nk]
    v = p[:, 2 * nk:2 * nk + nv]
    og = p[:, 2 * nk + nv:2 * nk + 2 * nv]
    xgk = p[:, 2 * nk + 2 * nv:2 * nk + 2 * nv + LANE]
    z = _dot(xgk.astype(BF16), gkup_ref[...]) + gkb_ref[...]
    log_a = -_softplus(-z) / B_TAU

    row, col = _iota2(c_len, c_len)
    sub = min(SUB, c_len)
    n_sub = c_len // sub
    tri = jnp.where(row >= col, 1.0, 0.0).astype(BF16)
    b = _dot_ones_lhs(tri, log_a)
    if n_sub > 1:
        tri_start = jnp.where(col < (row & -sub), 1.0, 0.0).astype(BF16)
        b_start = _dot_ones_lhs(tri_start, log_a)
    else:
        b_start = jnp.zeros_like(b)
    b_last = b[c_len - 1:c_len, :]
    q_inter = q * jnp.exp(b)
    q_local = q * jnp.exp(b - b_start)
    k_state = k * jnp.exp(b_last - b)
    row1 = lax.broadcasted_iota(jnp.int32, (c_len, 1), 0)
    k_local = []
    for i in range(n_sub):
        ref_i = b_start[i * sub:i * sub + 1, :]
        expo = jnp.where(row1 < (i + 1) * sub, ref_i - b, -1e30)
        k_local.append(k * jnp.exp(expo))

    log_a_pieces = _split3(log_a)
    ones_cv = jnp.ones((c_len, dv), BF16)
    outs = []
    for h in range(n_heads):
        ks = slice(h * dk, (h + 1) * dk)
        vh = v[:, h * dv:(h + 1) * dv]
        blocks = []
        for i in range(n_sub):
            blocks.append(_mm(q_local[i * sub:(i + 1) * sub, ks], k_local[i][:, ks], pp["gla_a"], NT))
        a_mat = jnp.concatenate(blocks, axis=0) if n_sub > 1 else blocks[0]
        a_mat = jnp.where(row >= col, a_mat, 0.0)
        s0 = st_ref[0, h]
        o_h = _mm(a_mat, vh, pp["gla_o"]) + _mm(q_inter[:, ks], s0, pp["gla_o"])
        dec = jnp.exp(sum(lax.dot_general(piece[:, ks], ones_cv, (TN, ((), ())), preferred_element_type=F32)
                          for piece in log_a_pieces))
        st_ref[0, h] = dec * s0 + _mm(k_state[:, ks], vh, pp["gla_s"], TN)
        o_h = o_h * lax.rsqrt(jnp.mean(o_h * o_h, axis=-1, keepdims=True) + EPS) * ng_ref[...]
        og_h = og[:, h * dv:(h + 1) * dv]
        outs.append(o_h * (og_h * jax.nn.sigmoid(og_h)))
    o_ref[0] = jnp.concatenate(outs, axis=-1)


def _gla_call(pb, gla0, c_len, gkup, gkb, ng):
    b, t, cols = pb.shape
    n_heads, dk, dv = gla0.shape[1:]
    seq = lambda w: pl.BlockSpec((1, c_len, w), lambda i, j: (i, j, 0))
    per_b = lambda shp: pl.BlockSpec((1,) + shp, lambda i, j: (i,) + (0,) * len(shp))
    consts = [gkup, gkb, ng]
    return pl.pallas_call(
        functools.partial(_gla_kernel, c_len),
        grid=(b, t // c_len),
        in_specs=[seq(cols), per_b(gla0.shape[1:])] + [_resident(w.shape) for w in consts],
        out_specs=[seq(n_heads * dv), per_b(gla0.shape[1:])],
        out_shape=[jax.ShapeDtypeStruct((b, t, n_heads * dv), F32), jax.ShapeDtypeStruct(gla0.shape, F32)],
        compiler_params=pltpu.CompilerParams(dimension_semantics=("arbitrary", "arbitrary"),
                                             vmem_limit_bytes=VMEM_LIMIT),
        name="gla_mixer",
    )(pb, gla0, *consts)


def _chunk_len(t):
    return GLA_CHUNK if t % GLA_CHUNK == 0 else t


def _pad_rows(w, start, total):
    return jnp.zeros((total, w.shape[1]), w.dtype).at[start:start + w.shape[0]].set(w)


def _prep_layer(l, ln1_g, ffn1_wg, ffn1_wu, ffn1_wd, ln_mix_g, w_in, mu_shift, w0, w_lora_up, a0, a_lora_up,
                g_lora_up, k_k, k_a, r_k, lnx_w, lnx_b, gk_up, gk_b, gla_norm_g, w_out, ln2_g, ffn2_wg, ffn2_wu,
                ffn2_wd):
    d_a = w0.shape[1]
    a_cols = mu_shift.shape[1]
    nk = gk_b.shape[1]
    d_b = w_out.shape[1] - d_a
    row = lambda x: x[l].reshape(1, -1)
    wi = w_in[l]
    wi_b = wi[:, a_cols:]
    q_k_v = wi_b[:, :2 * nk + d_b]
    xgk_w = wi_b[:, 2 * nk + d_b:2 * nk + d_b + B_GATE_RANK]
    og_w = wi_b[:, 2 * nk + d_b + B_GATE_RANK:]
    pad = jnp.zeros((wi.shape[0], LANE - B_GATE_RANK), wi.dtype)
    wib = jnp.concatenate([q_k_v, og_w, xgk_w, pad], axis=1)
    eseg = jnp.kron(jnp.eye(d_a // A_HEAD, dtype=F32), jnp.ones((A_HEAD, A_HEAD), F32)).astype(BF16)
    return dict(
        pre=(row(ln1_g), ffn1_wg[l].astype(BF16), ffn1_wu[l].astype(BF16), ffn1_wd[l].astype(BF16), row(ln_mix_g),
             wi[:, :a_cols].astype(BF16), wib.astype(BF16)),
        rwkv=(row(mu_shift), row(w0), _pad_rows(w_lora_up[l], 0, LANE).astype(BF16), row(a0),
              _pad_rows(a_lora_up[l], A_W_RANK, LANE).astype(BF16),
              _pad_rows(g_lora_up[l], A_W_RANK + A_A_RANK, LANE).astype(BF16),
              row(k_k), row(k_a), row(r_k), row(lnx_w), row(lnx_b), eseg),
        gla=(_pad_rows(gk_up[l], 0, LANE).astype(BF16), row(gk_b), row(gla_norm_g)),
        post=(w_out[l, :d_a].astype(BF16), w_out[l, d_a:].astype(BF16), row(ln2_g), ffn2_wg[l].astype(BF16),
              ffn2_wu[l].astype(BF16), ffn2_wd[l].astype(BF16)),
    )


def _trunk(x, shift, wkv, gla, layers, ln_f):
    b, t, d = x.shape
    c_len = _chunk_len(t)
    xf = x.reshape(b * t, d)
    new_shift, new_wkv, new_gla = [], [], []
    for l, lw in enumerate(layers):
        x1, pa, pb = _pre_call(xf, *lw["pre"])
        pa3 = pa.reshape(b, t, -1)
        oa, s_a = _rwkv_call(pa3, shift[l], wkv[l], c_len, *lw["rwkv"])
        ob, s_b = _gla_call(pb.reshape(b, t, -1), gla[l], c_len, *lw["gla"])
        xf = _post_call(x1, oa.reshape(b * t, -1), ob.reshape(b * t, -1), *lw["post"], ln_f,
                        final_norm=(l == len(layers) - 1))
        new_shift.append(pa3[:, -1:, :])
        new_wkv.append(s_a)
        new_gla.append(s_b)
    return xf.reshape(b, t, d), jnp.stack(new_shift), jnp.stack(new_wkv), jnp.stack(new_gla)


def kernel(x_prompt, x_sample, state_shift, state_wkv, state_gla, ln1_g, ffn1_wg, ffn1_wu, ffn1_wd, ln_mix_g, w_in,
           mu_shift, w0, w_lora_up, a0, a_lora_up, g_lora_up, k_k, k_a, r_k, lnx_w, lnx_b, gk_up, gk_b, gla_norm_g,
           w_out, ln2_g, ffn2_wg, ffn2_wu, ffn2_wd, ln_f_g):
    depth = ln1_g.shape[0]
    per_layer = (ln1_g, ffn1_wg, ffn1_wu, ffn1_wd, ln_mix_g, w_in, mu_shift, w0, w_lora_up, a0, a_lora_up, g_lora_up,
                 k_k, k_a, r_k.reshape(depth, -1), lnx_w, lnx_b, gk_up, gk_b, gla_norm_g, w_out, ln2_g, ffn2_wg,
                 ffn2_wu, ffn2_wd)
    layers = [_prep_layer(l, *per_layer) for l in range(depth)]
    ln_f = ln_f_g.reshape(1, -1)
    bp = x_prompt.shape[0]
    shift0 = jnp.zeros((depth, bp) + state_shift.shape[2:], F32)
    wkv0 = jnp.zeros((depth, bp) + state_wkv.shape[2:], F32)
    gla0 = jnp.zeros((depth, bp) + state_gla.shape[2:], F32)
    y_p, shift_p, wkv_p, gla_p = _trunk(x_prompt, shift0, wkv0, gla0, layers, ln_f)
    y_s, shift_s, wkv_s, gla_s = _trunk(x_sample, state_shift, state_wkv, state_gla, layers, ln_f)
    return (y_p, y_s, shift_p, wkv_p, gla_p, shift_s, wkv_s, gla_s)
```

```python
import functools
import math

import jax
import jax.numpy as jnp
from jax import lax
from jax.experimental import pallas as pl
from jax.experimental.pallas import tpu as pltpu

F32 = jnp.float32
BF16 = jnp.bfloat16
HIGHEST = lax.Precision.HIGHEST

EPS = 1e-6
LNX_EPS = 64e-5
A_HEAD = 64
A_W_RANK = 32
A_A_RANK = 32
A_G_RANK = 64
B_HEADS = 4
B_GATE_RANK = 16
B_TAU = 16.0
GLA_CHUNK = 64
SUB = 16
LANE = 128
VMEM_LIMIT = 56 * 1024 * 1024
TOKEN_TILE = 256
RWKV_CHUNKS_PER_STEP = 4
GLA_CHUNKS_PER_STEP = 4

NN = ((1,), (0,))
NT = ((1,), (1,))
TN = ((0,), (0,))

PASSES = dict(gram=1, inv=1, pq=1, gh=1, mn=1, chain=1, out=1, gla_a=1, gla_o=1, gla_s=1)


def _dot(a, b, precision=None):
    return jnp.dot(a, b, preferred_element_type=F32, precision=precision)


def _split2(x):
    hi = x.astype(BF16)
    return hi, (x - hi.astype(F32)).astype(BF16)


def _mm(a, b, passes, dims=NN):
    dg = lambda x, y, p=None: lax.dot_general(x, y, (dims, ((), ())), preferred_element_type=F32, precision=p)
    if passes == 6:
        return dg(a, b, HIGHEST)
    if passes == 1:
        return dg(a.astype(BF16), b.astype(BF16))
    a_hi, a_lo = _split2(a)
    b_hi, b_lo = _split2(b)
    return dg(a_hi, b_hi) + (dg(a_hi, b_lo) + dg(a_lo, b_hi))


def _split3(x):
    hi = x.astype(BF16)
    r1 = x - hi.astype(F32)
    mid = r1.astype(BF16)
    lo = (r1 - mid.astype(F32)).astype(BF16)
    return hi, mid, lo


def _dot_ones_lhs(ones_bf16, x):
    hi, lo = _split2(x)
    return _dot(ones_bf16, hi) + _dot(ones_bf16, lo)


def _dot_ones_rhs(x, ones_bf16):
    hi, lo = _split2(x)
    return _dot(hi, ones_bf16) + _dot(lo, ones_bf16)


def _transpose_sq(x, eye_bf16):
    dg = lambda piece: lax.dot_general(piece, eye_bf16, (TN, ((), ())), preferred_element_type=F32)
    hi, mid, lo = _split3(x)
    return dg(hi) + dg(mid) + dg(lo)


def _softplus(x):
    return jnp.maximum(x, 0.0) + jnp.log(1.0 + jnp.exp(-jnp.abs(x)))


def _rmsnorm(x, g):
    return x * lax.rsqrt(jnp.mean(x * x, axis=-1, keepdims=True) + EPS) * g


def _swiglu_half_step(x, ln_g, wg_ref, wu_ref, wd_ref):
    h = _rmsnorm(x, ln_g).astype(BF16)
    g = _dot(h, wg_ref[...])
    u = _dot(h, wu_ref[...])
    act = (g * jax.nn.sigmoid(g) * u).astype(BF16)
    return x + 0.5 * _dot(act, wd_ref[...])


def _pre_kernel(x_ref, ln1_ref, wg_ref, wu_ref, wd_ref, lnm_ref, wia_ref, wib_ref, x1_ref, pa_ref, pb_ref):
    x1 = _swiglu_half_step(x_ref[...], ln1_ref[...], wg_ref, wu_ref, wd_ref)
    x1_ref[...] = x1
    h = _rmsnorm(x1, lnm_ref[...]).astype(BF16)
    pa_ref[...] = _dot(h, wia_ref[...])
    pb_ref[...] = _dot(h, wib_ref[...])


def _resident(shape):
    return pl.BlockSpec(shape, lambda *_: (0,) * len(shape), pipeline_mode=pl.Buffered(1))


def _token_tile(n):
    tm = min(TOKEN_TILE, n)
    assert n % tm == 0
    return tm


def _pre_call(x, ln1, wg, wu, wd, lnm, wia, wib):
    n, d = x.shape
    tm = _token_tile(n)
    tok = lambda w: pl.BlockSpec((tm, w), lambda i: (i, 0))
    return pl.pallas_call(
        _pre_kernel,
        grid=(n // tm,),
        in_specs=[tok(d), _resident(ln1.shape), _resident(wg.shape), _resident(wu.shape), _resident(wd.shape),
                  _resident(lnm.shape), _resident(wia.shape), _resident(wib.shape)],
        out_specs=[tok(d), tok(wia.shape[1]), tok(wib.shape[1])],
        out_shape=[jax.ShapeDtypeStruct((n, d), F32), jax.ShapeDtypeStruct((n, wia.shape[1]), F32),
                   jax.ShapeDtypeStruct((n, wib.shape[1]), F32)],
        compiler_params=pltpu.CompilerParams(dimension_semantics=("arbitrary",), vmem_limit_bytes=VMEM_LIMIT),
        name="pre_ffn_inproj",
    )(x, ln1, wg, wu, wd, lnm, wia, wib)


def _post_kernel(final_norm, x1_ref, oa_ref, ob_ref, woa_ref, wob_ref, ln2_ref, wg_ref, wu_ref, wd_ref, lnf_ref,
                 y_ref):
    x2 = x1_ref[...] + _dot(oa_ref[...].astype(BF16), woa_ref[...]) + _dot(ob_ref[...].astype(BF16), wob_ref[...])
    x3 = _swiglu_half_step(x2, ln2_ref[...], wg_ref, wu_ref, wd_ref)
    y_ref[...] = _rmsnorm(x3, lnf_ref[...]) if final_norm else x3


def _post_call(x1, oa, ob, woa, wob, ln2, wg, wu, wd, lnf, final_norm):
    n, d = x1.shape
    tm = _token_tile(n)
    tok = lambda w: pl.BlockSpec((tm, w), lambda i: (i, 0))
    return pl.pallas_call(
        functools.partial(_post_kernel, final_norm),
        grid=(n // tm,),
        in_specs=[tok(d), tok(oa.shape[1]), tok(ob.shape[1]), _resident(woa.shape), _resident(wob.shape),
                  _resident(ln2.shape), _resident(wg.shape), _resident(wu.shape), _resident(wd.shape),
                  _resident(lnf.shape)],
        out_specs=tok(d),
        out_shape=jax.ShapeDtypeStruct((n, d), F32),
        compiler_params=pltpu.CompilerParams(dimension_semantics=("arbitrary",), vmem_limit_bytes=VMEM_LIMIT),
        name="post_outproj_ffn",
    )(x1, oa, ob, woa, wob, ln2, wg, wu, wd, lnf)


def _iota2(n, m):
    return lax.broadcasted_iota(jnp.int32, (n, m), 0), lax.broadcasted_iota(jnp.int32, (n, m), 1)


def _rwkv_kernel(c_len, n_sub, pa_ref, shift0_ref, wkv0_ref, mu_ref, w0_ref, wup_ref, a0_ref, aup_ref, gup_ref,
                 kk_ref, ka_ref, rk_ref, lnw_ref, lnb_ref, eseg_ref, o_ref, wkv_ref, prev_ref, st_ref):
    step = pl.program_id(1)
    n_heads = wkv_ref.shape[1]
    n_pairs = n_heads // 2
    d_a = n_heads * A_HEAD
    tb = c_len * n_sub
    c2 = 2 * c_len
    pp = PASSES

    r64, c64 = _iota2(A_HEAD, A_HEAD)
    eye64 = jnp.where(r64 == c64, 1.0, 0.0).astype(BF16)

    @pl.when(step == 0)
    def _():
        prev_ref[...] = shift0_ref[0]
        zero = jnp.zeros((A_HEAD, A_HEAD), F32)
        for j in range(n_pairs):
            even_t = _transpose_sq(wkv0_ref[0, 2 * j], eye64)
            odd_t = _transpose_sq(wkv0_ref[0, 2 * j + 1], eye64)
            st_ref[j] = jnp.concatenate([jnp.concatenate([even_t, zero], axis=1),
                                         jnp.concatenate([zero, odd_t], axis=1)], axis=0)

    p = pa_ref[0]
    row1 = lax.broadcasted_iota(jnp.int32, (tb, 1), 0)
    shifted = jnp.where(row1 == 0, prev_ref[...], pltpu.roll(p, 1, axis=0))
    prev_ref[...] = p[tb - 1:tb, :]
    xs = p + (shifted - p) * mu_ref[...]
    r = xs[:, 0:d_a]
    k = xs[:, d_a:2 * d_a]
    v = xs[:, 2 * d_a:3 * d_a]
    lora_in = xs[:, 3 * d_a:3 * d_a + LANE]

    lw = _dot(jnp.tanh(lora_in).astype(BF16), wup_ref[...])
    la = _dot(lora_in.astype(BF16), aup_ref[...])
    g = _dot(jax.nn.sigmoid(lora_in).astype(BF16), gup_ref[...])
    w_log = -_softplus(-(w0_ref[...] + lw)) - 0.5
    logw = -jnp.exp(w_log)
    a = jax.nn.sigmoid(a0_ref[...] + la)

    eseg = eseg_ref[...]
    kk = k * kk_ref[...]
    kk_norm = jnp.sqrt(_dot_ones_rhs(kk * kk, eseg))
    kk = kk / jnp.maximum(kk_norm, 1e-12)
    kmod = k * (1.0 + (a - 1.0) * ka_ref[...])
    kka = kk * a

    rt, ct = _iota2(tb, tb)
    tri = jnp.where((ct <= rt) & (ct >= (rt & -c_len)), 1.0, 0.0).astype(BF16)
    gc = _dot_ones_lhs(tri, logw)
    a_hat = -kk * jnp.exp(gc - logw)
    r_til = r * jnp.exp(gc)
    e_inv = jnp.exp(-gc)
    k_hat = kmod * e_inv
    b_hat = kka * e_inv

    rr, cc = _iota2(c2, c2)
    in_block = cc >= (rr & -c_len)
    strict = (cc < rr) & in_block
    lower = (cc <= rr) & in_block
    eye2 = jnp.where(rr == cc, 1.0, 0.0).astype(F32)
    r128, c128 = _iota2(LANE, LANE)
    eye128 = jnp.where(r128 == c128, 1.0, 0.0).astype(F32)
    half0 = lax.broadcasted_iota(jnp.int32, (1, LANE), 1) < A_HEAD
    same_head = (r128 < A_HEAD) == (c128 < A_HEAD)
    n_double = int(math.log2(c_len))
    assert 2 ** n_double == c_len and n_double >= 2

    def stack(x):
        return jnp.concatenate([jnp.where(half0, x, 0.0), jnp.where(half0, 0.0, x)], axis=0)

    def dup(x):
        return jnp.concatenate([x, x], axis=0)

    def cat(xs, axis):
        return jnp.concatenate(xs, axis=axis)

    chains = [(i, j) for i in range(n_sub) for j in range(n_pairs)]
    tile = lambda x, i, j: x[i * c_len:(i + 1) * c_len, j * LANE:(j + 1) * LANE]
    each = lambda f: [f(c) for c in range(len(chains))]
    at = lambda x: [tile(x, i, j) for i, j in chains]
    a_t, r_t, k_t, b_t, v_t = at(a_hat), at(r_til), at(k_hat), at(b_hat), at(v)

    gc_last = [gc[(i + 1) * c_len - 1:(i + 1) * c_len, :] for i in range(n_sub)]
    e_end = cat([jnp.exp(gc_last[i] - gc[i * c_len:(i + 1) * c_len]) for i in range(n_sub)], 0)
    kbar_t, bbar_t = at(kmod * e_end), at(kka * e_end)
    w_end = [jnp.exp(gc_last[i])[:, j * LANE:(j + 1) * LANE] for i, j in chains]

    a_s = each(lambda c: stack(a_t[c]))
    xa = each(lambda c: _mm(a_s[c], cat([dup(b_t[c]), dup(k_t[c])], 0), pp["gram"], NT))
    xr = each(lambda c: _mm(stack(r_t[c]), cat([dup(k_t[c]), dup(b_t[c])], 0), pp["gram"], NT))
    l_mat = each(lambda c: jnp.where(strict, xa[c][:, :c2], 0.0))
    a_ak = each(lambda c: jnp.where(strict, xa[c][:, c2:], 0.0))
    a_rk_rb = each(lambda c: cat([jnp.where(lower, xr[c][:, :c2], 0.0), jnp.where(lower, xr[c][:, c2:], 0.0)], 1))
    t_inv = each(lambda c: eye2 + l_mat[c])
    pw = each(lambda c: _mm(l_mat[c], l_mat[c], pp["inv"]))
    for _ in range(1, n_double - 1):
        y = each(lambda c: _mm(pw[c], cat([pw[c], t_inv[c]], 1), pp["inv"]))
        t_inv = each(lambda c: t_inv[c] + y[c][:, c2:])
        pw = each(lambda c: y[c][:, :c2])
    t_inv = each(lambda c: t_inv[c] + _mm(pw[c], t_inv[c], pp["inv"]))
    a_ak_v = each(lambda c: _mm(a_ak[c], dup(v_t[c]), pp["pq"]))
    pq = each(lambda c: _mm(t_inv[c], cat([a_s[c], a_ak_v[c]], 1), pp["pq"]))
    rhs = each(lambda c: cat([cat([dup(v_t[c]), jnp.zeros((c2, LANE), F32)], 1),
                              cat([pq[c][:, LANE:], pq[c][:, :LANE]], 1)], 0))
    hg = each(lambda c: _mm(a_rk_rb[c], rhs[c], pp["gh"]))
    nm = each(lambda c: _mm(cat([stack(kbar_t[c]), stack(bbar_t[c])], 0), rhs[c], pp["mn"], TN))
    g_pair = each(lambda c: r_t[c] + (hg[c][:c_len, LANE:] + hg[c][c_len:, LANE:]))
    h_pair = each(lambda c: jnp.where(half0, hg[c][:c_len, :LANE], hg[c][c_len:, :LANE]))
    n_t = each(lambda c: jnp.where(same_head, nm[c][:, :LANE], 0.0))
    m_t = each(lambda c: eye128 * w_end[c] + nm[c][:, LANE:])

    st = [st_ref[j] for j in range(n_pairs)]
    out_rows = []
    for i in range(n_sub):
        out_tiles = []
        for j in range(n_pairs):
            c = i * n_pairs + j
            both = _mm(cat([m_t[c], g_pair[c]], 0), st[j], pp["chain"])
            out_tiles.append(both[LANE:] + h_pair[c])
            st[j] = both[:LANE] + n_t[c]
        out_rows.append(cat(out_tiles, 1))
    for j in range(n_pairs):
        st_ref[j] = st[j]
    o = jnp.concatenate(out_rows, axis=0) if n_sub > 1 else out_rows[0]

    inv_n = 1.0 / A_HEAD
    mean = _dot_ones_rhs(o, eseg) * inv_n
    cen = o - mean
    var = _dot_ones_rhs(cen * cen, eseg) * inv_n
    on = cen * lax.rsqrt(var + LNX_EPS) * lnw_ref[...] + lnb_ref[...]
    bonus = _dot_ones_rhs(r * kmod * rk_ref[...], eseg) * v
    o_ref[0] = (on + bonus) * g

    @pl.when(step == pl.num_programs(1) - 1)
    def _():
        for j in range(n_pairs):
            st = st_ref[j]
            wkv_ref[0, 2 * j] = _transpose_sq(st[:A_HEAD, :A_HEAD], eye64)
            wkv_ref[0, 2 * j + 1] = _transpose_sq(st[A_HEAD:, A_HEAD:], eye64)


def _rwkv_call(pa, shift0, wkv0, c_len, mu, w0, wup, a0, aup, gup, k_k, k_a, r_k, lnw, lnb, eseg):
    b, t, a_cols = pa.shape
    n_heads = wkv0.shape[1]
    d_a = n_heads * A_HEAD
    n_chunks = t // c_len
    n_sub = math.gcd(n_chunks, RWKV_CHUNKS_PER_STEP)
    tb = c_len * n_sub
    seq = lambda w: pl.BlockSpec((1, tb, w), lambda i, j: (i, j, 0))
    per_b = lambda shp: pl.BlockSpec((1,) + shp, lambda i, j: (i,) + (0,) * len(shp))
    consts = [mu, w0, wup, a0, aup, gup, k_k, k_a, r_k, lnw, lnb, eseg]
    return pl.pallas_call(
        functools.partial(_rwkv_kernel, c_len, n_sub),
        grid=(b, n_chunks // n_sub),
        in_specs=[seq(a_cols), per_b((1, a_cols)), per_b(wkv0.shape[1:])] + [_resident(w.shape) for w in consts],
        out_specs=[seq(d_a), per_b(wkv0.shape[1:])],
        out_shape=[jax.ShapeDtypeStruct((b, t, d_a), F32), jax.ShapeDtypeStruct(wkv0.shape, F32)],
        scratch_shapes=[pltpu.VMEM((1, a_cols), F32), pltpu.VMEM((n_heads // 2, LANE, LANE), F32)],
        compiler_params=pltpu.CompilerParams(dimension_semantics=("arbitrary", "arbitrary"),
                                             vmem_limit_bytes=VMEM_LIMIT),
        name="rwkv7_mixer",
    )(pa, shift0, wkv0, *consts)


def _gla_kernel(c_len, n_chunks, pb_ref, gla0_ref, gkup_ref, gkb_ref, ng_ref, o_ref, st_ref):
    step = pl.program_id(1)
    n_pairs = st_ref.shape[1]
    dk = st_ref.shape[2] // 2
    dv = st_ref.shape[3]
    assert 2 * dk == LANE and dv == LANE
    n_heads = 2 * n_pairs
    nk = n_heads * dk
    nv = n_heads * dv
    tb = c_len * n_chunks
    pp = PASSES

    @pl.when(step == 0)
    def _():
        st_ref[...] = gla0_ref[...]

    def cat(xs, axis):
        return jnp.concatenate(xs, axis=axis)

    p = pb_ref[0]
    q = p[:, 0:nk] * (dk ** -0.5)
    k = p[:, nk:2 * nk]
    v = p[:, 2 * nk:2 * nk + nv]
    og = p[:, 2 * nk + nv:2 * nk + 2 * nv]
    xgk = p[:, 2 * nk + 2 * nv:2 * nk + 2 * nv + LANE]
    z = _dot(xgk.astype(BF16), gkup_ref[...]) + gkb_ref[...]
    log_a = -_softplus(-z) / B_TAU

    sub = min(SUB, c_len)
    n_sub = c_len // sub
    row, col = _iota2(tb, tb)
    in_chunk = col >= (row & -c_len)
    tri = jnp.where((col <= row) & in_chunk, 1.0, 0.0).astype(BF16)
    b = _dot_ones_lhs(tri, log_a)
    if n_sub > 1:
        tri_start = jnp.where((col < (row & -sub)) & in_chunk, 1.0, 0.0).astype(BF16)
        b_start = _dot_ones_lhs(tri_start, log_a)
    else:
        b_start = jnp.zeros_like(b)
    q_inter = q * jnp.exp(b)
    q_local = q * jnp.exp(b - b_start)
    rows_of = lambda x, i: x[i * c_len:(i + 1) * c_len]
    b_last = [b[(i + 1) * c_len - 1:(i + 1) * c_len, :] for i in range(n_chunks)]
    k_state = cat([rows_of(k, i) * jnp.exp(b_last[i] - rows_of(b, i)) for i in range(n_chunks)], 0)
    row1 = lax.broadcasted_iota(jnp.int32, (c_len, 1), 0)
    k_local = []
    for i in range(n_chunks):
        per_sub = []
        for s in range(n_sub):
            ref = b_start[i * c_len + s * sub:i * c_len + s * sub + 1, :]
            expo = jnp.where(row1 < (s + 1) * sub, ref - rows_of(b, i), -1e30)
            per_sub.append(rows_of(k, i) * jnp.exp(expo))
        k_local.append(per_sub)

    half0 = lax.broadcasted_iota(jnp.int32, (1, LANE), 1) < dk
    rc, cc = _iota2(c_len, c_len)
    causal = rc >= cc
    top_rows = lax.broadcasted_iota(jnp.int32, (LANE, 1), 0) < dk
    log_a_hi, log_a_lo = _split2(log_a)
    ones_cv = jnp.ones((c_len, dv), BF16)
    kt = lambda x, i, j: x[i * c_len:(i + 1) * c_len, j * LANE:(j + 1) * LANE]
    vt = lambda x, i, h: x[i * c_len:(i + 1) * c_len, h * dv:(h + 1) * dv]
    pairs = [(i, j) for i in range(n_chunks) for j in range(n_pairs)]
    heads = [(i, h) for i in range(n_chunks) for h in range(n_heads)]

    def a_rows(i, j, s):
        ql = kt(q_local, i, j)[s * sub:(s + 1) * sub]
        lhs = cat([jnp.where(half0, ql, 0.0), jnp.where(half0, 0.0, ql)], 0)
        return _mm(lhs, kt(k_local[i][s], 0, j), pp["gla_a"], NT)
    a_blk = {(i, j): [a_rows(i, j, s) for s in range(n_sub)] for i, j in pairs}
    a_mat = {}
    for i, h in heads:
        blocks = [a_blk[(i, h // 2)][s][(h % 2) * sub:(h % 2 + 1) * sub] for s in range(n_sub)]
        a_mat[(i, h)] = jnp.where(causal, cat(blocks, 0) if n_sub > 1 else blocks[0], 0.0)
    o_intra = {(i, h): _mm(a_mat[(i, h)], vt(v, i, h), pp["gla_o"]) for i, h in heads}
    upd = {}
    dec = {}
    for i, j in pairs:
        kv = _mm(kt(k_state, i, j), v[i * c_len:(i + 1) * c_len, 2 * j * dv:(2 * j + 2) * dv], pp["gla_s"], TN)
        upd[(i, j)] = jnp.where(top_rows, kv[:, :dv], kv[:, dv:])
        col_sum = lambda piece: lax.dot_general(kt(piece, i, j), ones_cv, (TN, ((), ())), preferred_element_type=F32)
        dec[(i, j)] = jnp.exp(col_sum(log_a_hi) + col_sum(log_a_lo))
    states = {}
    for j in range(n_pairs):
        s_cur = st_ref[0, j]
        for i in range(n_chunks):
            states[(i, j)] = s_cur
            s_cur = dec[(i, j)] * s_cur + upd[(i, j)]
        st_ref[0, j] = s_cur
    out_rows = []
    for i in range(n_chunks):
        outs = []
        for h in range(n_heads):
            qi = kt(q_inter, i, h // 2)
            qi = jnp.where(half0, qi, 0.0) if h % 2 == 0 else jnp.where(half0, 0.0, qi)
            o_h = o_intra[(i, h)] + _mm(qi, states[(i, h // 2)], pp["gla_o"])
            o_h = o_h * lax.rsqrt(jnp.mean(o_h * o_h, axis=-1, keepdims=True) + EPS) * ng_ref[...]
            og_h = vt(og, i, h)
            outs.append(o_h * (og_h * jax.nn.sigmoid(og_h)))
        out_rows.append(cat(outs, 1))
    o_ref[0] = cat(out_rows, 0) if n_chunks > 1 else out_rows[0]


def _gla_call(pb, gla0, c_len, gkup, gkb, ng):
    b, t, cols = pb.shape
    n_heads, dk, dv = gla0.shape[1:]
    n_chunks = t // c_len
    per_step = math.gcd(n_chunks, GLA_CHUNKS_PER_STEP)
    tb = c_len * per_step
    st_shape = (n_heads // 2, 2 * dk, dv)
    seq = lambda w: pl.BlockSpec((1, tb, w), lambda i, j: (i, j, 0))
    per_b = lambda shp: pl.BlockSpec((1,) + shp, lambda i, j: (i,) + (0,) * len(shp))
    consts = [gkup, gkb, ng]
    o, st = pl.pallas_call(
        functools.partial(_gla_kernel, c_len, per_step),
        grid=(b, n_chunks // per_step),
        in_specs=[seq(cols), per_b(st_shape)] + [_resident(w.shape) for w in consts],
        out_specs=[seq(n_heads * dv), per_b(st_shape)],
        out_shape=[jax.ShapeDtypeStruct((b, t, n_heads * dv), F32), jax.ShapeDtypeStruct((b,) + st_shape, F32)],
        compiler_params=pltpu.CompilerParams(dimension_semantics=("arbitrary", "arbitrary"),
                                             vmem_limit_bytes=VMEM_LIMIT),
        name="gla_mixer",
    )(pb, gla0.reshape((b,) + st_shape), *consts)
    return o, st.reshape(gla0.shape)


def _chunk_len(t):
    return GLA_CHUNK if t % GLA_CHUNK == 0 else t


def _pad_rows(w, start, total):
    return jnp.zeros((total, w.shape[1]), w.dtype).at[start:start + w.shape[0]].set(w)


def _prep_layer(l, ln1_g, ffn1_wg, ffn1_wu, ffn1_wd, ln_mix_g, w_in, mu_shift, w0, w_lora_up, a0, a_lora_up,
                g_lora_up, k_k, k_a, r_k, lnx_w, lnx_b, gk_up, gk_b, gla_norm_g, w_out, ln2_g, ffn2_wg, ffn2_wu,
                ffn2_wd):
    d_a = w0.shape[1]
    a_cols = mu_shift.shape[1]
    nk = gk_b.shape[1]
    d_b = w_out.shape[1] - d_a
    row = lambda x: x[l].reshape(1, -1)
    wi = w_in[l]
    wi_b = wi[:, a_cols:]
    q_k_v = wi_b[:, :2 * nk + d_b]
    xgk_w = wi_b[:, 2 * nk + d_b:2 * nk + d_b + B_GATE_RANK]
    og_w = wi_b[:, 2 * nk + d_b + B_GATE_RANK:]
    pad = jnp.zeros((wi.shape[0], LANE - B_GATE_RANK), wi.dtype)
    wib = jnp.concatenate([q_k_v, og_w, xgk_w, pad], axis=1)
    eseg = jnp.kron(jnp.eye(d_a // A_HEAD, dtype=F32), jnp.ones((A_HEAD, A_HEAD), F32)).astype(BF16)
    return dict(
        pre=(row(ln1_g), ffn1_wg[l].astype(BF16), ffn1_wu[l].astype(BF16), ffn1_wd[l].astype(BF16), row(ln_mix_g),
             wi[:, :a_cols].astype(BF16), wib.astype(BF16)),
        rwkv=(row(mu_shift), row(w0), _pad_rows(w_lora_up[l], 0, LANE).astype(BF16), row(a0),
              _pad_rows(a_lora_up[l], A_W_RANK, LANE).astype(BF16),
              _pad_rows(g_lora_up[l], A_W_RANK + A_A_RANK, LANE).astype(BF16),
              row(k_k), row(k_a), row(r_k), row(lnx_w), row(lnx_b), eseg),
        gla=(_pad_rows(gk_up[l], 0, LANE).astype(BF16), row(gk_b), row(gla_norm_g)),
        post=(w_out[l, :d_a].astype(BF16), w_out[l, d_a:].astype(BF16), row(ln2_g), ffn2_wg[l].astype(BF16),
              ffn2_wu[l].astype(BF16), ffn2_wd[l].astype(BF16)),
    )


def _trunk(x, shift, wkv, gla, layers, ln_f):
    b, t, d = x.shape
    c_len = _chunk_len(t)
    xf = x.reshape(b * t, d)
    new_shift, new_wkv, new_gla = [], [], []
    for l, lw in enumerate(layers):
        x1, pa, pb = _pre_call(xf, *lw["pre"])
        pa3 = pa.reshape(b, t, -1)
        oa, s_a = _rwkv_call(pa3, shift[l], wkv[l], c_len, *lw["rwkv"])
        ob, s_b = _gla_call(pb.reshape(b, t, -1), gla[l], c_len, *lw["gla"])
        xf = _post_call(x1, oa.reshape(b * t, -1), ob.reshape(b * t, -1), *lw["post"], ln_f,
                        final_norm=(l == len(layers) - 1))
        new_shift.append(pa3[:, -1:, :])
        new_wkv.append(s_a)
        new_gla.append(s_b)
    return xf.reshape(b, t, d), jnp.stack(new_shift), jnp.stack(new_wkv), jnp.stack(new_gla)


def kernel(x_prompt, x_sample, state_shift, state_wkv, state_gla, ln1_g, ffn1_wg, ffn1_wu, ffn1_wd, ln_mix_g, w_in,
           mu_shift, w0, w_lora_up, a0, a_lora_up, g_lora_up, k_k, k_a, r_k, lnx_w, lnx_b, gk_up, gk_b, gla_norm_g,
           w_out, ln2_g, ffn2_wg, ffn2_wu, ffn2_wd, ln_f_g):
    depth = ln1_g.shape[0]
    per_layer = (ln1_g, ffn1_wg, ffn1_wu, ffn1_wd, ln_mix_g, w_in, mu_shift, w0, w_lora_up, a0, a_lora_up, g_lora_up,
                 k_k, k_a, r_k.reshape(depth, -1), lnx_w, lnx_b, gk_up, gk_b, gla_norm_g, w_out, ln2_g, ffn2_wg,
                 ffn2_wu, ffn2_wd)
    layers = [_prep_layer(l, *per_layer) for l in range(depth)]
    ln_f = ln_f_g.reshape(1, -1)
    bp = x_prompt.shape[0]
    shift0 = jnp.zeros((depth, bp) + state_shift.shape[2:], F32)
    wkv0 = jnp.zeros((depth, bp) + state_wkv.shape[2:], F32)
    gla0 = jnp.zeros((depth, bp) + state_gla.shape[2:], F32)
    y_p, shift_p, wkv_p, gla_p = _trunk(x_prompt, shift0, wkv0, gla0, layers, ln_f)
    y_s, shift_s, wkv_s, gla_s = _trunk(x_sample, state_shift, state_wkv, state_gla, layers, ln_f)
    return (y_p, y_s, shift_p, wkv_p, gla_p, shift_s, wkv_s, gla_s)
```

```python
import functools
import math

import jax
import jax.numpy as jnp
from jax import lax
from jax.experimental import pallas as pl
from jax.experimental.pallas import tpu as pltpu

F32 = jnp.float32
BF16 = jnp.bfloat16
HIGHEST = lax.Precision.HIGHEST

EPS = 1e-6
LNX_EPS = 64e-5
A_HEAD = 64
A_W_RANK = 32
A_A_RANK = 32
A_G_RANK = 64
B_HEADS = 4
B_GATE_RANK = 16
B_TAU = 16.0
GLA_CHUNK = 64
SUB = 16
LANE = 128
VMEM_LIMIT = 56 * 1024 * 1024
TOKEN_TILE = 256
RWKV_CHUNKS_PER_STEP = 8
RWKV_CHUNKS_PER_GROUP = 4
GLA_CHUNKS_PER_STEP = 4
HEAD_SUM_TILE = 256

NN = ((1,), (0,))
NT = ((1,), (1,))
TN = ((0,), (0,))


def _dot(a, b, precision=None):
    return jnp.dot(a, b, preferred_element_type=F32, precision=precision)


def _split2(x):
    hi = x.astype(BF16)
    return hi, (x - hi.astype(F32)).astype(BF16)


def _mm(a, b, dims=NN):
    return lax.dot_general(a.astype(BF16), b.astype(BF16), (dims, ((), ())), preferred_element_type=F32)


def _split3(x):
    hi = x.astype(BF16)
    r1 = x - hi.astype(F32)
    mid = r1.astype(BF16)
    lo = (r1 - mid.astype(F32)).astype(BF16)
    return hi, mid, lo


def _dot_ones_lhs(ones_bf16, x):
    hi, lo = _split2(x)
    return _dot(ones_bf16, hi) + _dot(ones_bf16, lo)


def _dot_ones_rhs(x, ones_bf16):
    hi, lo = _split2(x)
    return _dot(hi, ones_bf16) + _dot(lo, ones_bf16)


def _transpose_sq(x, eye_bf16):
    dg = lambda piece: lax.dot_general(piece, eye_bf16, (TN, ((), ())), preferred_element_type=F32)
    hi, mid, lo = _split3(x)
    return dg(hi) + dg(mid) + dg(lo)


def _softplus(x):
    return jnp.maximum(x, 0.0) + jnp.log(1.0 + jnp.exp(-jnp.abs(x)))


def _rmsnorm(x, g):
    return x * lax.rsqrt(jnp.mean(x * x, axis=-1, keepdims=True) + EPS) * g


def _swiglu_half_step(x, ln_g, wg_ref, wu_ref, wd_ref):
    h = _rmsnorm(x, ln_g).astype(BF16)
    g = _dot(h, wg_ref[...])
    u = _dot(h, wu_ref[...])
    act = (g * jax.nn.sigmoid(g) * u).astype(BF16)
    return x + 0.5 * _dot(act, wd_ref[...])


def _pre_kernel(x_ref, ln1_ref, wg_ref, wu_ref, wd_ref, lnm_ref, wia_ref, wib_ref, x1_ref, pa_ref, pb_ref):
    x1 = _swiglu_half_step(x_ref[...], ln1_ref[...], wg_ref, wu_ref, wd_ref)
    x1_ref[...] = x1
    h = _rmsnorm(x1, lnm_ref[...]).astype(BF16)
    pa_ref[...] = _dot(h, wia_ref[...])
    pb_ref[...] = _dot(h, wib_ref[...])


def _resident(shape):
    return pl.BlockSpec(shape, lambda *_: (0,) * len(shape), pipeline_mode=pl.Buffered(1))


def _token_tile(n):
    tm = min(TOKEN_TILE, n)
    assert n % tm == 0
    return tm


def _pre_call(x, ln1, wg, wu, wd, lnm, wia, wib):
    n, d = x.shape
    tm = _token_tile(n)
    tok = lambda w: pl.BlockSpec((tm, w), lambda i: (i, 0))
    return pl.pallas_call(
        _pre_kernel,
        grid=(n // tm,),
        in_specs=[tok(d), _resident(ln1.shape), _resident(wg.shape), _resident(wu.shape), _resident(wd.shape),
                  _resident(lnm.shape), _resident(wia.shape), _resident(wib.shape)],
        out_specs=[tok(d), tok(wia.shape[1]), tok(wib.shape[1])],
        out_shape=[jax.ShapeDtypeStruct((n, d), F32), jax.ShapeDtypeStruct((n, wia.shape[1]), F32),
                   jax.ShapeDtypeStruct((n, wib.shape[1]), F32)],
        compiler_params=pltpu.CompilerParams(dimension_semantics=("arbitrary",), vmem_limit_bytes=VMEM_LIMIT),
        name="pre_ffn_inproj",
    )(x, ln1, wg, wu, wd, lnm, wia, wib)


def _post_kernel(final_norm, x1_ref, oa_ref, ob_ref, woa_ref, wob_ref, ln2_ref, wg_ref, wu_ref, wd_ref, lnf_ref,
                 y_ref):
    x2 = x1_ref[...] + _dot(oa_ref[...].astype(BF16), woa_ref[...]) + _dot(ob_ref[...].astype(BF16), wob_ref[...])
    x3 = _swiglu_half_step(x2, ln2_ref[...], wg_ref, wu_ref, wd_ref)
    y_ref[...] = _rmsnorm(x3, lnf_ref[...]) if final_norm else x3


def _post_call(x1, oa, ob, woa, wob, ln2, wg, wu, wd, lnf, final_norm):
    n, d = x1.shape
    tm = _token_tile(n)
    tok = lambda w: pl.BlockSpec((tm, w), lambda i: (i, 0))
    return pl.pallas_call(
        functools.partial(_post_kernel, final_norm),
        grid=(n // tm,),
        in_specs=[tok(d), tok(oa.shape[1]), tok(ob.shape[1]), _resident(woa.shape), _resident(wob.shape),
                  _resident(ln2.shape), _resident(wg.shape), _resident(wu.shape), _resident(wd.shape),
                  _resident(lnf.shape)],
        out_specs=tok(d),
        out_shape=jax.ShapeDtypeStruct((n, d), F32),
        compiler_params=pltpu.CompilerParams(dimension_semantics=("arbitrary",), vmem_limit_bytes=VMEM_LIMIT),
        name="post_outproj_ffn",
    )(x1, oa, ob, woa, wob, ln2, wg, wu, wd, lnf)


def _iota2(n, m):
    return lax.broadcasted_iota(jnp.int32, (n, m), 0), lax.broadcasted_iota(jnp.int32, (n, m), 1)


def _rwkv_kernel(c_len, n_sub, pa_ref, shift0_ref, wkv0_ref, mu_ref, w0_ref, wup_ref, a0_ref, aup_ref, gup_ref,
                 kk_ref, ka_ref, rk_ref, lnw_ref, lnb_ref, eseg_ref, o_ref, wkv_ref, prev_ref, st_ref):
    step = pl.program_id(1)
    n_heads = wkv_ref.shape[1]
    n_pairs = n_heads // 2
    d_a = n_heads * A_HEAD
    tb = c_len * n_sub
    c2 = 2 * c_len

    r64, c64 = _iota2(A_HEAD, A_HEAD)
    eye64 = jnp.where(r64 == c64, 1.0, 0.0).astype(BF16)

    @pl.when(step == 0)
    def _():
        prev_ref[...] = shift0_ref[0]
        for j in range(n_pairs):
            st_ref[j] = jnp.concatenate([_transpose_sq(wkv0_ref[0, 2 * j], eye64),
                                         _transpose_sq(wkv0_ref[0, 2 * j + 1], eye64)], axis=1)

    def cat(xs, axis):
        return jnp.concatenate(xs, axis=axis)

    eseg = eseg_ref[...]

    def head_sum(x):
        w = eseg.shape[0]
        return cat([_dot(x[:, s:s + w].astype(BF16), eseg) for s in range(0, d_a, w)], 1)

    row1 = lax.broadcasted_iota(jnp.int32, (c_len, 1), 0)
    rt, ct = _iota2(c_len, c_len)
    tri = jnp.where(ct <= rt, 1.0, 0.0).astype(BF16)
    pre = {}

    def prep(i):
        p = pa_ref[0, i * c_len:(i + 1) * c_len, :]
        prev = prev_ref[...] if i == 0 else pa_ref[0, i * c_len - 1:i * c_len, :]
        shifted = jnp.where(row1 == 0, prev, pltpu.roll(p, 1, axis=0))
        xs = p + (shifted - p) * mu_ref[...]
        r = xs[:, 0:d_a]
        k = xs[:, d_a:2 * d_a]
        v = xs[:, 2 * d_a:3 * d_a]
        lora_in = xs[:, 3 * d_a:3 * d_a + LANE]
        yield
        lw = _dot(jnp.tanh(lora_in).astype(BF16), wup_ref[...])
        la = _dot(lora_in.astype(BF16), aup_ref[...])
        g = _dot(jax.nn.sigmoid(lora_in).astype(BF16), gup_ref[...])
        kk = k * kk_ref[...]
        kk_sq = head_sum(kk * kk)
        yield
        w_log = -_softplus(-(w0_ref[...] + lw)) - 0.5
        logw = -jnp.exp(w_log)
        gc = _dot_ones_lhs(tri, logw)
        yield
        a = jax.nn.sigmoid(a0_ref[...] + la)
        kk = kk / jnp.maximum(jnp.sqrt(kk_sq), 1e-12)
        kmod = k * (1.0 + (a - 1.0) * ka_ref[...])
        kka = kk * a
        yield
        gc_last = gc[c_len - 1:c_len, :]
        e_inv = jnp.exp(-gc)
        e_end = jnp.exp(gc_last - gc)
        yield
        pre[i] = dict(a_hat=-kk * jnp.exp(gc - logw), r_til=r * jnp.exp(gc), k_hat=kmod * e_inv, b_hat=kka * e_inv,
                      v=v, kbar=kmod * e_end, bbar=kka * e_end, w_end=jnp.exp(gc_last),
                      bonus=head_sum(r * kmod * rk_ref[...]) * v, g=g)

    rr, cc = _iota2(c_len, c2)
    col_t = cc & (c_len - 1)
    strict = col_t < rr
    lower = col_t <= rr
    eye_tt = jnp.where(col_t == rr, 1.0, 0.0).astype(F32)
    rk, ck = _iota2(A_HEAD, LANE)
    eye_kk = jnp.where((ck & (A_HEAD - 1)) == rk, 1.0, 0.0).astype(F32)
    half_k = lax.broadcasted_iota(jnp.int32, (1, LANE), 1) < A_HEAD
    half_t = lax.broadcasted_iota(jnp.int32, (1, c2), 1) < c_len
    n_double = int(math.log2(c_len))
    assert 2 ** n_double == c_len and n_double >= 2

    def bd(x, half):
        return cat([jnp.where(half, x, 0.0), jnp.where(half, 0.0, x)], 0)

    bd_k = lambda x: bd(x, half_k)
    bd_t = lambda x: bd(x, half_t)

    st = [st_ref[j] for j in range(n_pairs)]
    mixed = {}

    def mix(chunks):
        chains = [(i, j) for i in chunks for j in range(n_pairs)]
        each = lambda f: [f(c) for c in range(len(chains))]
        at = lambda name: [pre[i][name][:, j * LANE:(j + 1) * LANE] for i, j in chains]
        a_t, r_t, k_t, b_t, v_t = at("a_hat"), at("r_til"), at("k_hat"), at("b_hat"), at("v")
        kbar_t, bbar_t, w_end = at("kbar"), at("bbar"), at("w_end")
        bd_b = each(lambda c: bd_k(b_t[c]))
        bd_kh = each(lambda c: bd_k(k_t[c]))
        xa = each(lambda c: _mm(a_t[c], cat([bd_b[c], bd_kh[c]], 0), NT))
        yield
        xr = each(lambda c: _mm(r_t[c], cat([bd_kh[c], bd_b[c]], 0), NT))
        l_mat = each(lambda c: jnp.where(strict, xa[c][:, :c2], 0.0))
        a_ak = each(lambda c: jnp.where(strict, xa[c][:, c2:], 0.0))
        yield
        t_inv = each(lambda c: eye_tt + l_mat[c])
        pw = each(lambda c: _mm(l_mat[c], bd_t(l_mat[c])))
        a_rk_rb = each(lambda c: cat([jnp.where(lower, xr[c][:, :c2], 0.0), jnp.where(lower, xr[c][:, c2:], 0.0)], 1))
        yield
        for _ in range(1, n_double - 1):
            y = each(lambda c: _mm(pw[c], cat([bd_t(pw[c]), bd_t(t_inv[c])], 1)))
            t_inv = each(lambda c: t_inv[c] + y[c][:, c2:])
            pw = each(lambda c: y[c][:, :c2])
            yield
        t_inv = each(lambda c: t_inv[c] + _mm(pw[c], bd_t(t_inv[c])))
        bd_v = each(lambda c: bd_k(v_t[c]))
        a_ak_v = each(lambda c: _mm(a_ak[c], bd_v[c]))
        yield
        pq = each(lambda c: _mm(t_inv[c], cat([bd_k(a_t[c]), bd_k(a_ak_v[c])], 1)))
        p_t = each(lambda c: pq[c][:, :LANE])
        q_t = each(lambda c: pq[c][:, LANE:])
        yield
        hg = each(lambda c: _mm(a_rk_rb[c], cat([cat([bd_v[c], jnp.zeros((c2, LANE), F32)], 1),
                                                 cat([bd_k(q_t[c]), bd_k(p_t[c])], 1)], 0)))
        yield
        nm = each(lambda c: _mm(cat([kbar_t[c], bbar_t[c]], 0),
                                cat([cat([v_t[c], jnp.zeros((c_len, LANE), F32)], 1), cat([q_t[c], p_t[c]], 1)], 0),
                                TN))
        n_sbs = each(lambda c: jnp.where(half_k, nm[c][:A_HEAD, :LANE], nm[c][A_HEAD:, :LANE]))
        m_sbs = each(lambda c: jnp.where(half_k, nm[c][:A_HEAD, LANE:], nm[c][A_HEAD:, LANE:]) + eye_kk * w_end[c])
        g_pair = each(lambda c: r_t[c] + hg[c][:, LANE:])
        h_pair = each(lambda c: hg[c][:, :LANE])
        yield
        for n, i in enumerate(chunks):
            out_tiles = []
            for j in range(n_pairs):
                c = n * n_pairs + j
                both = _mm(cat([m_sbs[c], g_pair[c]], 0), bd_k(st[j]))
                out_tiles.append(both[A_HEAD:] + h_pair[c])
                st[j] = both[:A_HEAD] + n_sbs[c]
            mixed[i] = cat(out_tiles, 1)
            yield

    def post(i):
        o = mixed[i]
        inv_n = 1.0 / A_HEAD
        mean = head_sum(o) * inv_n
        cen = o - mean
        yield
        var = head_sum(cen * cen) * inv_n
        on = cen * lax.rsqrt(var + LNX_EPS) * lnw_ref[...] + lnb_ref[...]
        yield
        o_ref[0, i * c_len:(i + 1) * c_len, :] = (on + pre[i]["bonus"]) * pre[i]["g"]

    def emit(*tasks):
        tasks = list(tasks)
        while tasks:
            for task in list(tasks):
                if next(task, True):
                    tasks.remove(task)

    groups = [list(range(s, min(s + RWKV_CHUNKS_PER_GROUP, n_sub))) for s in range(0, n_sub, RWKV_CHUNKS_PER_GROUP)]
    emit(*[prep(i) for i in groups[0]])
    for n, grp in enumerate(groups):
        overlapped = [prep(i) for i in groups[n + 1]] if n + 1 < len(groups) else []
        overlapped += [post(i) for i in groups[n - 1]] if n > 0 else []
        emit(mix(grp), *overlapped)
    emit(*[post(i) for i in groups[-1]])
    for j in range(n_pairs):
        st_ref[j] = st[j]
    prev_ref[...] = pa_ref[0, tb - 1:tb, :]

    @pl.when(step == pl.num_programs(1) - 1)
    def _():
        for j in range(n_pairs):
            st_j = st_ref[j]
            wkv_ref[0, 2 * j] = _transpose_sq(st_j[:, :A_HEAD], eye64)
            wkv_ref[0, 2 * j + 1] = _transpose_sq(st_j[:, A_HEAD:], eye64)


def _rwkv_call(pa, shift0, wkv0, c_len, mu, w0, wup, a0, aup, gup, k_k, k_a, r_k, lnw, lnb, eseg):
    b, t, a_cols = pa.shape
    n_heads = wkv0.shape[1]
    d_a = n_heads * A_HEAD
    n_chunks = t // c_len
    n_sub = math.gcd(n_chunks, RWKV_CHUNKS_PER_STEP)
    tb = c_len * n_sub
    seq = lambda w: pl.BlockSpec((1, tb, w), lambda i, j: (i, j, 0))
    per_b = lambda shp: pl.BlockSpec((1,) + shp, lambda i, j: (i,) + (0,) * len(shp))
    consts = [mu, w0, wup, a0, aup, gup, k_k, k_a, r_k, lnw, lnb, eseg]
    return pl.pallas_call(
        functools.partial(_rwkv_kernel, c_len, n_sub),
        grid=(b, n_chunks // n_sub),
        in_specs=[seq(a_cols), per_b((1, a_cols)), per_b(wkv0.shape[1:])] + [_resident(w.shape) for w in consts],
        out_specs=[seq(d_a), per_b(wkv0.shape[1:])],
        out_shape=[jax.ShapeDtypeStruct((b, t, d_a), F32), jax.ShapeDtypeStruct(wkv0.shape, F32)],
        scratch_shapes=[pltpu.VMEM((1, a_cols), F32), pltpu.VMEM((n_heads // 2, A_HEAD, LANE), F32)],
        compiler_params=pltpu.CompilerParams(dimension_semantics=("arbitrary", "arbitrary"),
                                             vmem_limit_bytes=VMEM_LIMIT),
        name="rwkv7_mixer",
    )(pa, shift0, wkv0, *consts)


def _gla_kernel(c_len, n_chunks, pb_ref, gla0_ref, gkup_ref, gkb_ref, ng_ref, o_ref, st_ref):
    step = pl.program_id(1)
    n_pairs = st_ref.shape[1]
    dk = st_ref.shape[2] // 2
    dv = st_ref.shape[3]
    assert 2 * dk == LANE and dv == LANE
    n_heads = 2 * n_pairs
    nk = n_heads * dk
    nv = n_heads * dv
    tb = c_len * n_chunks

    @pl.when(step == 0)
    def _():
        st_ref[...] = gla0_ref[...]

    def cat(xs, axis):
        return jnp.concatenate(xs, axis=axis)

    p = pb_ref[0]
    q = p[:, 0:nk] * (dk ** -0.5)
    k = p[:, nk:2 * nk]
    v = p[:, 2 * nk:2 * nk + nv]
    og = p[:, 2 * nk + nv:2 * nk + 2 * nv]
    xgk = p[:, 2 * nk + 2 * nv:2 * nk + 2 * nv + LANE]
    z = _dot(xgk.astype(BF16), gkup_ref[...]) + gkb_ref[...]
    log_a = -_softplus(-z) / B_TAU

    sub = min(SUB, c_len)
    n_sub = c_len // sub
    row, col = _iota2(tb, tb)
    in_chunk = col >= (row & -c_len)
    tri = jnp.where((col <= row) & in_chunk, 1.0, 0.0).astype(BF16)
    b = _dot_ones_lhs(tri, log_a)
    if n_sub > 1:
        tri_start = jnp.where((col < (row & -sub)) & in_chunk, 1.0, 0.0).astype(BF16)
        b_start = _dot_ones_lhs(tri_start, log_a)
    else:
        b_start = jnp.zeros_like(b)
    q_inter = q * jnp.exp(b)
    q_local = q * jnp.exp(b - b_start)
    rows_of = lambda x, i: x[i * c_len:(i + 1) * c_len]
    b_last = [b[(i + 1) * c_len - 1:(i + 1) * c_len, :] for i in range(n_chunks)]
    k_state = cat([rows_of(k, i) * jnp.exp(b_last[i] - rows_of(b, i)) for i in range(n_chunks)], 0)
    row1 = lax.broadcasted_iota(jnp.int32, (c_len, 1), 0)
    k_local = []
    for i in range(n_chunks):
        per_sub = []
        for s in range(n_sub):
            ref = b_start[i * c_len + s * sub:i * c_len + s * sub + 1, :]
            expo = jnp.where(row1 < (s + 1) * sub, ref - rows_of(b, i), -1e30)
            per_sub.append(rows_of(k, i) * jnp.exp(expo))
        k_local.append(per_sub)

    half0 = lax.broadcasted_iota(jnp.int32, (1, LANE), 1) < dk
    rc, cc = _iota2(c_len, c_len)
    causal = rc >= cc
    top_rows = lax.broadcasted_iota(jnp.int32, (LANE, 1), 0) < dk
    log_a_hi, log_a_lo = _split2(log_a)
    ones_cv = jnp.ones((c_len, dv), BF16)
    kt = lambda x, i, j: x[i * c_len:(i + 1) * c_len, j * LANE:(j + 1) * LANE]
    vt = lambda x, i, h: x[i * c_len:(i + 1) * c_len, h * dv:(h + 1) * dv]
    pairs = [(i, j) for i in range(n_chunks) for j in range(n_pairs)]
    heads = [(i, h) for i in range(n_chunks) for h in range(n_heads)]

    def a_rows(i, j, s):
        ql = kt(q_local, i, j)[s * sub:(s + 1) * sub]
        lhs = cat([jnp.where(half0, ql, 0.0), jnp.where(half0, 0.0, ql)], 0)
        return _mm(lhs, kt(k_local[i][s], 0, j), NT)
    a_blk = {(i, j): [a_rows(i, j, s) for s in range(n_sub)] for i, j in pairs}
    a_mat = {}
    for i, h in heads:
        blocks = [a_blk[(i, h // 2)][s][(h % 2) * sub:(h % 2 + 1) * sub] for s in range(n_sub)]
        a_mat[(i, h)] = jnp.where(causal, cat(blocks, 0) if n_sub > 1 else blocks[0], 0.0)
    o_intra = {(i, h): _mm(a_mat[(i, h)], vt(v, i, h)) for i, h in heads}
    upd = {}
    dec = {}
    for i, j in pairs:
        kv = _mm(kt(k_state, i, j), v[i * c_len:(i + 1) * c_len, 2 * j * dv:(2 * j + 2) * dv], TN)
        upd[(i, j)] = jnp.where(top_rows, kv[:, :dv], kv[:, dv:])
        col_sum = lambda piece: lax.dot_general(kt(piece, i, j), ones_cv, (TN, ((), ())), preferred_element_type=F32)
        dec[(i, j)] = jnp.exp(col_sum(log_a_hi) + col_sum(log_a_lo))
    states = {}
    for j in range(n_pairs):
        s_cur = st_ref[0, j]
        for i in range(n_chunks):
            states[(i, j)] = s_cur
            s_cur = dec[(i, j)] * s_cur + upd[(i, j)]
        st_ref[0, j] = s_cur
    out_rows = []
    for i in range(n_chunks):
        outs = []
        for h in range(n_heads):
            qi = kt(q_inter, i, h // 2)
            qi = jnp.where(half0, qi, 0.0) if h % 2 == 0 else jnp.where(half0, 0.0, qi)
            o_h = o_intra[(i, h)] + _mm(qi, states[(i, h // 2)])
            o_h = o_h * lax.rsqrt(jnp.mean(o_h * o_h, axis=-1, keepdims=True) + EPS) * ng_ref[...]
            og_h = vt(og, i, h)
            outs.append(o_h * (og_h * jax.nn.sigmoid(og_h)))
        out_rows.append(cat(outs, 1))
    o_ref[0] = cat(out_rows, 0) if n_chunks > 1 else out_rows[0]


def _gla_call(pb, gla0, c_len, gkup, gkb, ng):
    b, t, cols = pb.shape
    n_heads, dk, dv = gla0.shape[1:]
    n_chunks = t // c_len
    per_step = math.gcd(n_chunks, GLA_CHUNKS_PER_STEP)
    tb = c_len * per_step
    st_shape = (n_heads // 2, 2 * dk, dv)
    seq = lambda w: pl.BlockSpec((1, tb, w), lambda i, j: (i, j, 0))
    per_b = lambda shp: pl.BlockSpec((1,) + shp, lambda i, j: (i,) + (0,) * len(shp))
    consts = [gkup, gkb, ng]
    o, st = pl.pallas_call(
        functools.partial(_gla_kernel, c_len, per_step),
        grid=(b, n_chunks // per_step),
        in_specs=[seq(cols), per_b(st_shape)] + [_resident(w.shape) for w in consts],
        out_specs=[seq(n_heads * dv), per_b(st_shape)],
        out_shape=[jax.ShapeDtypeStruct((b, t, n_heads * dv), F32), jax.ShapeDtypeStruct((b,) + st_shape, F32)],
        compiler_params=pltpu.CompilerParams(dimension_semantics=("arbitrary", "arbitrary"),
                                             vmem_limit_bytes=VMEM_LIMIT),
        name="gla_mixer",
    )(pb, gla0.reshape((b,) + st_shape), *consts)
    return o, st.reshape(gla0.shape)


def _chunk_len(t):
    return GLA_CHUNK if t % GLA_CHUNK == 0 else t


def _pad_rows(w, start, total):
    return jnp.zeros((total, w.shape[1]), w.dtype).at[start:start + w.shape[0]].set(w)


def _prep_layer(l, ln1_g, ffn1_wg, ffn1_wu, ffn1_wd, ln_mix_g, w_in, mu_shift, w0, w_lora_up, a0, a_lora_up,
                g_lora_up, k_k, k_a, r_k, lnx_w, lnx_b, gk_up, gk_b, gla_norm_g, w_out, ln2_g, ffn2_wg, ffn2_wu,
                ffn2_wd):
    d_a = w0.shape[1]
    a_cols = mu_shift.shape[1]
    nk = gk_b.shape[1]
    d_b = w_out.shape[1] - d_a
    row = lambda x: x[l].reshape(1, -1)
    wi = w_in[l]
    wi_b = wi[:, a_cols:]
    q_k_v = wi_b[:, :2 * nk + d_b]
    xgk_w = wi_b[:, 2 * nk + d_b:2 * nk + d_b + B_GATE_RANK]
    og_w = wi_b[:, 2 * nk + d_b + B_GATE_RANK:]
    pad = jnp.zeros((wi.shape[0], LANE - B_GATE_RANK), wi.dtype)
    wib = jnp.concatenate([q_k_v, og_w, xgk_w, pad], axis=1)
    eseg = jnp.kron(jnp.eye(HEAD_SUM_TILE // A_HEAD, dtype=F32), jnp.ones((A_HEAD, A_HEAD), F32)).astype(BF16)
    return dict(
        pre=(row(ln1_g), ffn1_wg[l].astype(BF16), ffn1_wu[l].astype(BF16), ffn1_wd[l].astype(BF16), row(ln_mix_g),
             wi[:, :a_cols].astype(BF16), wib.astype(BF16)),
        rwkv=(row(mu_shift), row(w0), _pad_rows(w_lora_up[l], 0, LANE).astype(BF16), row(a0),
              _pad_rows(a_lora_up[l], A_W_RANK, LANE).astype(BF16),
              _pad_rows(g_lora_up[l], A_W_RANK + A_A_RANK, LANE).astype(BF16),
              row(k_k), row(k_a), row(r_k), row(lnx_w), row(lnx_b), eseg),
        gla=(_pad_rows(gk_up[l], 0, LANE).astype(BF16), row(gk_b), row(gla_norm_g)),
        post=(w_out[l, :d_a].astype(BF16), w_out[l, d_a:].astype(BF16), row(ln2_g), ffn2_wg[l].astype(BF16),
              ffn2_wu[l].astype(BF16), ffn2_wd[l].astype(BF16)),
    )


def _trunk(x, shift, wkv, gla, layers, ln_f):
    b, t, d = x.shape
    c_len = _chunk_len(t)
    xf = x.reshape(b * t, d)
    new_shift, new_wkv, new_gla = [], [], []
    for l, lw in enumerate(layers):
        x1, pa, pb = _pre_call(xf, *lw["pre"])
        pa3 = pa.reshape(b, t, -1)
        oa, s_a = _rwkv_call(pa3, shift[l], wkv[l], c_len, *lw["rwkv"])
        ob, s_b = _gla_call(pb.reshape(b, t, -1), gla[l], c_len, *lw["gla"])
        xf = _post_call(x1, oa.reshape(b * t, -1), ob.reshape(b * t, -1), *lw["post"], ln_f,
                        final_norm=(l == len(layers) - 1))
        new_shift.append(pa3[:, -1:, :])
        new_wkv.append(s_a)
        new_gla.append(s_b)
    return xf.reshape(b, t, d), jnp.stack(new_shift), jnp.stack(new_wkv), jnp.stack(new_gla)


def kernel(x_prompt, x_sample, state_shift, state_wkv, state_gla, ln1_g, ffn1_wg, ffn1_wu, ffn1_wd, ln_mix_g, w_in,
           mu_shift, w0, w_lora_up, a0, a_lora_up, g_lora_up, k_k, k_a, r_k, lnx_w, lnx_b, gk_up, gk_b, gla_norm_g,
           w_out, ln2_g, ffn2_wg, ffn2_wu, ffn2_wd, ln_f_g):
    depth = ln1_g.shape[0]
    per_layer = (ln1_g, ffn1_wg, ffn1_wu, ffn1_wd, ln_mix_g, w_in, mu_shift, w0, w_lora_up, a0, a_lora_up, g_lora_up,
                 k_k, k_a, r_k.reshape(depth, -1), lnx_w, lnx_b, gk_up, gk_b, gla_norm_g, w_out, ln2_g, ffn2_wg,
                 ffn2_wu, ffn2_wd)
    layers = [_prep_layer(l, *per_layer) for l in range(depth)]
    ln_f = ln_f_g.reshape(1, -1)
    bp = x_prompt.shape[0]
    shift0 = jnp.zeros((depth, bp) + state_shift.shape[2:], F32)
    wkv0 = jnp.zeros((depth, bp) + state_wkv.shape[2:], F32)
    gla0 = jnp.zeros((depth, bp) + state_gla.shape[2:], F32)
    y_p, shift_p, wkv_p, gla_p = _trunk(x_prompt, shift0, wkv0, gla0, layers, ln_f)
    y_s, shift_s, wkv_s, gla_s = _trunk(x_sample, state_shift, state_wkv, state_gla, layers, ln_f)
    return (y_p, y_s, shift_p, wkv_p, gla_p, shift_s, wkv_s, gla_s)
```

```python
import functools
import math

import jax
import jax.numpy as jnp
from jax import lax
from jax.experimental import pallas as pl
from jax.experimental.pallas import tpu as pltpu

F32 = jnp.float32
BF16 = jnp.bfloat16
HIGHEST = lax.Precision.HIGHEST

EPS = 1e-6
LNX_EPS = 64e-5
A_HEAD = 64
A_W_RANK = 32
A_A_RANK = 32
A_G_RANK = 64
B_HEADS = 4
B_GATE_RANK = 16
B_TAU = 16.0
GLA_CHUNK = 64
SUB = 16
LANE = 128
VMEM_LIMIT = 56 * 1024 * 1024
TOKEN_TILE = 512
RWKV_CHUNKS_PER_STEP = 8
RWKV_CHUNKS_PER_GROUP = 4
GLA_CHUNKS_PER_STEP = 4
HEAD_SUM_TILE = 256

NN = ((1,), (0,))
NT = ((1,), (1,))
TN = ((0,), (0,))


def _dot(a, b, precision=None):
    return jnp.dot(a, b, preferred_element_type=F32, precision=precision)


def _split2(x):
    hi = x.astype(BF16)
    return hi, (x - hi.astype(F32)).astype(BF16)


def _mm(a, b, dims=NN):
    return lax.dot_general(a.astype(BF16), b.astype(BF16), (dims, ((), ())), preferred_element_type=F32)


def _split3(x):
    hi = x.astype(BF16)
    r1 = x - hi.astype(F32)
    mid = r1.astype(BF16)
    lo = (r1 - mid.astype(F32)).astype(BF16)
    return hi, mid, lo


def _dot_ones_lhs(ones_bf16, x):
    hi, lo = _split2(x)
    return _dot(ones_bf16, hi) + _dot(ones_bf16, lo)


def _dot_ones_rhs(x, ones_bf16):
    hi, lo = _split2(x)
    return _dot(hi, ones_bf16) + _dot(lo, ones_bf16)


def _transpose_sq(x, eye_bf16):
    dg = lambda piece: lax.dot_general(piece, eye_bf16, (TN, ((), ())), preferred_element_type=F32)
    hi, mid, lo = _split3(x)
    return dg(hi) + dg(mid) + dg(lo)


def _softplus(x):
    return jnp.maximum(x, 0.0) + jnp.log(1.0 + jnp.exp(-jnp.abs(x)))


def _rmsnorm(x, g):
    return x * lax.rsqrt(jnp.mean(x * x, axis=-1, keepdims=True) + EPS) * g


def _swiglu_half_step(x, ln_g, wg_ref, wu_ref, wd_ref):
    h = _rmsnorm(x, ln_g).astype(BF16)
    g = _dot(h, wg_ref[...])
    u = _dot(h, wu_ref[...])
    act = (g * jax.nn.sigmoid(g) * u).astype(BF16)
    return x + 0.5 * _dot(act, wd_ref[...])


def _pre_kernel(x_ref, ln1_ref, wg_ref, wu_ref, wd_ref, lnm_ref, wia_ref, wib_ref, x1_ref, pa_ref, pb_ref):
    x1 = _swiglu_half_step(x_ref[...], ln1_ref[...], wg_ref, wu_ref, wd_ref)
    x1_ref[...] = x1
    h = _rmsnorm(x1, lnm_ref[...]).astype(BF16)
    pa_ref[...] = _dot(h, wia_ref[...])
    pb_ref[...] = _dot(h, wib_ref[...])


def _resident(shape):
    return pl.BlockSpec(shape, lambda *_: (0,) * len(shape), pipeline_mode=pl.Buffered(1))


def _token_tile(n):
    tm = min(TOKEN_TILE, n)
    assert n % tm == 0
    return tm


def _pre_call(x, ln1, wg, wu, wd, lnm, wia, wib):
    n, d = x.shape
    tm = _token_tile(n)
    tok = lambda w: pl.BlockSpec((tm, w), lambda i: (i, 0))
    return pl.pallas_call(
        _pre_kernel,
        grid=(n // tm,),
        in_specs=[tok(d), _resident(ln1.shape), _resident(wg.shape), _resident(wu.shape), _resident(wd.shape),
                  _resident(lnm.shape), _resident(wia.shape), _resident(wib.shape)],
        out_specs=[tok(d), tok(wia.shape[1]), tok(wib.shape[1])],
        out_shape=[jax.ShapeDtypeStruct((n, d), F32), jax.ShapeDtypeStruct((n, wia.shape[1]), F32),
                   jax.ShapeDtypeStruct((n, wib.shape[1]), F32)],
        compiler_params=pltpu.CompilerParams(dimension_semantics=("arbitrary",), vmem_limit_bytes=VMEM_LIMIT),
        name="pre_ffn_inproj",
    )(x, ln1, wg, wu, wd, lnm, wia, wib)


def _post_kernel(final_norm, x1_ref, oa_ref, ob_ref, woa_ref, wob_ref, ln2_ref, wg_ref, wu_ref, wd_ref, lnf_ref,
                 y_ref):
    x2 = x1_ref[...] + _dot(oa_ref[...].astype(BF16), woa_ref[...]) + _dot(ob_ref[...].astype(BF16), wob_ref[...])
    x3 = _swiglu_half_step(x2, ln2_ref[...], wg_ref, wu_ref, wd_ref)
    y_ref[...] = _rmsnorm(x3, lnf_ref[...]) if final_norm else x3


def _post_call(x1, oa, ob, woa, wob, ln2, wg, wu, wd, lnf, final_norm):
    n, d = x1.shape
    tm = _token_tile(n)
    tok = lambda w: pl.BlockSpec((tm, w), lambda i: (i, 0))
    return pl.pallas_call(
        functools.partial(_post_kernel, final_norm),
        grid=(n // tm,),
        in_specs=[tok(d), tok(oa.shape[1]), tok(ob.shape[1]), _resident(woa.shape), _resident(wob.shape),
                  _resident(ln2.shape), _resident(wg.shape), _resident(wu.shape), _resident(wd.shape),
                  _resident(lnf.shape)],
        out_specs=tok(d),
        out_shape=jax.ShapeDtypeStruct((n, d), F32),
        compiler_params=pltpu.CompilerParams(dimension_semantics=("arbitrary",), vmem_limit_bytes=VMEM_LIMIT),
        name="post_outproj_ffn",
    )(x1, oa, ob, woa, wob, ln2, wg, wu, wd, lnf)


def _iota2(n, m):
    return lax.broadcasted_iota(jnp.int32, (n, m), 0), lax.broadcasted_iota(jnp.int32, (n, m), 1)


STASHED = ("a_hat", "r_til", "k_hat", "b_hat", "v", "kbar", "bbar", "bonus", "g")


def _rwkv_kernel(c_len, n_sub, pipelined, pa_ref, nxt_ref, shift0_ref, wkv0_ref, mu_ref, w0_ref, wup_ref, a0_ref,
                 aup_ref, gup_ref, kk_ref, ka_ref, rk_ref, lnw_ref, lnb_ref, eseg_ref, o_ref, wkv_ref, prev_ref,
                 st_ref, stash_ref, wend_ref):
    step = pl.program_id(1)
    n_heads = wkv_ref.shape[1]
    n_pairs = n_heads // 2
    d_a = n_heads * A_HEAD
    tb = c_len * n_sub
    c2 = 2 * c_len

    r64, c64 = _iota2(A_HEAD, A_HEAD)
    eye64 = jnp.where(r64 == c64, 1.0, 0.0).astype(BF16)

    @pl.when(step == 0)
    def _():
        prev_ref[...] = shift0_ref[0]
        for j in range(n_pairs):
            st_ref[j] = jnp.concatenate([_transpose_sq(wkv0_ref[0, 2 * j], eye64),
                                         _transpose_sq(wkv0_ref[0, 2 * j + 1], eye64)], axis=1)

    def cat(xs, axis):
        return jnp.concatenate(xs, axis=axis)

    eseg = eseg_ref[...]

    def head_sum(x):
        w = eseg.shape[0]
        return cat([_dot(x[:, s:s + w].astype(BF16), eseg) for s in range(0, d_a, w)], 1)

    row1 = lax.broadcasted_iota(jnp.int32, (c_len, 1), 0)
    rt, ct = _iota2(c_len, c_len)
    tri = jnp.where(ct <= rt, 1.0, 0.0).astype(BF16)
    pre = {}

    def rows_of(ref, i):
        return ref[0, i * c_len:(i + 1) * c_len, :]

    def row_before(ref, i):
        return ref[0, i * c_len - 1:i * c_len, :]

    def prep(p, prev, sink):
        shifted = jnp.where(row1 == 0, prev, pltpu.roll(p, 1, axis=0))
        xs = p + (shifted - p) * mu_ref[...]
        r = xs[:, 0:d_a]
        k = xs[:, d_a:2 * d_a]
        v = xs[:, 2 * d_a:3 * d_a]
        lora_in = xs[:, 3 * d_a:3 * d_a + LANE]
        yield
        lw = _dot(jnp.tanh(lora_in).astype(BF16), wup_ref[...])
        la = _dot(lora_in.astype(BF16), aup_ref[...])
        g = _dot(jax.nn.sigmoid(lora_in).astype(BF16), gup_ref[...])
        kk = k * kk_ref[...]
        kk_sq = head_sum(kk * kk)
        yield
        logw = (-math.exp(-0.5) * math.log2(math.e)) * jax.nn.sigmoid(w0_ref[...] + lw)
        gc = _dot_ones_lhs(tri, logw)
        yield
        a = jax.nn.sigmoid(a0_ref[...] + la)
        kk = kk * lax.rsqrt(jnp.maximum(kk_sq, 1e-24))
        kmod = k * (1.0 + (a - 1.0) * ka_ref[...])
        kka = kk * a
        yield
        gc_last = gc[c_len - 1:c_len, :]
        e_inv = jnp.exp2(-gc)
        e_end = jnp.exp2(gc_last - gc)
        yield
        sink(dict(a_hat=-kk * jnp.exp2(gc - logw), r_til=r * jnp.exp2(gc), k_hat=kmod * e_inv, b_hat=kka * e_inv,
                  v=v, kbar=kmod * e_end, bbar=kka * e_end, bonus=head_sum(r * kmod * rk_ref[...]) * v, g=g,
                  w_end=jnp.exp2(gc_last)))

    def keep(i):
        return lambda ops: pre.__setitem__(i, ops)

    def stash(slot):
        def store(ops):
            for n, name in enumerate(STASHED):
                stash_ref[n, slot * c_len:(slot + 1) * c_len, :] = ops[name]
            wend_ref[slot] = ops["w_end"]
        return store

    def unstash(slot):
        ops = {name: stash_ref[n, slot * c_len:(slot + 1) * c_len, :] for n, name in enumerate(STASHED)}
        ops["w_end"] = wend_ref[slot]
        return ops

    rr, cc = _iota2(c_len, c2)
    col_t = cc & (c_len - 1)
    strict = col_t < rr
    lower = col_t <= rr
    eye_tt = jnp.where(col_t == rr, 1.0, 0.0).astype(F32)
    rk, ck = _iota2(A_HEAD, LANE)
    eye_kk = jnp.where((ck & (A_HEAD - 1)) == rk, 1.0, 0.0).astype(F32)
    half_k = lax.broadcasted_iota(jnp.int32, (1, LANE), 1) < A_HEAD
    half_t = lax.broadcasted_iota(jnp.int32, (1, c2), 1) < c_len
    n_double = int(math.log2(c_len))
    assert 2 ** n_double == c_len and n_double >= 2

    def bd(x, half):
        xb = x.astype(BF16)
        return cat([xb * jnp.where(half, 1.0, 0.0).astype(BF16), xb * jnp.where(half, 0.0, 1.0).astype(BF16)], 0)

    bd_k = lambda x: bd(x, half_k)
    bd_t = lambda x: bd(x, half_t)

    st = [st_ref[j] for j in range(n_pairs)]
    mixed = {}

    def mix(chunks):
        chains = [(i, j) for i in chunks for j in range(n_pairs)]
        each = lambda f: [f(c) for c in range(len(chains))]
        at = lambda name: [pre[i][name][:, j * LANE:(j + 1) * LANE] for i, j in chains]
        a_t, r_t, k_t, b_t, v_t = at("a_hat"), at("r_til"), at("k_hat"), at("b_hat"), at("v")
        kbar_t, bbar_t, w_end = at("kbar"), at("bbar"), at("w_end")
        bd_b = each(lambda c: bd_k(b_t[c]))
        bd_kh = each(lambda c: bd_k(k_t[c]))
        xa = each(lambda c: _mm(a_t[c], cat([bd_b[c], bd_kh[c]], 0), NT))
        yield
        xr = each(lambda c: _mm(r_t[c], cat([bd_kh[c], bd_b[c]], 0), NT))
        l_mat = each(lambda c: jnp.where(strict, xa[c][:, :c2], 0.0))
        a_ak = each(lambda c: jnp.where(strict, xa[c][:, c2:], 0.0))
        yield
        t_inv = each(lambda c: eye_tt + l_mat[c])
        pw = each(lambda c: _mm(l_mat[c], bd_t(l_mat[c])))
        a_rk_rb = each(lambda c: cat([jnp.where(lower, xr[c][:, :c2], 0.0), jnp.where(lower, xr[c][:, c2:], 0.0)], 1))
        yield
        for _ in range(1, n_double - 1):
            y = each(lambda c: _mm(pw[c], cat([bd_t(pw[c]), bd_t(t_inv[c])], 1)))
            t_inv = each(lambda c: t_inv[c] + y[c][:, c2:])
            pw = each(lambda c: y[c][:, :c2])
            yield
        t_inv = each(lambda c: t_inv[c] + _mm(pw[c], bd_t(t_inv[c])))
        bd_v = each(lambda c: bd_k(v_t[c]))
        a_ak_v = each(lambda c: _mm(a_ak[c], bd_v[c]))
        yield
        pq = each(lambda c: _mm(t_inv[c], cat([bd_k(a_t[c]), bd_k(a_ak_v[c])], 1)))
        p_t = each(lambda c: pq[c][:, :LANE])
        q_t = each(lambda c: pq[c][:, LANE:])
        yield
        hg = each(lambda c: _mm(a_rk_rb[c], cat([cat([bd_v[c], jnp.zeros((c2, LANE), BF16)], 1),
                                                 cat([bd_k(q_t[c]), bd_k(p_t[c])], 1)], 0)))
        yield
        nm = each(lambda c: _mm(cat([kbar_t[c], bbar_t[c]], 0),
                                cat([cat([v_t[c], jnp.zeros((c_len, LANE), F32)], 1), cat([q_t[c], p_t[c]], 1)], 0),
                                TN))
        n_sbs = each(lambda c: jnp.where(half_k, nm[c][:A_HEAD, :LANE], nm[c][A_HEAD:, :LANE]))
        m_sbs = each(lambda c: jnp.where(half_k, nm[c][:A_HEAD, LANE:], nm[c][A_HEAD:, LANE:]) + eye_kk * w_end[c])
        g_pair = each(lambda c: r_t[c] + hg[c][:, LANE:])
        h_pair = each(lambda c: hg[c][:, :LANE])
        yield
        for n, i in enumerate(chunks):
            out_tiles = []
            for j in range(n_pairs):
                c = n * n_pairs + j
                both = _mm(cat([m_sbs[c], g_pair[c]], 0), bd_k(st[j]))
                out_tiles.append(both[A_HEAD:] + h_pair[c])
                st[j] = both[:A_HEAD] + n_sbs[c]
            mixed[i] = cat(out_tiles, 1)
            yield

    def post(i):
        o = mixed[i]
        inv_n = 1.0 / A_HEAD
        mean = head_sum(o) * inv_n
        cen = o - mean
        yield
        var = head_sum(cen * cen) * inv_n
        on = cen * lax.rsqrt(var + LNX_EPS) * lnw_ref[...] + lnb_ref[...]
        yield
        o_ref[0, i * c_len:(i + 1) * c_len, :] = (on + pre[i]["bonus"]) * pre[i]["g"]

    def emit(*tasks):
        tasks = list(tasks)
        while tasks:
            for task in list(tasks):
                if next(task, True):
                    tasks.remove(task)

    def prep_here(i):
        prev = row_before(pa_ref, i) if i > 0 else prev_ref[...]
        return prep(rows_of(pa_ref, i), prev, keep(i))

    groups = [list(range(s, min(s + RWKV_CHUNKS_PER_GROUP, n_sub))) for s in range(0, n_sub, RWKV_CHUNKS_PER_GROUP)]
    if not pipelined:
        emit(*[prep_here(i) for i in groups[0]])
        carried = []
    else:
        @pl.when(step == 0)
        def _():
            for i in groups[0]:
                prev = row_before(pa_ref, i) if i > 0 else shift0_ref[0]
                emit(prep(rows_of(pa_ref, i), prev, stash(i)))
        for i in groups[0]:
            pre[i] = unstash(i)
        carried = [prep(rows_of(nxt_ref, i), row_before(nxt_ref, i) if i > 0 else row_before(pa_ref, n_sub), stash(i))
                   for i in groups[0]]
    for n, grp in enumerate(groups):
        overlapped = [prep_here(i) for i in groups[n + 1]] if n + 1 < len(groups) else carried
        overlapped += [post(i) for i in groups[n - 1]] if n > 0 else []
        emit(mix(grp), *overlapped)
    emit(*[post(i) for i in groups[-1]])
    for j in range(n_pairs):
        st_ref[j] = st[j]
    prev_ref[...] = row_before(pa_ref, n_sub)

    @pl.when(step == pl.num_programs(1) - 1)
    def _():
        for j in range(n_pairs):
            st_j = st_ref[j]
            wkv_ref[0, 2 * j] = _transpose_sq(st_j[:, :A_HEAD], eye64)
            wkv_ref[0, 2 * j + 1] = _transpose_sq(st_j[:, A_HEAD:], eye64)


def _rwkv_call(pa, shift0, wkv0, c_len, mu, w0, wup, a0, aup, gup, k_k, k_a, r_k, lnw, lnb, eseg):
    b, t, a_cols = pa.shape
    n_heads = wkv0.shape[1]
    d_a = n_heads * A_HEAD
    n_chunks = t // c_len
    n_sub = math.gcd(n_chunks, RWKV_CHUNKS_PER_STEP)
    n_steps = n_chunks // n_sub
    grp = math.gcd(n_sub, RWKV_CHUNKS_PER_GROUP)
    pipelined = n_steps > 1 and n_sub > grp
    tb = c_len * n_sub
    seq = lambda w: pl.BlockSpec((1, tb, w), lambda i, j: (i, j, 0))
    nxt = pl.BlockSpec((1, grp * c_len, a_cols),
                       lambda i, j: (i, jnp.minimum((j + 1) * (n_sub // grp), n_chunks // grp - 1), 0))
    per_b = lambda shp: pl.BlockSpec((1,) + shp, lambda i, j: (i,) + (0,) * len(shp))
    consts = [mu, w0, wup, a0, aup, gup, k_k, k_a, r_k, lnw, lnb, eseg]
    return pl.pallas_call(
        functools.partial(_rwkv_kernel, c_len, n_sub, pipelined),
        grid=(b, n_steps),
        in_specs=[seq(a_cols), nxt, per_b((1, a_cols)), per_b(wkv0.shape[1:])] + [_resident(w.shape) for w in consts],
        out_specs=[seq(d_a), per_b(wkv0.shape[1:])],
        out_shape=[jax.ShapeDtypeStruct((b, t, d_a), F32), jax.ShapeDtypeStruct(wkv0.shape, F32)],
        scratch_shapes=[pltpu.VMEM((1, a_cols), F32), pltpu.VMEM((n_heads // 2, A_HEAD, LANE), F32),
                        pltpu.VMEM((len(STASHED), grp * c_len, d_a), F32), pltpu.VMEM((grp, 1, d_a), F32)],
        compiler_params=pltpu.CompilerParams(dimension_semantics=("arbitrary", "arbitrary"),
                                             vmem_limit_bytes=VMEM_LIMIT),
        name="rwkv7_mixer",
    )(pa, pa, shift0, wkv0, *consts)


def _gla_kernel(c_len, n_chunks, pb_ref, gla0_ref, gkup_ref, gkb_ref, ng_ref, o_ref, st_ref):
    step = pl.program_id(1)
    n_pairs = st_ref.shape[1]
    dk = st_ref.shape[2] // 2
    dv = st_ref.shape[3]
    assert 2 * dk == LANE and dv == LANE
    n_heads = 2 * n_pairs
    nk = n_heads * dk
    nv = n_heads * dv
    tb = c_len * n_chunks

    @pl.when(step == 0)
    def _():
        st_ref[...] = gla0_ref[...]

    def cat(xs, axis):
        return jnp.concatenate(xs, axis=axis)

    p = pb_ref[0]
    q = p[:, 0:nk] * (dk ** -0.5)
    k = p[:, nk:2 * nk]
    v = p[:, 2 * nk:2 * nk + nv]
    og = p[:, 2 * nk + nv:2 * nk + 2 * nv]
    xgk = p[:, 2 * nk + 2 * nv:2 * nk + 2 * nv + LANE]
    z = _dot(xgk.astype(BF16), gkup_ref[...]) + gkb_ref[...]
    log_a = -_softplus(-z) / B_TAU

    sub = min(SUB, c_len)
    n_sub = c_len // sub
    row, col = _iota2(tb, tb)
    in_chunk = col >= (row & -c_len)
    tri = jnp.where((col <= row) & in_chunk, 1.0, 0.0).astype(BF16)
    b = _dot_ones_lhs(tri, log_a)
    if n_sub > 1:
        tri_start = jnp.where((col < (row & -sub)) & in_chunk, 1.0, 0.0).astype(BF16)
        b_start = _dot_ones_lhs(tri_start, log_a)
    else:
        b_start = jnp.zeros_like(b)
    q_inter = q * jnp.exp(b)
    q_local = q * jnp.exp(b - b_start)
    rows_of = lambda x, i: x[i * c_len:(i + 1) * c_len]
    b_last = [b[(i + 1) * c_len - 1:(i + 1) * c_len, :] for i in range(n_chunks)]
    k_state = cat([rows_of(k, i) * jnp.exp(b_last[i] - rows_of(b, i)) for i in range(n_chunks)], 0)
    row1 = lax.broadcasted_iota(jnp.int32, (c_len, 1), 0)
    k_local = []
    for i in range(n_chunks):
        per_sub = []
        for s in range(n_sub):
            ref = b_start[i * c_len + s * sub:i * c_len + s * sub + 1, :]
            expo = jnp.where(row1 < (s + 1) * sub, ref - rows_of(b, i), -1e30)
            per_sub.append(rows_of(k, i) * jnp.exp(expo))
        k_local.append(per_sub)

    half0 = lax.broadcasted_iota(jnp.int32, (1, LANE), 1) < dk
    rc, cc = _iota2(c_len, c_len)
    causal = rc >= cc
    top_rows = lax.broadcasted_iota(jnp.int32, (LANE, 1), 0) < dk
    log_a_hi, log_a_lo = _split2(log_a)
    ones_cv = jnp.ones((c_len, dv), BF16)
    kt = lambda x, i, j: x[i * c_len:(i + 1) * c_len, j * LANE:(j + 1) * LANE]
    vt = lambda x, i, h: x[i * c_len:(i + 1) * c_len, h * dv:(h + 1) * dv]
    pairs = [(i, j) for i in range(n_chunks) for j in range(n_pairs)]
    heads = [(i, h) for i in range(n_chunks) for h in range(n_heads)]

    def a_rows(i, j, s):
        ql = kt(q_local, i, j)[s * sub:(s + 1) * sub]
        lhs = cat([jnp.where(half0, ql, 0.0), jnp.where(half0, 0.0, ql)], 0)
        return _mm(lhs, kt(k_local[i][s], 0, j), NT)
    a_blk = {(i, j): [a_rows(i, j, s) for s in range(n_sub)] for i, j in pairs}
    a_mat = {}
    for i, h in heads:
        blocks = [a_blk[(i, h // 2)][s][(h % 2) * sub:(h % 2 + 1) * sub] for s in range(n_sub)]
        a_mat[(i, h)] = jnp.where(causal, cat(blocks, 0) if n_sub > 1 else blocks[0], 0.0)
    o_intra = {(i, h): _mm(a_mat[(i, h)], vt(v, i, h)) for i, h in heads}
    upd = {}
    dec = {}
    for i, j in pairs:
        kv = _mm(kt(k_state, i, j), v[i * c_len:(i + 1) * c_len, 2 * j * dv:(2 * j + 2) * dv], TN)
        upd[(i, j)] = jnp.where(top_rows, kv[:, :dv], kv[:, dv:])
        col_sum = lambda piece: lax.dot_general(kt(piece, i, j), ones_cv, (TN, ((), ())), preferred_element_type=F32)
        dec[(i, j)] = jnp.exp(col_sum(log_a_hi) + col_sum(log_a_lo))
    states = {}
    for j in range(n_pairs):
        s_cur = st_ref[0, j]
        for i in range(n_chunks):
            states[(i, j)] = s_cur
            s_cur = dec[(i, j)] * s_cur + upd[(i, j)]
        st_ref[0, j] = s_cur
    out_rows = []
    for i in range(n_chunks):
        outs = []
        for h in range(n_heads):
            qi = kt(q_inter, i, h // 2)
            qi = jnp.where(half0, qi, 0.0) if h % 2 == 0 else jnp.where(half0, 0.0, qi)
            o_h = o_intra[(i, h)] + _mm(qi, states[(i, h // 2)])
            o_h = o_h * lax.rsqrt(jnp.mean(o_h * o_h, axis=-1, keepdims=True) + EPS) * ng_ref[...]
            og_h = vt(og, i, h)
            outs.append(o_h * (og_h * jax.nn.sigmoid(og_h)))
        out_rows.append(cat(outs, 1))
    o_ref[0] = cat(out_rows, 0) if n_chunks > 1 else out_rows[0]


def _gla_call(pb, gla0, c_len, gkup, gkb, ng):
    b, t, cols = pb.shape
    n_heads, dk, dv = gla0.shape[1:]
    n_chunks = t // c_len
    per_step = math.gcd(n_chunks, GLA_CHUNKS_PER_STEP)
    tb = c_len * per_step
    st_shape = (n_heads // 2, 2 * dk, dv)
    seq = lambda w: pl.BlockSpec((1, tb, w), lambda i, j: (i, j, 0))
    per_b = lambda shp: pl.BlockSpec((1,) + shp, lambda i, j: (i,) + (0,) * len(shp))
    consts = [gkup, gkb, ng]
    o, st = pl.pallas_call(
        functools.partial(_gla_kernel, c_len, per_step),
        grid=(b, n_chunks // per_step),
        in_specs=[seq(cols), per_b(st_shape)] + [_resident(w.shape) for w in consts],
        out_specs=[seq(n_heads * dv), per_b(st_shape)],
        out_shape=[jax.ShapeDtypeStruct((b, t, n_heads * dv), F32), jax.ShapeDtypeStruct((b,) + st_shape, F32)],
        compiler_params=pltpu.CompilerParams(dimension_semantics=("arbitrary", "arbitrary"),
                                             vmem_limit_bytes=VMEM_LIMIT),
        name="gla_mixer",
    )(pb, gla0.reshape((b,) + st_shape), *consts)
    return o, st.reshape(gla0.shape)


def _chunk_len(t):
    return GLA_CHUNK if t % GLA_CHUNK == 0 else t


def _pad_rows(w, start, total):
    return jnp.zeros((total, w.shape[1]), w.dtype).at[start:start + w.shape[0]].set(w)


def _prep_layer(l, ln1_g, ffn1_wg, ffn1_wu, ffn1_wd, ln_mix_g, w_in, mu_shift, w0, w_lora_up, a0, a_lora_up,
                g_lora_up, k_k, k_a, r_k, lnx_w, lnx_b, gk_up, gk_b, gla_norm_g, w_out, ln2_g, ffn2_wg, ffn2_wu,
                ffn2_wd):
    d_a = w0.shape[1]
    a_cols = mu_shift.shape[1]
    nk = gk_b.shape[1]
    d_b = w_out.shape[1] - d_a
    row = lambda x: x[l].reshape(1, -1)
    wi = w_in[l]
    wi_b = wi[:, a_cols:]
    q_k_v = wi_b[:, :2 * nk + d_b]
    xgk_w = wi_b[:, 2 * nk + d_b:2 * nk + d_b + B_GATE_RANK]
    og_w = wi_b[:, 2 * nk + d_b + B_GATE_RANK:]
    pad = jnp.zeros((wi.shape[0], LANE - B_GATE_RANK), wi.dtype)
    wib = jnp.concatenate([q_k_v, og_w, xgk_w, pad], axis=1)
    eseg = jnp.kron(jnp.eye(HEAD_SUM_TILE // A_HEAD, dtype=F32), jnp.ones((A_HEAD, A_HEAD), F32)).astype(BF16)
    return dict(
        pre=(row(ln1_g), ffn1_wg[l].astype(BF16), ffn1_wu[l].astype(BF16), ffn1_wd[l].astype(BF16), row(ln_mix_g),
             wi[:, :a_cols].astype(BF16), wib.astype(BF16)),
        rwkv=(row(mu_shift), row(w0), _pad_rows(w_lora_up[l], 0, LANE).astype(BF16), row(a0),
              _pad_rows(a_lora_up[l], A_W_RANK, LANE).astype(BF16),
              _pad_rows(g_lora_up[l], A_W_RANK + A_A_RANK, LANE).astype(BF16),
              row(k_k), row(k_a), row(r_k), row(lnx_w), row(lnx_b), eseg),
        gla=(_pad_rows(gk_up[l], 0, LANE).astype(BF16), row(gk_b), row(gla_norm_g)),
        post=(w_out[l, :d_a].astype(BF16), w_out[l, d_a:].astype(BF16), row(ln2_g), ffn2_wg[l].astype(BF16),
              ffn2_wu[l].astype(BF16), ffn2_wd[l].astype(BF16)),
    )


def _trunk(x, shift, wkv, gla, layers, ln_f):
    b, t, d = x.shape
    c_len = _chunk_len(t)
    xf = x.reshape(b * t, d)
    new_shift, new_wkv, new_gla = [], [], []
    for l, lw in enumerate(layers):
        x1, pa, pb = _pre_call(xf, *lw["pre"])
        pa3 = pa.reshape(b, t, -1)
        oa, s_a = _rwkv_call(pa3, shift[l], wkv[l], c_len, *lw["rwkv"])
        ob, s_b = _gla_call(pb.reshape(b, t, -1), gla[l], c_len, *lw["gla"])
        xf = _post_call(x1, oa.reshape(b * t, -1), ob.reshape(b * t, -1), *lw["post"], ln_f,
                        final_norm=(l == len(layers) - 1))
        new_shift.append(pa3[:, -1:, :])
        new_wkv.append(s_a)
        new_gla.append(s_b)
    return xf.reshape(b, t, d), jnp.stack(new_shift), jnp.stack(new_wkv), jnp.stack(new_gla)


def kernel(x_prompt, x_sample, state_shift, state_wkv, state_gla, ln1_g, ffn1_wg, ffn1_wu, ffn1_wd, ln_mix_g, w_in,
           mu_shift, w0, w_lora_up, a0, a_lora_up, g_lora_up, k_k, k_a, r_k, lnx_w, lnx_b, gk_up, gk_b, gla_norm_g,
           w_out, ln2_g, ffn2_wg, ffn2_wu, ffn2_wd, ln_f_g):
    depth = ln1_g.shape[0]
    per_layer = (ln1_g, ffn1_wg, ffn1_wu, ffn1_wd, ln_mix_g, w_in, mu_shift, w0, w_lora_up, a0, a_lora_up, g_lora_up,
                 k_k, k_a, r_k.reshape(depth, -1), lnx_w, lnx_b, gk_up, gk_b, gla_norm_g, w_out, ln2_g, ffn2_wg,
                 ffn2_wu, ffn2_wd)
    layers = [_prep_layer(l, *per_layer) for l in range(depth)]
    ln_f = ln_f_g.reshape(1, -1)
    bp = x_prompt.shape[0]
    shift0 = jnp.zeros((depth, bp) + state_shift.shape[2:], F32)
    wkv0 = jnp.zeros((depth, bp) + state_wkv.shape[2:], F32)
    gla0 = jnp.zeros((depth, bp) + state_gla.shape[2:], F32)
    y_p, shift_p, wkv_p, gla_p = _trunk(x_prompt, shift0, wkv0, gla0, layers, ln_f)
    y_s, shift_s, wkv_s, gla_s = _trunk(x_sample, state_shift, state_wkv, state_gla, layers, ln_f)
    return (y_p, y_s, shift_p, wkv_p, gla_p, shift_s, wkv_s, gla_s)
```

```python
import functools
import math

import jax
import jax.numpy as jnp
from jax import lax
from jax.experimental import pallas as pl
from jax.experimental.pallas import tpu as pltpu

F32 = jnp.float32
BF16 = jnp.bfloat16
HIGHEST = lax.Precision.HIGHEST

EPS = 1e-6
LNX_EPS = 64e-5
A_HEAD = 64
A_W_RANK = 32
A_A_RANK = 32
A_G_RANK = 64
B_HEADS = 4
B_GATE_RANK = 16
B_TAU = 16.0
GLA_CHUNK = 64
SUB = 16
LANE = 128
VMEM_LIMIT = 56 * 1024 * 1024
PRE_TILE = 256
POST_TILE = 512
RWKV_CHUNKS_PER_STEP = 8
RWKV_CHUNKS_PER_GROUP = 4
GLA_CHUNKS_PER_STEP = 4
HEAD_SUM_TILE = 256

NN = ((1,), (0,))
NT = ((1,), (1,))
TN = ((0,), (0,))


def _dot(a, b, precision=None):
    return jnp.dot(a, b, preferred_element_type=F32, precision=precision)


def _split2(x):
    hi = x.astype(BF16)
    return hi, (x - hi.astype(F32)).astype(BF16)


def _mm(a, b, dims=NN):
    return lax.dot_general(a.astype(BF16), b.astype(BF16), (dims, ((), ())), preferred_element_type=F32)


def _split3(x):
    hi = x.astype(BF16)
    r1 = x - hi.astype(F32)
    mid = r1.astype(BF16)
    lo = (r1 - mid.astype(F32)).astype(BF16)
    return hi, mid, lo


def _dot_ones_lhs(ones_bf16, x):
    hi, lo = _split2(x)
    return _dot(ones_bf16, hi) + _dot(ones_bf16, lo)


def _dot_ones_rhs(x, ones_bf16):
    hi, lo = _split2(x)
    return _dot(hi, ones_bf16) + _dot(lo, ones_bf16)


def _transpose_sq(x, eye_bf16):
    dg = lambda piece: lax.dot_general(piece, eye_bf16, (TN, ((), ())), preferred_element_type=F32)
    hi, mid, lo = _split3(x)
    return dg(hi) + dg(mid) + dg(lo)


def _softplus(x):
    return jnp.maximum(x, 0.0) + jnp.log(1.0 + jnp.exp(-jnp.abs(x)))


def _rmsnorm(x, g):
    return x * lax.rsqrt(jnp.mean(x * x, axis=-1, keepdims=True) + EPS) * g


def _swiglu_half_step(x, ln_g, wg_ref, wu_ref, wd_ref):
    h = _rmsnorm(x, ln_g).astype(BF16)
    g = _dot(h, wg_ref[...])
    u = _dot(h, wu_ref[...])
    act = (g * jax.nn.sigmoid(g) * u).astype(BF16)
    return x + 0.5 * _dot(act, wd_ref[...])


def _iota2(n, m):
    return lax.broadcasted_iota(jnp.int32, (n, m), 0), lax.broadcasted_iota(jnp.int32, (n, m), 1)


def _cat(xs, axis):
    return jnp.concatenate(xs, axis=axis)


def _emit(*tasks):
    tasks = list(tasks)
    while tasks:
        for task in list(tasks):
            if next(task, True):
                tasks.remove(task)


def _head_sum(x, eseg):
    w = eseg.shape[0]
    return _cat([_dot(x[:, s:s + w].astype(BF16), eseg) for s in range(0, x.shape[1], w)], 1)


OPERANDS = ("a_hat", "k_hat", "b_hat", "kbar", "bbar", "v", "r_til", "bonus", "g")


def _pre_kernel(c_len, n_tiles, x_ref, shift_ref, ln1_ref, wg_ref, wu_ref, wd_ref, lnm_ref, wia_ref, wib_ref, mu_ref,
                w0_ref, wup_ref, a0_ref, aup_ref, gup_ref, kk_ref, ka_ref, rk_ref, eseg_ref,
                x1_ref, pb_ref, ops_ref, wend_ref, last_ref, pa_scr, carry_ref):
    step = pl.program_id(0)
    tm = x_ref.shape[0]
    d_a = w0_ref.shape[1]
    n_chunks = tm // c_len
    eseg = eseg_ref[...]
    row1 = lax.broadcasted_iota(jnp.int32, (c_len, 1), 0)
    rt, ct = _iota2(c_len, c_len)
    tri = jnp.where(ct <= rt, 1.0, 0.0).astype(BF16)

    def front_end(load_p, load_prev, chunk):
        p = load_p()
        shifted = jnp.where(row1 == 0, load_prev(), pltpu.roll(p, 1, axis=0))
        last_ref[chunk] = p[c_len - 1:c_len, :]
        xs = p + (shifted - p) * mu_ref[...]
        r = xs[:, 0:d_a]
        k = xs[:, d_a:2 * d_a]
        v = xs[:, 2 * d_a:3 * d_a]
        lora_in = xs[:, 3 * d_a:3 * d_a + LANE]
        yield
        lw = _dot(jnp.tanh(lora_in).astype(BF16), wup_ref[...])
        la = _dot(lora_in.astype(BF16), aup_ref[...])
        g = _dot(jax.nn.sigmoid(lora_in).astype(BF16), gup_ref[...])
        kk = k * kk_ref[...]
        kk_sq = _head_sum(kk * kk, eseg)
        yield
        logw = (-math.exp(-0.5) * math.log2(math.e)) * jax.nn.sigmoid(w0_ref[...] + lw)
        gc = _dot_ones_lhs(tri, logw)
        yield
        a = jax.nn.sigmoid(a0_ref[...] + la)
        kk = kk * lax.rsqrt(jnp.maximum(kk_sq, 1e-24))
        kmod = k * (1.0 + (a - 1.0) * ka_ref[...])
        kka = kk * a
        bonus = _head_sum(r * kmod * rk_ref[...], eseg) * v
        yield
        gc_last = gc[c_len - 1:c_len, :]
        e_inv = jnp.exp2(-gc)
        e_end = jnp.exp2(gc_last - gc)
        ops = dict(a_hat=-kk * jnp.exp2(gc - logw), k_hat=kmod * e_inv, b_hat=kka * e_inv, kbar=kmod * e_end,
                   bbar=kka * e_end, v=v, r_til=r * jnp.exp2(gc), bonus=bonus, g=g)
        rows = slice(chunk * c_len, (chunk + 1) * c_len)
        for n, name in enumerate(OPERANDS):
            ops_ref[rows, n * d_a:(n + 1) * d_a] = ops[name].astype(BF16)
        wend_ref[chunk] = jnp.exp2(gc_last)

    def ffn(out):
        x = x_ref[...]
        h = _rmsnorm(x, ln1_ref[...]).astype(BF16)
        g = _dot(h, wg_ref[...])
        yield
        u = _dot(h, wu_ref[...])
        yield
        act = (g * jax.nn.sigmoid(g) * u).astype(BF16)
        x1 = x + 0.5 * _dot(act, wd_ref[...])
        x1_ref[...] = x1
        yield
        h = _rmsnorm(x1, lnm_ref[...]).astype(BF16)
        out["pa"] = _dot(h, wia_ref[...])
        yield
        pb_ref[...] = _dot(h, wib_ref[...])

    if n_tiles == 1:
        out = {}
        _emit(ffn(out))
        pa = out["pa"]
        _emit(*[front_end(lambda c=c: pa[c * c_len:(c + 1) * c_len], lambda c=c: shift_ref[c], c)
                for c in range(n_chunks)])
        return

    @pl.when(step == 0)
    def _():
        pa_scr[...] = jnp.zeros(pa_scr.shape, F32)
        carry_ref[...] = shift_ref[0]

    def previous_tile():
        return [front_end(lambda c=c: pa_scr[c * c_len:(c + 1) * c_len, :],
                          lambda c=c: carry_ref[...] if c == 0 else pa_scr[c * c_len - 1:c * c_len, :], c)
                for c in range(n_chunks)]

    @pl.when(step < n_tiles)
    def _():
        out = {}
        _emit(ffn(out), *previous_tile())
        carry_ref[...] = jnp.where(step > 0, pa_scr[tm - 1:tm, :], carry_ref[...])
        pa_scr[...] = out["pa"]

    @pl.when(step == n_tiles)
    def _():
        _emit(*previous_tile())


def _resident(shape):
    return pl.BlockSpec(shape, lambda *_: (0,) * len(shape), pipeline_mode=pl.Buffered(1))


def _token_tile(n, tile):
    tm = min(tile, n)
    assert n % tm == 0
    return tm


def _pre_call(x, shift, c_len, ln1, wg, wu, wd, lnm, wia, wib, *front_consts):
    n, d = x.shape
    a_cols, b_cols = wia.shape[1], wib.shape[1]
    d_a = front_consts[1].shape[1]
    tm = _token_tile(n, PRE_TILE)
    n_tiles = n // tm
    n_chunks = tm // c_len
    assert (shift.shape[0] == 1) if n_tiles > 1 else (shift.shape[0] == n_chunks)
    lag = 1 if n_tiles > 1 else 0
    cur = lambda w: pl.BlockSpec((tm, w), lambda i: (jnp.minimum(i, n_tiles - 1), 0))
    late = lambda shp: pl.BlockSpec(shp, lambda i: (jnp.maximum(i - lag, 0),) + (0,) * (len(shp) - 1))
    consts = [ln1, wg, wu, wd, lnm, wia, wib, *front_consts]
    return pl.pallas_call(
        functools.partial(_pre_kernel, c_len, n_tiles),
        grid=(n_tiles + lag,),
        in_specs=[cur(d), _resident(shift.shape)] + [_resident(w.shape) for w in consts],
        out_specs=[cur(d), cur(b_cols), late((tm, len(OPERANDS) * d_a)), late((n_chunks, 1, d_a)),
                   late((n_chunks, 1, a_cols))],
        out_shape=[jax.ShapeDtypeStruct((n, d), F32), jax.ShapeDtypeStruct((n, b_cols), F32),
                   jax.ShapeDtypeStruct((n, len(OPERANDS) * d_a), BF16),
                   jax.ShapeDtypeStruct((n // c_len, 1, d_a), F32),
                   jax.ShapeDtypeStruct((n // c_len, 1, a_cols), F32)],
        scratch_shapes=[pltpu.VMEM((tm, a_cols), F32), pltpu.VMEM((1, a_cols), F32)],
        compiler_params=pltpu.CompilerParams(dimension_semantics=("arbitrary",), vmem_limit_bytes=VMEM_LIMIT),
        name="pre_ffn_inproj",
    )(x, shift, *consts)


def _post_kernel(final_norm, x1_ref, oa_ref, ob_ref, woa_ref, wob_ref, ln2_ref, wg_ref, wu_ref, wd_ref, lnf_ref,
                 y_ref):
    x2 = x1_ref[...] + _dot(oa_ref[...].astype(BF16), woa_ref[...]) + _dot(ob_ref[...].astype(BF16), wob_ref[...])
    x3 = _swiglu_half_step(x2, ln2_ref[...], wg_ref, wu_ref, wd_ref)
    y_ref[...] = _rmsnorm(x3, lnf_ref[...]) if final_norm else x3


def _post_call(x1, oa, ob, woa, wob, ln2, wg, wu, wd, lnf, final_norm):
    n, d = x1.shape
    tm = _token_tile(n, POST_TILE)
    tok = lambda w: pl.BlockSpec((tm, w), lambda i: (i, 0))
    return pl.pallas_call(
        functools.partial(_post_kernel, final_norm),
        grid=(n // tm,),
        in_specs=[tok(d), tok(oa.shape[1]), tok(ob.shape[1]), _resident(woa.shape), _resident(wob.shape),
                  _resident(ln2.shape), _resident(wg.shape), _resident(wu.shape), _resident(wd.shape),
                  _resident(lnf.shape)],
        out_specs=tok(d),
        out_shape=jax.ShapeDtypeStruct((n, d), F32),
        compiler_params=pltpu.CompilerParams(dimension_semantics=("arbitrary",), vmem_limit_bytes=VMEM_LIMIT),
        name="post_outproj_ffn",
    )(x1, oa, ob, woa, wob, ln2, wg, wu, wd, lnf)


def _rwkv_kernel(c_len, n_sub, ops_ref, wend_ref, wkv0_ref, lnw_ref, lnb_ref, eseg_ref, o_ref, wkv_ref, st_ref):
    step = pl.program_id(1)
    n_heads = wkv_ref.shape[1]
    n_pairs = n_heads // 2
    d_a = n_heads * A_HEAD
    tb = c_len * n_sub
    c2 = 2 * c_len

    r64, c64 = _iota2(A_HEAD, A_HEAD)
    eye64 = jnp.where(r64 == c64, 1.0, 0.0).astype(BF16)

    @pl.when(step == 0)
    def _():
        for j in range(n_pairs):
            st_ref[j] = jnp.concatenate([_transpose_sq(wkv0_ref[0, 2 * j], eye64),
                                         _transpose_sq(wkv0_ref[0, 2 * j + 1], eye64)], axis=1)

    cat = _cat
    eseg = eseg_ref[...]

    def operand(name, i, j=None):
        col = OPERANDS.index(name) * d_a
        cols = slice(col, col + d_a) if j is None else slice(col + j * LANE, col + (j + 1) * LANE)
        return ops_ref[0, i * c_len:(i + 1) * c_len, cols]

    rr, cc = _iota2(c_len, c2)
    col_t = cc & (c_len - 1)
    strict = col_t < rr
    lower = col_t <= rr
    eye_tt = jnp.where(col_t == rr, 1.0, 0.0).astype(F32)
    rk, ck = _iota2(A_HEAD, LANE)
    eye_kk = jnp.where((ck & (A_HEAD - 1)) == rk, 1.0, 0.0).astype(F32)
    half_k = lax.broadcasted_iota(jnp.int32, (1, LANE), 1) < A_HEAD
    half_t = lax.broadcasted_iota(jnp.int32, (1, c2), 1) < c_len
    n_double = int(math.log2(c_len))
    assert 2 ** n_double == c_len and n_double >= 2

    def bd(x, half):
        xb = x.astype(BF16)
        return cat([xb * jnp.where(half, 1.0, 0.0).astype(BF16), xb * jnp.where(half, 0.0, 1.0).astype(BF16)], 0)

    bd_k = lambda x: bd(x, half_k)
    bd_t = lambda x: bd(x, half_t)

    st = [st_ref[j] for j in range(n_pairs)]
    mixed = {}

    def mix(chunks):
        chains = [(i, j) for i in chunks for j in range(n_pairs)]
        each = lambda f: [f(c) for c in range(len(chains))]
        at = lambda name: [operand(name, i, j) for i, j in chains]
        a_t, r_t, k_t, b_t, v_t = at("a_hat"), at("r_til"), at("k_hat"), at("b_hat"), at("v")
        kbar_t, bbar_t = at("kbar"), at("bbar")
        w_end = [wend_ref[0, i, :, j * LANE:(j + 1) * LANE] for i, j in chains]
        bd_b = each(lambda c: bd_k(b_t[c]))
        bd_kh = each(lambda c: bd_k(k_t[c]))
        xa = each(lambda c: _mm(a_t[c], cat([bd_b[c], bd_kh[c]], 0), NT))
        yield
        xr = each(lambda c: _mm(r_t[c], cat([bd_kh[c], bd_b[c]], 0), NT))
        l_mat = each(lambda c: jnp.where(strict, xa[c][:, :c2], 0.0))
        a_ak = each(lambda c: jnp.where(strict, xa[c][:, c2:], 0.0))
        yield
        t_inv = each(lambda c: eye_tt + l_mat[c])
        pw = each(lambda c: _mm(l_mat[c], bd_t(l_mat[c])))
        a_rk_rb = each(lambda c: cat([jnp.where(lower, xr[c][:, :c2], 0.0), jnp.where(lower, xr[c][:, c2:], 0.0)], 1))
        yield
        for _ in range(1, n_double - 1):
            y = each(lambda c: _mm(pw[c], cat([bd_t(pw[c]), bd_t(t_inv[c])], 1)))
            t_inv = each(lambda c: t_inv[c] + y[c][:, c2:])
            pw = each(lambda c: y[c][:, :c2])
            yield
        t_inv = each(lambda c: t_inv[c] + _mm(pw[c], bd_t(t_inv[c])))
        bd_v = each(lambda c: bd_k(v_t[c]))
        a_ak_v = each(lambda c: _mm(a_ak[c], bd_v[c]))
        yield
        pq = each(lambda c: _mm(t_inv[c], cat([bd_k(a_t[c]), bd_k(a_ak_v[c])], 1)))
        p_t = each(lambda c: pq[c][:, :LANE])
        q_t = each(lambda c: pq[c][:, LANE:])
        yield
        hg = each(lambda c: _mm(a_rk_rb[c], cat([cat([bd_v[c], jnp.zeros((c2, LANE), BF16)], 1),
                                                 cat([bd_k(q_t[c]), bd_k(p_t[c])], 1)], 0)))
        yield
        nm = each(lambda c: _mm(cat([kbar_t[c], bbar_t[c]], 0),
                                cat([cat([v_t[c], jnp.zeros((c_len, LANE), BF16)], 1),
                                     cat([q_t[c], p_t[c]], 1).astype(BF16)], 0), TN))
        n_sbs = each(lambda c: jnp.where(half_k, nm[c][:A_HEAD, :LANE], nm[c][A_HEAD:, :LANE]))
        m_sbs = each(lambda c: jnp.where(half_k, nm[c][:A_HEAD, LANE:], nm[c][A_HEAD:, LANE:]) + eye_kk * w_end[c])
        g_pair = each(lambda c: r_t[c] + hg[c][:, LANE:])
        h_pair = each(lambda c: hg[c][:, :LANE])
        yield
        for n, i in enumerate(chunks):
            out_tiles = []
            for j in range(n_pairs):
                c = n * n_pairs + j
                both = _mm(cat([m_sbs[c], g_pair[c]], 0), bd_k(st[j]))
                out_tiles.append(both[A_HEAD:] + h_pair[c])
                st[j] = both[:A_HEAD] + n_sbs[c]
            mixed[i] = cat(out_tiles, 1)
            yield

    def post(i):
        o = mixed[i]
        inv_n = 1.0 / A_HEAD
        mean = _head_sum(o, eseg) * inv_n
        cen = o - mean
        yield
        var = _head_sum(cen * cen, eseg) * inv_n
        on = cen * lax.rsqrt(var + LNX_EPS) * lnw_ref[...] + lnb_ref[...]
        yield
        o_ref[0, i * c_len:(i + 1) * c_len, :] = (on + operand("bonus", i)) * operand("g", i)

    groups = [list(range(s, min(s + RWKV_CHUNKS_PER_GROUP, n_sub))) for s in range(0, n_sub, RWKV_CHUNKS_PER_GROUP)]
    for n, grp in enumerate(groups):
        _emit(mix(grp), *([post(i) for i in groups[n - 1]] if n > 0 else []))
    _emit(*[post(i) for i in groups[-1]])
    for j in range(n_pairs):
        st_ref[j] = st[j]

    @pl.when(step == pl.num_programs(1) - 1)
    def _():
        for j in range(n_pairs):
            st_j = st_ref[j]
            wkv_ref[0, 2 * j] = _transpose_sq(st_j[:, :A_HEAD], eye64)
            wkv_ref[0, 2 * j + 1] = _transpose_sq(st_j[:, A_HEAD:], eye64)


def _rwkv_call(ops, wend, wkv0, c_len, lnw, lnb, eseg):
    b, t, ops_cols = ops.shape
    n_heads = wkv0.shape[1]
    d_a = n_heads * A_HEAD
    n_chunks = t // c_len
    n_sub = math.gcd(n_chunks, RWKV_CHUNKS_PER_STEP)
    seq = lambda w: pl.BlockSpec((1, c_len * n_sub, w), lambda i, j: (i, j, 0))
    per_b = lambda shp: pl.BlockSpec((1,) + shp, lambda i, j: (i,) + (0,) * len(shp))
    consts = [lnw, lnb, eseg]
    return pl.pallas_call(
        functools.partial(_rwkv_kernel, c_len, n_sub),
        grid=(b, n_chunks // n_sub),
        in_specs=[seq(ops_cols), pl.BlockSpec((1, n_sub, 1, d_a), lambda i, j: (i, j, 0, 0)), per_b(wkv0.shape[1:])]
        + [_resident(w.shape) for w in consts],
        out_specs=[seq(d_a), per_b(wkv0.shape[1:])],
        out_shape=[jax.ShapeDtypeStruct((b, t, d_a), F32), jax.ShapeDtypeStruct(wkv0.shape, F32)],
        scratch_shapes=[pltpu.VMEM((n_heads // 2, A_HEAD, LANE), F32)],
        compiler_params=pltpu.CompilerParams(dimension_semantics=("arbitrary", "arbitrary"),
                                             vmem_limit_bytes=VMEM_LIMIT),
        name="rwkv7_mixer",
    )(ops, wend, wkv0, *consts)


def _gla_kernel(c_len, n_chunks, pb_ref, gla0_ref, gkup_ref, gkb_ref, ng_ref, o_ref, st_ref):
    step = pl.program_id(1)
    n_pairs = st_ref.shape[1]
    dk = st_ref.shape[2] // 2
    dv = st_ref.shape[3]
    assert 2 * dk == LANE and dv == LANE
    n_heads = 2 * n_pairs
    nk = n_heads * dk
    nv = n_heads * dv
    tb = c_len * n_chunks

    @pl.when(step == 0)
    def _():
        st_ref[...] = gla0_ref[...]

    def cat(xs, axis):
        return jnp.concatenate(xs, axis=axis)

    p = pb_ref[0]
    q = p[:, 0:nk] * (dk ** -0.5)
    k = p[:, nk:2 * nk]
    v = p[:, 2 * nk:2 * nk + nv]
    og = p[:, 2 * nk + nv:2 * nk + 2 * nv]
    xgk = p[:, 2 * nk + 2 * nv:2 * nk + 2 * nv + LANE]
    z = _dot(xgk.astype(BF16), gkup_ref[...]) + gkb_ref[...]
    log_a = -_softplus(-z) / B_TAU

    sub = min(SUB, c_len)
    n_sub = c_len // sub
    row, col = _iota2(tb, tb)
    in_chunk = col >= (row & -c_len)
    tri = jnp.where((col <= row) & in_chunk, 1.0, 0.0).astype(BF16)
    b = _dot_ones_lhs(tri, log_a)
    if n_sub > 1:
        tri_start = jnp.where((col < (row & -sub)) & in_chunk, 1.0, 0.0).astype(BF16)
        b_start = _dot_ones_lhs(tri_start, log_a)
    else:
        b_start = jnp.zeros_like(b)
    q_inter = q * jnp.exp(b)
    q_local = q * jnp.exp(b - b_start)
    rows_of = lambda x, i: x[i * c_len:(i + 1) * c_len]
    b_last = [b[(i + 1) * c_len - 1:(i + 1) * c_len, :] for i in range(n_chunks)]
    k_state = cat([rows_of(k, i) * jnp.exp(b_last[i] - rows_of(b, i)) for i in range(n_chunks)], 0)
    row1 = lax.broadcasted_iota(jnp.int32, (c_len, 1), 0)
    k_local = []
    for i in range(n_chunks):
        per_sub = []
        for s in range(n_sub):
            ref = b_start[i * c_len + s * sub:i * c_len + s * sub + 1, :]
            expo = jnp.where(row1 < (s + 1) * sub, ref - rows_of(b, i), -1e30)
            per_sub.append(rows_of(k, i) * jnp.exp(expo))
        k_local.append(per_sub)

    half0 = lax.broadcasted_iota(jnp.int32, (1, LANE), 1) < dk
    rc, cc = _iota2(c_len, c_len)
    causal = rc >= cc
    top_rows = lax.broadcasted_iota(jnp.int32, (LANE, 1), 0) < dk
    log_a_hi, log_a_lo = _split2(log_a)
    ones_cv = jnp.ones((c_len, dv), BF16)
    kt = lambda x, i, j: x[i * c_len:(i + 1) * c_len, j * LANE:(j + 1) * LANE]
    vt = lambda x, i, h: x[i * c_len:(i + 1) * c_len, h * dv:(h + 1) * dv]
    pairs = [(i, j) for i in range(n_chunks) for j in range(n_pairs)]
    heads = [(i, h) for i in range(n_chunks) for h in range(n_heads)]

    def a_rows(i, j, s):
        ql = kt(q_local, i, j)[s * sub:(s + 1) * sub]
        lhs = cat([jnp.where(half0, ql, 0.0), jnp.where(half0, 0.0, ql)], 0)
        return _mm(lhs, kt(k_local[i][s], 0, j), NT)
    a_blk = {(i, j): [a_rows(i, j, s) for s in range(n_sub)] for i, j in pairs}
    a_mat = {}
    for i, h in heads:
        blocks = [a_blk[(i, h // 2)][s][(h % 2) * sub:(h % 2 + 1) * sub] for s in range(n_sub)]
        a_mat[(i, h)] = jnp.where(causal, cat(blocks, 0) if n_sub > 1 else blocks[0], 0.0)
    o_intra = {(i, h): _mm(a_mat[(i, h)], vt(v, i, h)) for i, h in heads}
    upd = {}
    dec = {}
    for i, j in pairs:
        kv = _mm(kt(k_state, i, j), v[i * c_len:(i + 1) * c_len, 2 * j * dv:(2 * j + 2) * dv], TN)
        upd[(i, j)] = jnp.where(top_rows, kv[:, :dv], kv[:, dv:])
        col_sum = lambda piece: lax.dot_general(kt(piece, i, j), ones_cv, (TN, ((), ())), preferred_element_type=F32)
        dec[(i, j)] = jnp.exp(col_sum(log_a_hi) + col_sum(log_a_lo))
    states = {}
    for j in range(n_pairs):
        s_cur = st_ref[0, j]
        for i in range(n_chunks):
            states[(i, j)] = s_cur
            s_cur = dec[(i, j)] * s_cur + upd[(i, j)]
        st_ref[0, j] = s_cur
    out_rows = []
    for i in range(n_chunks):
        outs = []
        for h in range(n_heads):
            qi = kt(q_inter, i, h // 2)
            qi = jnp.where(half0, qi, 0.0) if h % 2 == 0 else jnp.where(half0, 0.0, qi)
            o_h = o_intra[(i, h)] + _mm(qi, states[(i, h // 2)])
            o_h = o_h * lax.rsqrt(jnp.mean(o_h * o_h, axis=-1, keepdims=True) + EPS) * ng_ref[...]
            og_h = vt(og, i, h)
            outs.append(o_h * (og_h * jax.nn.sigmoid(og_h)))
        out_rows.append(cat(outs, 1))
    o_ref[0] = cat(out_rows, 0) if n_chunks > 1 else out_rows[0]


def _gla_call(pb, gla0, c_len, gkup, gkb, ng):
    b, t, cols = pb.shape
    n_heads, dk, dv = gla0.shape[1:]
    n_chunks = t // c_len
    per_step = math.gcd(n_chunks, GLA_CHUNKS_PER_STEP)
    tb = c_len * per_step
    st_shape = (n_heads // 2, 2 * dk, dv)
    seq = lambda w: pl.BlockSpec((1, tb, w), lambda i, j: (i, j, 0))
    per_b = lambda shp: pl.BlockSpec((1,) + shp, lambda i, j: (i,) + (0,) * len(shp))
    consts = [gkup, gkb, ng]
    o, st = pl.pallas_call(
        functools.partial(_gla_kernel, c_len, per_step),
        grid=(b, n_chunks // per_step),
        in_specs=[seq(cols), per_b(st_shape)] + [_resident(w.shape) for w in consts],
        out_specs=[seq(n_heads * dv), per_b(st_shape)],
        out_shape=[jax.ShapeDtypeStruct((b, t, n_heads * dv), F32), jax.ShapeDtypeStruct((b,) + st_shape, F32)],
        compiler_params=pltpu.CompilerParams(dimension_semantics=("arbitrary", "arbitrary"),
                                             vmem_limit_bytes=VMEM_LIMIT),
        name="gla_mixer",
    )(pb, gla0.reshape((b,) + st_shape), *consts)
    return o, st.reshape(gla0.shape)


def _chunk_len(t):
    return GLA_CHUNK if t % GLA_CHUNK == 0 else t


def _pad_rows(w, start, total):
    return jnp.zeros((total, w.shape[1]), w.dtype).at[start:start + w.shape[0]].set(w)


def _prep_layer(l, ln1_g, ffn1_wg, ffn1_wu, ffn1_wd, ln_mix_g, w_in, mu_shift, w0, w_lora_up, a0, a_lora_up,
                g_lora_up, k_k, k_a, r_k, lnx_w, lnx_b, gk_up, gk_b, gla_norm_g, w_out, ln2_g, ffn2_wg, ffn2_wu,
                ffn2_wd):
    d_a = w0.shape[1]
    a_cols = mu_shift.shape[1]
    nk = gk_b.shape[1]
    d_b = w_out.shape[1] - d_a
    row = lambda x: x[l].reshape(1, -1)
    wi = w_in[l]
    wi_b = wi[:, a_cols:]
    q_k_v = wi_b[:, :2 * nk + d_b]
    xgk_w = wi_b[:, 2 * nk + d_b:2 * nk + d_b + B_GATE_RANK]
    og_w = wi_b[:, 2 * nk + d_b + B_GATE_RANK:]
    pad = jnp.zeros((wi.shape[0], LANE - B_GATE_RANK), wi.dtype)
    wib = jnp.concatenate([q_k_v, og_w, xgk_w, pad], axis=1)
    eseg = jnp.kron(jnp.eye(HEAD_SUM_TILE // A_HEAD, dtype=F32), jnp.ones((A_HEAD, A_HEAD), F32)).astype(BF16)
    return dict(
        pre=(row(ln1_g), ffn1_wg[l].astype(BF16), ffn1_wu[l].astype(BF16), ffn1_wd[l].astype(BF16), row(ln_mix_g),
             wi[:, :a_cols].astype(BF16), wib.astype(BF16),
             row(mu_shift), row(w0), _pad_rows(w_lora_up[l], 0, LANE).astype(BF16), row(a0),
             _pad_rows(a_lora_up[l], A_W_RANK, LANE).astype(BF16),
             _pad_rows(g_lora_up[l], A_W_RANK + A_A_RANK, LANE).astype(BF16), row(k_k), row(k_a), row(r_k), eseg),
        rwkv=(row(lnx_w), row(lnx_b), eseg),
        gla=(_pad_rows(gk_up[l], 0, LANE).astype(BF16), row(gk_b), row(gla_norm_g)),
        post=(w_out[l, :d_a].astype(BF16), w_out[l, d_a:].astype(BF16), row(ln2_g), ffn2_wg[l].astype(BF16),
              ffn2_wu[l].astype(BF16), ffn2_wd[l].astype(BF16)),
    )


def _trunk(x, shift, wkv, gla, layers, ln_f):
    b, t, d = x.shape
    c_len = _chunk_len(t)
    xf = x.reshape(b * t, d)
    new_shift, new_wkv, new_gla = [], [], []
    for l, lw in enumerate(layers):
        x1, pb, ops, wend, last = _pre_call(xf, shift[l], c_len, *lw["pre"])
        oa, s_a = _rwkv_call(ops.reshape(b, t, -1), wend.reshape(b, t // c_len, 1, -1), wkv[l], c_len, *lw["rwkv"])
        ob, s_b = _gla_call(pb.reshape(b, t, -1), gla[l], c_len, *lw["gla"])
        xf = _post_call(x1, oa.reshape(b * t, -1), ob.reshape(b * t, -1), *lw["post"], ln_f,
                        final_norm=(l == len(layers) - 1))
        new_shift.append(last.reshape(b, t // c_len, 1, -1)[:, -1])
        new_wkv.append(s_a)
        new_gla.append(s_b)
    return xf.reshape(b, t, d), jnp.stack(new_shift), jnp.stack(new_wkv), jnp.stack(new_gla)


def kernel(x_prompt, x_sample, state_shift, state_wkv, state_gla, ln1_g, ffn1_wg, ffn1_wu, ffn1_wd, ln_mix_g, w_in,
           mu_shift, w0, w_lora_up, a0, a_lora_up, g_lora_up, k_k, k_a, r_k, lnx_w, lnx_b, gk_up, gk_b, gla_norm_g,
           w_out, ln2_g, ffn2_wg, ffn2_wu, ffn2_wd, ln_f_g):
    depth = ln1_g.shape[0]
    per_layer = (ln1_g, ffn1_wg, ffn1_wu, ffn1_wd, ln_mix_g, w_in, mu_shift, w0, w_lora_up, a0, a_lora_up, g_lora_up,
                 k_k, k_a, r_k.reshape(depth, -1), lnx_w, lnx_b, gk_up, gk_b, gla_norm_g, w_out, ln2_g, ffn2_wg,
                 ffn2_wu, ffn2_wd)
    layers = [_prep_layer(l, *per_layer) for l in range(depth)]
    ln_f = ln_f_g.reshape(1, -1)
    bp = x_prompt.shape[0]
    shift0 = jnp.zeros((depth, bp) + state_shift.shape[2:], F32)
    wkv0 = jnp.zeros((depth, bp) + state_wkv.shape[2:], F32)
    gla0 = jnp.zeros((depth, bp) + state_gla.shape[2:], F32)
    y_p, shift_p, wkv_p, gla_p = _trunk(x_prompt, shift0, wkv0, gla0, layers, ln_f)
    y_s, shift_s, wkv_s, gla_s = _trunk(x_sample, state_shift, state_wkv, state_gla, layers, ln_f)
    return (y_p, y_s, shift_p, wkv_p, gla_p, shift_s, wkv_s, gla_s)
```

```python
import functools
import math

import jax
import jax.numpy as jnp
from jax import lax
from jax.experimental import pallas as pl
from jax.experimental.pallas import tpu as pltpu

F32 = jnp.float32
BF16 = jnp.bfloat16
HIGHEST = lax.Precision.HIGHEST

EPS = 1e-6
LNX_EPS = 64e-5
A_HEAD = 64
A_W_RANK = 32
A_A_RANK = 32
A_G_RANK = 64
B_HEADS = 4
B_GATE_RANK = 16
B_TAU = 16.0
GLA_CHUNK = 64
SUB = 16
LANE = 128
VMEM_LIMIT = 56 * 1024 * 1024
PRE_TILE = 256
POST_TILE = 512
RWKV_CHUNKS_PER_STEP = 8
RWKV_SEQUENCES_PER_STEP = 16
RWKV_CHUNKS_PER_GROUP = 4
GLA_CHUNKS_PER_STEP = 4
GLA_SEQUENCES_PER_STEP = 16
HEAD_SUM_TILE = 256

NN = ((1,), (0,))
NT = ((1,), (1,))
TN = ((0,), (0,))


def _dot(a, b, precision=None):
    return jnp.dot(a, b, preferred_element_type=F32, precision=precision)


def _split2(x):
    hi = x.astype(BF16)
    return hi, (x - hi.astype(F32)).astype(BF16)


def _mm(a, b, dims=NN):
    return lax.dot_general(a.astype(BF16), b.astype(BF16), (dims, ((), ())), preferred_element_type=F32)


def _dot_ones_lhs(ones_bf16, x):
    hi, lo = _split2(x)
    return _dot(ones_bf16, hi) + _dot(ones_bf16, lo)


def _softplus(x):
    return jnp.maximum(x, 0.0) + jnp.log(1.0 + jnp.exp(-jnp.abs(x)))


def _rmsnorm(x, g):
    return x * lax.rsqrt(jnp.mean(x * x, axis=-1, keepdims=True) + EPS) * g


def _swiglu_half_step(x, ln_g, wg_ref, wu_ref, wd_ref):
    h = _rmsnorm(x, ln_g).astype(BF16)
    g = _dot(h, wg_ref[...])
    u = _dot(h, wu_ref[...])
    act = (g * jax.nn.sigmoid(g) * u).astype(BF16)
    return x + 0.5 * _dot(act, wd_ref[...])


def _iota2(n, m):
    return lax.broadcasted_iota(jnp.int32, (n, m), 0), lax.broadcasted_iota(jnp.int32, (n, m), 1)


def _cat(xs, axis):
    return jnp.concatenate(xs, axis=axis)


def _emit(*tasks):
    tasks = list(tasks)
    while tasks:
        for task in list(tasks):
            if next(task, True):
                tasks.remove(task)


def _head_sum(x, eseg):
    w = eseg.shape[0]
    return _cat([_dot(x[:, s:s + w].astype(BF16), eseg) for s in range(0, x.shape[1], w)], 1)


OPERANDS = ("a_hat", "k_hat", "b_hat", "kbar", "bbar", "v", "r_til", "bonus", "g")


def _pre_kernel(c_len, n_tiles, x_ref, shift_ref, ln1_ref, wg_ref, wu_ref, wd_ref, lnm_ref, wia_ref, wib_ref, mu_ref,
                w0_ref, wup_ref, a0_ref, aup_ref, gup_ref, kk_ref, ka_ref, rk_ref, eseg_ref,
                x1_ref, pb_ref, ops_ref, wend_ref, last_ref, pa_scr, carry_ref):
    step = pl.program_id(0)
    tm = x_ref.shape[0]
    d_a = w0_ref.shape[1]
    n_chunks = tm // c_len
    eseg = eseg_ref[...]
    row1 = lax.broadcasted_iota(jnp.int32, (c_len, 1), 0)
    rt, ct = _iota2(c_len, c_len)
    tri = jnp.where(ct <= rt, 1.0, 0.0).astype(BF16)

    def front_end(load_p, load_prev, chunk):
        p = load_p()
        shifted = jnp.where(row1 == 0, load_prev(), pltpu.roll(p, 1, axis=0))
        last_ref[chunk] = p[c_len - 1:c_len, :]
        xs = p + (shifted - p) * mu_ref[...]
        r = xs[:, 0:d_a]
        k = xs[:, d_a:2 * d_a]
        v = xs[:, 2 * d_a:3 * d_a]
        lora_in = xs[:, 3 * d_a:3 * d_a + LANE]
        yield
        lw = _dot(jnp.tanh(lora_in).astype(BF16), wup_ref[...])
        la = _dot(lora_in.astype(BF16), aup_ref[...])
        g = _dot(jax.nn.sigmoid(lora_in).astype(BF16), gup_ref[...])
        kk = k * kk_ref[...]
        kk_sq = _head_sum(kk * kk, eseg)
        yield
        logw = (-math.exp(-0.5) * math.log2(math.e)) * jax.nn.sigmoid(w0_ref[...] + lw)
        gc = _dot_ones_lhs(tri, logw)
        yield
        a = jax.nn.sigmoid(a0_ref[...] + la)
        kk = kk * lax.rsqrt(jnp.maximum(kk_sq, 1e-24))
        kmod = k * (1.0 + (a - 1.0) * ka_ref[...])
        kka = kk * a
        bonus = _head_sum(r * kmod * rk_ref[...], eseg) * v
        yield
        gc_last = gc[c_len - 1:c_len, :]
        e_inv = jnp.exp2(-gc)
        e_end = jnp.exp2(gc_last - gc)
        ops = dict(a_hat=-kk * jnp.exp2(gc - logw), k_hat=kmod * e_inv, b_hat=kka * e_inv, kbar=kmod * e_end,
                   bbar=kka * e_end, v=v, r_til=r * jnp.exp2(gc), bonus=bonus, g=g)
        rows = slice(chunk * c_len, (chunk + 1) * c_len)
        for n, name in enumerate(OPERANDS):
            ops_ref[rows, n * d_a:(n + 1) * d_a] = ops[name].astype(BF16)
        wend_ref[chunk] = jnp.exp2(gc_last)

    def ffn(out):
        x = x_ref[...]
        h = _rmsnorm(x, ln1_ref[...]).astype(BF16)
        g = _dot(h, wg_ref[...])
        yield
        u = _dot(h, wu_ref[...])
        yield
        act = (g * jax.nn.sigmoid(g) * u).astype(BF16)
        x1 = x + 0.5 * _dot(act, wd_ref[...])
        x1_ref[...] = x1
        yield
        h = _rmsnorm(x1, lnm_ref[...]).astype(BF16)
        out["pa"] = _dot(h, wia_ref[...])
        yield
        pb_ref[...] = _dot(h, wib_ref[...])

    if n_tiles == 1:
        out = {}
        _emit(ffn(out))
        pa = out["pa"]
        _emit(*[front_end(lambda c=c: pa[c * c_len:(c + 1) * c_len], lambda c=c: shift_ref[c], c)
                for c in range(n_chunks)])
        return

    @pl.when(step == 0)
    def _():
        pa_scr[...] = jnp.zeros(pa_scr.shape, F32)
        carry_ref[...] = shift_ref[0]

    def previous_tile():
        return [front_end(lambda c=c: pa_scr[c * c_len:(c + 1) * c_len, :],
                          lambda c=c: carry_ref[...] if c == 0 else pa_scr[c * c_len - 1:c * c_len, :], c)
                for c in range(n_chunks)]

    @pl.when(step < n_tiles)
    def _():
        out = {}
        _emit(ffn(out), *previous_tile())
        carry_ref[...] = jnp.where(step > 0, pa_scr[tm - 1:tm, :], carry_ref[...])
        pa_scr[...] = out["pa"]

    @pl.when(step == n_tiles)
    def _():
        _emit(*previous_tile())


def _resident(shape):
    return pl.BlockSpec(shape, lambda *_: (0,) * len(shape), pipeline_mode=pl.Buffered(1))


def _token_tile(n, tile):
    tm = min(tile, n)
    assert n % tm == 0
    return tm


def _pre_call(x, shift, c_len, ln1, wg, wu, wd, lnm, wia, wib, *front_consts):
    n, d = x.shape
    a_cols, b_cols = wia.shape[1], wib.shape[1]
    d_a = front_consts[1].shape[1]
    tm = _token_tile(n, PRE_TILE)
    n_tiles = n // tm
    n_chunks = tm // c_len
    assert (shift.shape[0] == 1) if n_tiles > 1 else (shift.shape[0] == n_chunks)
    lag = 1 if n_tiles > 1 else 0
    cur = lambda w: pl.BlockSpec((tm, w), lambda i: (jnp.minimum(i, n_tiles - 1), 0))
    late = lambda shp: pl.BlockSpec(shp, lambda i: (jnp.maximum(i - lag, 0),) + (0,) * (len(shp) - 1))
    consts = [ln1, wg, wu, wd, lnm, wia, wib, *front_consts]
    return pl.pallas_call(
        functools.partial(_pre_kernel, c_len, n_tiles),
        grid=(n_tiles + lag,),
        in_specs=[cur(d), _resident(shift.shape)] + [_resident(w.shape) for w in consts],
        out_specs=[cur(d), cur(b_cols), late((tm, len(OPERANDS) * d_a)), late((n_chunks, 1, d_a)),
                   late((n_chunks, 1, a_cols))],
        out_shape=[jax.ShapeDtypeStruct((n, d), F32), jax.ShapeDtypeStruct((n, b_cols), F32),
                   jax.ShapeDtypeStruct((n, len(OPERANDS) * d_a), BF16),
                   jax.ShapeDtypeStruct((n // c_len, 1, d_a), F32),
                   jax.ShapeDtypeStruct((n // c_len, 1, a_cols), F32)],
        scratch_shapes=[pltpu.VMEM((tm, a_cols), F32), pltpu.VMEM((1, a_cols), F32)],
        compiler_params=pltpu.CompilerParams(dimension_semantics=("arbitrary",), vmem_limit_bytes=VMEM_LIMIT),
        name="pre_ffn_inproj",
    )(x, shift, *consts)


def _post_kernel(final_norm, x1_ref, oa_ref, ob_ref, woa_ref, wob_ref, ln2_ref, wg_ref, wu_ref, wd_ref, lnf_ref,
                 y_ref):
    x2 = x1_ref[...] + _dot(oa_ref[...].astype(BF16), woa_ref[...]) + _dot(ob_ref[...].astype(BF16), wob_ref[...])
    x3 = _swiglu_half_step(x2, ln2_ref[...], wg_ref, wu_ref, wd_ref)
    y_ref[...] = _rmsnorm(x3, lnf_ref[...]) if final_norm else x3


def _post_call(x1, oa, ob, woa, wob, ln2, wg, wu, wd, lnf, final_norm):
    n, d = x1.shape
    tm = _token_tile(n, POST_TILE)
    tok = lambda w: pl.BlockSpec((tm, w), lambda i: (i, 0))
    return pl.pallas_call(
        functools.partial(_post_kernel, final_norm),
        grid=(n // tm,),
        in_specs=[tok(d), tok(oa.shape[1]), tok(ob.shape[1]), _resident(woa.shape), _resident(wob.shape),
                  _resident(ln2.shape), _resident(wg.shape), _resident(wu.shape), _resident(wd.shape),
                  _resident(lnf.shape)],
        out_specs=tok(d),
        out_shape=jax.ShapeDtypeStruct((n, d), F32),
        compiler_params=pltpu.CompilerParams(dimension_semantics=("arbitrary",), vmem_limit_bytes=VMEM_LIMIT),
        name="post_outproj_ffn",
    )(x1, oa, ob, woa, wob, ln2, wg, wu, wd, lnf)


def _rwkv_kernel(c_len, n_sub, chained, ops_ref, wend_ref, wkv0_ref, lnw_ref, lnb_ref, eseg_ref, o_ref, wkv_ref,
                 st_ref):
    step = pl.program_id(1)
    n_pairs = wkv_ref.shape[1]
    d_a = n_pairs * LANE
    c2 = 2 * c_len

    if chained:
        @pl.when(step == 0)
        def _():
            st_ref[...] = wkv0_ref[0]

    cat = _cat
    eseg = eseg_ref[...]

    def operand(name, i, j=None):
        col = OPERANDS.index(name) * d_a
        cols = slice(col, col + d_a) if j is None else slice(col + j * LANE, col + (j + 1) * LANE)
        return ops_ref[0, i * c_len:(i + 1) * c_len, cols]

    rr, cc = _iota2(c_len, c2)
    col_t = cc & (c_len - 1)
    strict = col_t < rr
    lower = col_t <= rr
    eye_tt = jnp.where(col_t == rr, 1.0, 0.0).astype(F32)
    rk, ck = _iota2(A_HEAD, LANE)
    eye_kk = jnp.where((ck & (A_HEAD - 1)) == rk, 1.0, 0.0).astype(F32)
    half_k = lax.broadcasted_iota(jnp.int32, (1, LANE), 1) < A_HEAD
    half_t = lax.broadcasted_iota(jnp.int32, (1, c2), 1) < c_len
    n_double = int(math.log2(c_len))
    assert 2 ** n_double == c_len and n_double >= 2

    def bd(x, half):
        xb = x.astype(BF16)
        return cat([xb * jnp.where(half, 1.0, 0.0).astype(BF16), xb * jnp.where(half, 0.0, 1.0).astype(BF16)], 0)

    bd_k = lambda x: bd(x, half_k)
    bd_t = lambda x: bd(x, half_t)
    st = [st_ref[j] for j in range(n_pairs)] if chained else None
    mixed = {}

    def mix(chunks):
        chains = [(i, j) for i in chunks for j in range(n_pairs)]
        each = lambda f: [f(c) for c in range(len(chains))]
        at = lambda name: [operand(name, i, j) for i, j in chains]
        a_t, r_t, k_t, b_t, v_t = at("a_hat"), at("r_til"), at("k_hat"), at("b_hat"), at("v")
        kbar_t, bbar_t = at("kbar"), at("bbar")
        w_end = [wend_ref[0, i, :, j * LANE:(j + 1) * LANE] for i, j in chains]
        bd_b = each(lambda c: bd_k(b_t[c]))
        bd_kh = each(lambda c: bd_k(k_t[c]))
        xa = each(lambda c: _mm(a_t[c], cat([bd_b[c], bd_kh[c]], 0), NT))
        yield
        xr = each(lambda c: _mm(r_t[c], cat([bd_kh[c], bd_b[c]], 0), NT))
        l_mat = each(lambda c: jnp.where(strict, xa[c][:, :c2], 0.0))
        a_ak = each(lambda c: jnp.where(strict, xa[c][:, c2:], 0.0))
        yield
        t_inv = each(lambda c: eye_tt + l_mat[c])
        pw = each(lambda c: _mm(l_mat[c], bd_t(l_mat[c])))
        a_rk_rb = each(lambda c: cat([jnp.where(lower, xr[c][:, :c2], 0.0), jnp.where(lower, xr[c][:, c2:], 0.0)], 1))
        yield
        for _ in range(1, n_double - 1):
            y = each(lambda c: _mm(pw[c], cat([bd_t(pw[c]), bd_t(t_inv[c])], 1)))
            t_inv = each(lambda c: t_inv[c] + y[c][:, c2:])
            pw = each(lambda c: y[c][:, :c2])
            yield
        t_inv = each(lambda c: t_inv[c] + _mm(pw[c], bd_t(t_inv[c])))
        bd_v = each(lambda c: bd_k(v_t[c]))
        a_ak_v = each(lambda c: _mm(a_ak[c], bd_v[c]))
        yield
        pq = each(lambda c: _mm(t_inv[c], cat([bd_k(a_t[c]), bd_k(a_ak_v[c])], 1)))
        p_t = each(lambda c: pq[c][:, :LANE])
        q_t = each(lambda c: pq[c][:, LANE:])
        yield
        hg = each(lambda c: _mm(a_rk_rb[c], cat([cat([bd_v[c], jnp.zeros((c2, LANE), BF16)], 1),
                                                 cat([bd_k(q_t[c]), bd_k(p_t[c])], 1)], 0)))
        yield
        nm = each(lambda c: _mm(cat([kbar_t[c], bbar_t[c]], 0),
                                cat([cat([v_t[c], jnp.zeros((c_len, LANE), BF16)], 1),
                                     cat([q_t[c], p_t[c]], 1).astype(BF16)], 0), TN))
        n_sbs = each(lambda c: jnp.where(half_k, nm[c][:A_HEAD, :LANE], nm[c][A_HEAD:, :LANE]))
        m_sbs = each(lambda c: jnp.where(half_k, nm[c][:A_HEAD, LANE:], nm[c][A_HEAD:, LANE:]) + eye_kk * w_end[c])
        g_pair = each(lambda c: r_t[c] + hg[c][:, LANE:])
        h_pair = each(lambda c: hg[c][:, :LANE])
        yield
        for n, i in enumerate(chunks):
            out_tiles = []
            for j in range(n_pairs):
                c = n * n_pairs + j
                both = _mm(cat([m_sbs[c], g_pair[c]], 0), bd_k(st[j] if chained else wkv0_ref[i, j]))
                out_tiles.append(both[A_HEAD:] + h_pair[c])
                if chained:
                    st[j] = both[:A_HEAD] + n_sbs[c]
                else:
                    wkv_ref[i, j] = both[:A_HEAD] + n_sbs[c]
            mixed[i] = cat(out_tiles, 1)
            yield

    def post(i):
        o = mixed[i]
        inv_n = 1.0 / A_HEAD
        mean = _head_sum(o, eseg) * inv_n
        cen = o - mean
        yield
        var = _head_sum(cen * cen, eseg) * inv_n
        on = cen * lax.rsqrt(var + LNX_EPS) * lnw_ref[...] + lnb_ref[...]
        yield
        o_ref[0, i * c_len:(i + 1) * c_len, :] = (on + operand("bonus", i)) * operand("g", i)

    groups = [list(range(s, min(s + RWKV_CHUNKS_PER_GROUP, n_sub))) for s in range(0, n_sub, RWKV_CHUNKS_PER_GROUP)]
    for n, grp in enumerate(groups):
        _emit(mix(grp), *([post(i) for i in groups[n - 1]] if n > 0 else []))
    _emit(*[post(i) for i in groups[-1]])
    if chained:
        for j in range(n_pairs):
            st_ref[j] = st[j]

        @pl.when(step == pl.num_programs(1) - 1)
        def _():
            wkv_ref[0] = st_ref[...]


def _pairs_layout(s):
    b, h, nv, nk = s.shape
    return s.reshape(b, h // 2, 2, nv, nk).transpose(0, 1, 4, 2, 3).reshape(b, h // 2, nk, 2 * nv)


def _heads_layout(p):
    b, hp, nk, nv2 = p.shape
    return p.reshape(b, hp, nk, 2, nv2 // 2).transpose(0, 1, 3, 4, 2).reshape(b, 2 * hp, nv2 // 2, nk)


def _rwkv_call(ops, wend, wkv0, c_len, lnw, lnb, eseg):
    b, t, ops_cols = ops.shape
    st0 = _pairs_layout(wkv0)
    n_pairs = st0.shape[1]
    d_a = n_pairs * LANE
    n_chunks = t // c_len
    chained = n_chunks > 1
    if chained:
        n_sub = math.gcd(n_chunks, RWKV_CHUNKS_PER_STEP)
        grid = (b, n_chunks // n_sub)
        state = pl.BlockSpec((1,) + st0.shape[1:], lambda i, j: (i, 0, 0, 0))
    else:
        n_sub = math.gcd(b, RWKV_SEQUENCES_PER_STEP)
        grid = (1, b // n_sub)
        ops, wend = ops.reshape(1, b * t, ops_cols), wend.reshape(1, b, 1, d_a)
        state = pl.BlockSpec((n_sub,) + st0.shape[1:], lambda i, j: (j, 0, 0, 0))
    seq = lambda w: pl.BlockSpec((1, c_len * n_sub, w), lambda i, j: (i, j, 0))
    consts = [lnw, lnb, eseg]
    o, st = pl.pallas_call(
        functools.partial(_rwkv_kernel, c_len, n_sub, chained),
        grid=grid,
        in_specs=[seq(ops_cols), pl.BlockSpec((1, n_sub, 1, d_a), lambda i, j: (i, j, 0, 0)), state]
        + [_resident(w.shape) for w in consts],
        out_specs=[seq(d_a), state],
        out_shape=[jax.ShapeDtypeStruct(ops.shape[:2] + (d_a,), F32), jax.ShapeDtypeStruct(st0.shape, F32)],
        scratch_shapes=[pltpu.VMEM(st0.shape[1:], F32)],
        compiler_params=pltpu.CompilerParams(dimension_semantics=("arbitrary", "arbitrary"),
                                             vmem_limit_bytes=VMEM_LIMIT),
        name="rwkv7_mixer",
    )(ops, wend, st0, *consts)
    return o.reshape(b, t, d_a), _heads_layout(st)


def _gla_kernel(c_len, n_chunks, chained, pb_ref, gla0_ref, gkup_ref, gkb_ref, ng_ref, o_ref, st_ref):
    step = pl.program_id(1)
    n_pairs = st_ref.shape[1]
    dk = st_ref.shape[2] // 2
    dv = st_ref.shape[3]
    assert 2 * dk == LANE and dv == LANE
    n_heads = 2 * n_pairs
    nk = n_heads * dk
    nv = n_heads * dv
    tb = c_len * n_chunks

    if chained:
        @pl.when(step == 0)
        def _():
            st_ref[...] = gla0_ref[...]

    def cat(xs, axis):
        return jnp.concatenate(xs, axis=axis)

    p = pb_ref[0]
    q = p[:, 0:nk] * (dk ** -0.5)
    k = p[:, nk:2 * nk]
    v = p[:, 2 * nk:2 * nk + nv]
    og = p[:, 2 * nk + nv:2 * nk + 2 * nv]
    xgk = p[:, 2 * nk + 2 * nv:2 * nk + 2 * nv + LANE]
    z = _dot(xgk.astype(BF16), gkup_ref[...]) + gkb_ref[...]
    log_a = -_softplus(-z) / B_TAU

    sub = min(SUB, c_len)
    n_sub = c_len // sub
    row, col = _iota2(tb, tb)
    in_chunk = col >= (row & -c_len)
    tri = jnp.where((col <= row) & in_chunk, 1.0, 0.0).astype(BF16)
    b = _dot_ones_lhs(tri, log_a)
    if n_sub > 1:
        tri_start = jnp.where((col < (row & -sub)) & in_chunk, 1.0, 0.0).astype(BF16)
        b_start = _dot_ones_lhs(tri_start, log_a)
    else:
        b_start = jnp.zeros_like(b)
    q_inter = q * jnp.exp(b)
    q_local = q * jnp.exp(b - b_start)
    rows_of = lambda x, i: x[i * c_len:(i + 1) * c_len]
    b_last = [b[(i + 1) * c_len - 1:(i + 1) * c_len, :] for i in range(n_chunks)]
    k_state = cat([rows_of(k, i) * jnp.exp(b_last[i] - rows_of(b, i)) for i in range(n_chunks)], 0)
    row1 = lax.broadcasted_iota(jnp.int32, (c_len, 1), 0)
    k_local = []
    for i in range(n_chunks):
        per_sub = []
        for s in range(n_sub):
            ref = b_start[i * c_len + s * sub:i * c_len + s * sub + 1, :]
            expo = jnp.where(row1 < (s + 1) * sub, ref - rows_of(b, i), -1e30)
            per_sub.append(rows_of(k, i) * jnp.exp(expo))
        k_local.append(per_sub)

    half0 = lax.broadcasted_iota(jnp.int32, (1, LANE), 1) < dk
    rc, cc = _iota2(c_len, c_len)
    causal = rc >= cc
    top_rows = lax.broadcasted_iota(jnp.int32, (LANE, 1), 0) < dk
    log_a_hi, log_a_lo = _split2(log_a)
    ones_cv = jnp.ones((c_len, dv), BF16)
    kt = lambda x, i, j: x[i * c_len:(i + 1) * c_len, j * LANE:(j + 1) * LANE]
    vt = lambda x, i, h: x[i * c_len:(i + 1) * c_len, h * dv:(h + 1) * dv]
    pairs = [(i, j) for i in range(n_chunks) for j in range(n_pairs)]
    heads = [(i, h) for i in range(n_chunks) for h in range(n_heads)]

    def a_rows(i, j, s):
        ql = kt(q_local, i, j)[s * sub:(s + 1) * sub]
        lhs = cat([jnp.where(half0, ql, 0.0), jnp.where(half0, 0.0, ql)], 0)
        return _mm(lhs, kt(k_local[i][s], 0, j), NT)
    a_blk = {(i, j): [a_rows(i, j, s) for s in range(n_sub)] for i, j in pairs}
    a_mat = {}
    for i, h in heads:
        blocks = [a_blk[(i, h // 2)][s][(h % 2) * sub:(h % 2 + 1) * sub] for s in range(n_sub)]
        a_mat[(i, h)] = jnp.where(causal, cat(blocks, 0) if n_sub > 1 else blocks[0], 0.0)
    o_intra = {(i, h): _mm(a_mat[(i, h)], vt(v, i, h)) for i, h in heads}
    upd = {}
    dec = {}
    for i, j in pairs:
        kv = _mm(kt(k_state, i, j), v[i * c_len:(i + 1) * c_len, 2 * j * dv:(2 * j + 2) * dv], TN)
        upd[(i, j)] = jnp.where(top_rows, kv[:, :dv], kv[:, dv:])
        col_sum = lambda piece: lax.dot_general(kt(piece, i, j), ones_cv, (TN, ((), ())), preferred_element_type=F32)
        dec[(i, j)] = jnp.exp(col_sum(log_a_hi) + col_sum(log_a_lo))
    states = {}
    for j in range(n_pairs):
        if chained:
            s_cur = st_ref[0, j]
            for i in range(n_chunks):
                states[(i, j)] = s_cur
                s_cur = dec[(i, j)] * s_cur + upd[(i, j)]
            st_ref[0, j] = s_cur
        else:
            for i in range(n_chunks):
                states[(i, j)] = gla0_ref[i, j]
                st_ref[i, j] = dec[(i, j)] * states[(i, j)] + upd[(i, j)]
    out_rows = []
    for i in range(n_chunks):
        outs = []
        for h in range(n_heads):
            qi = kt(q_inter, i, h // 2)
            qi = jnp.where(half0, qi, 0.0) if h % 2 == 0 else jnp.where(half0, 0.0, qi)
            o_h = o_intra[(i, h)] + _mm(qi, states[(i, h // 2)])
            o_h = o_h * lax.rsqrt(jnp.mean(o_h * o_h, axis=-1, keepdims=True) + EPS) * ng_ref[...]
            og_h = vt(og, i, h)
            outs.append(o_h * (og_h * jax.nn.sigmoid(og_h)))
        out_rows.append(cat(outs, 1))
    o_ref[0] = cat(out_rows, 0) if n_chunks > 1 else out_rows[0]


def _gla_call(pb, gla0, c_len, gkup, gkb, ng):
    b, t, cols = pb.shape
    n_heads, dk, dv = gla0.shape[1:]
    n_chunks = t // c_len
    st_shape = (n_heads // 2, 2 * dk, dv)
    chained = n_chunks > 1
    if chained:
        per_step = math.gcd(n_chunks, GLA_CHUNKS_PER_STEP)
        grid = (b, n_chunks // per_step)
        state = pl.BlockSpec((1,) + st_shape, lambda i, j: (i, 0, 0, 0))
    else:
        per_step = math.gcd(b, GLA_SEQUENCES_PER_STEP)
        grid = (1, b // per_step)
        pb = pb.reshape(1, b * t, cols)
        state = pl.BlockSpec((per_step,) + st_shape, lambda i, j: (j, 0, 0, 0))
    seq = lambda w: pl.BlockSpec((1, c_len * per_step, w), lambda i, j: (i, j, 0))
    consts = [gkup, gkb, ng]
    o, st = pl.pallas_call(
        functools.partial(_gla_kernel, c_len, per_step, chained),
        grid=grid,
        in_specs=[seq(cols), state] + [_resident(w.shape) for w in consts],
        out_specs=[seq(n_heads * dv), state],
        out_shape=[jax.ShapeDtypeStruct(pb.shape[:2] + (n_heads * dv,), F32),
                   jax.ShapeDtypeStruct((b,) + st_shape, F32)],
        compiler_params=pltpu.CompilerParams(dimension_semantics=("arbitrary", "arbitrary"),
                                             vmem_limit_bytes=VMEM_LIMIT),
        name="gla_mixer",
    )(pb, gla0.reshape((b,) + st_shape), *consts)
    return o.reshape(b, t, n_heads * dv), st.reshape(gla0.shape)


def _chunk_len(t):
    return GLA_CHUNK if t % GLA_CHUNK == 0 else t


def _pad_rows(w, start, total):
    return jnp.zeros((total, w.shape[1]), w.dtype).at[start:start + w.shape[0]].set(w)


def _prep_layer(l, ln1_g, ffn1_wg, ffn1_wu, ffn1_wd, ln_mix_g, w_in, mu_shift, w0, w_lora_up, a0, a_lora_up,
                g_lora_up, k_k, k_a, r_k, lnx_w, lnx_b, gk_up, gk_b, gla_norm_g, w_out, ln2_g, ffn2_wg, ffn2_wu,
                ffn2_wd):
    d_a = w0.shape[1]
    a_cols = mu_shift.shape[1]
    nk = gk_b.shape[1]
    d_b = w_out.shape[1] - d_a
    row = lambda x: x[l].reshape(1, -1)
    wi = w_in[l]
    wi_b = wi[:, a_cols:]
    q_k_v = wi_b[:, :2 * nk + d_b]
    xgk_w = wi_b[:, 2 * nk + d_b:2 * nk + d_b + B_GATE_RANK]
    og_w = wi_b[:, 2 * nk + d_b + B_GATE_RANK:]
    pad = jnp.zeros((wi.shape[0], LANE - B_GATE_RANK), wi.dtype)
    wib = jnp.concatenate([q_k_v, og_w, xgk_w, pad], axis=1)
    eseg = jnp.kron(jnp.eye(HEAD_SUM_TILE // A_HEAD, dtype=F32), jnp.ones((A_HEAD, A_HEAD), F32)).astype(BF16)
    return dict(
        pre=(row(ln1_g), ffn1_wg[l].astype(BF16), ffn1_wu[l].astype(BF16), ffn1_wd[l].astype(BF16), row(ln_mix_g),
             wi[:, :a_cols].astype(BF16), wib.astype(BF16),
             row(mu_shift), row(w0), _pad_rows(w_lora_up[l], 0, LANE).astype(BF16), row(a0),
             _pad_rows(a_lora_up[l], A_W_RANK, LANE).astype(BF16),
             _pad_rows(g_lora_up[l], A_W_RANK + A_A_RANK, LANE).astype(BF16), row(k_k), row(k_a), row(r_k), eseg),
        rwkv=(row(lnx_w), row(lnx_b), eseg),
        gla=(_pad_rows(gk_up[l], 0, LANE).astype(BF16), row(gk_b), row(gla_norm_g)),
        post=(w_out[l, :d_a].astype(BF16), w_out[l, d_a:].astype(BF16), row(ln2_g), ffn2_wg[l].astype(BF16),
              ffn2_wu[l].astype(BF16), ffn2_wd[l].astype(BF16)),
    )


def _trunk(x, shift, wkv, gla, layers, ln_f):
    b, t, d = x.shape
    c_len = _chunk_len(t)
    xf = x.reshape(b * t, d)
    new_shift, new_wkv, new_gla = [], [], []
    for l, lw in enumerate(layers):
        x1, pb, ops, wend, last = _pre_call(xf, shift[l], c_len, *lw["pre"])
        oa, s_a = _rwkv_call(ops.reshape(b, t, -1), wend.reshape(b, t // c_len, 1, -1), wkv[l], c_len, *lw["rwkv"])
        ob, s_b = _gla_call(pb.reshape(b, t, -1), gla[l], c_len, *lw["gla"])
        xf = _post_call(x1, oa.reshape(b * t, -1), ob.reshape(b * t, -1), *lw["post"], ln_f,
                        final_norm=(l == len(layers) - 1))
        new_shift.append(last.reshape(b, t // c_len, 1, -1)[:, -1])
        new_wkv.append(s_a)
        new_gla.append(s_b)
    return xf.reshape(b, t, d), jnp.stack(new_shift), jnp.stack(new_wkv), jnp.stack(new_gla)


def kernel(x_prompt, x_sample, state_shift, state_wkv, state_gla, ln1_g, ffn1_wg, ffn1_wu, ffn1_wd, ln_mix_g, w_in,
           mu_shift, w0, w_lora_up, a0, a_lora_up, g_lora_up, k_k, k_a, r_k, lnx_w, lnx_b, gk_up, gk_b, gla_norm_g,
           w_out, ln2_g, ffn2_wg, ffn2_wu, ffn2_wd, ln_f_g):
    depth = ln1_g.shape[0]
    per_layer = (ln1_g, ffn1_wg, ffn1_wu, ffn1_wd, ln_mix_g, w_in, mu_shift, w0, w_lora_up, a0, a_lora_up, g_lora_up,
                 k_k, k_a, r_k.reshape(depth, -1), lnx_w, lnx_b, gk_up, gk_b, gla_norm_g, w_out, ln2_g, ffn2_wg,
                 ffn2_wu, ffn2_wd)
    layers = [_prep_layer(l, *per_layer) for l in range(depth)]
    ln_f = ln_f_g.reshape(1, -1)
    bp = x_prompt.shape[0]
    shift0 = jnp.zeros((depth, bp) + state_shift.shape[2:], F32)
    wkv0 = jnp.zeros((depth, bp) + state_wkv.shape[2:], F32)
    gla0 = jnp.zeros((depth, bp) + state_gla.shape[2:], F32)
    y_p, shift_p, wkv_p, gla_p = _trunk(x_prompt, shift0, wkv0, gla0, layers, ln_f)
    y_s, shift_s, wkv_s, gla_s = _trunk(x_sample, state_shift, state_wkv, state_gla, layers, ln_f)
    return (y_p, y_s, shift_p, wkv_p, gla_p, shift_s, wkv_s, gla_s)
```

```python
import functools
import math

import jax
import jax.numpy as jnp
from jax import lax
from jax.experimental import pallas as pl
from jax.experimental.pallas import tpu as pltpu

F32 = jnp.float32
BF16 = jnp.bfloat16
HIGHEST = lax.Precision.HIGHEST

EPS = 1e-6
LNX_EPS = 64e-5
A_HEAD = 64
A_W_RANK = 32
A_A_RANK = 32
A_G_RANK = 64
B_HEADS = 4
B_GATE_RANK = 16
B_TAU = 16.0
GLA_CHUNK = 64
SUB = 16
LANE = 128
VMEM_LIMIT = 56 * 1024 * 1024
PRE_TILE = 256
POST_TILE = 512
MIXER_CHUNKS_PER_STEP = 8
MIXER_SEQUENCES_PER_STEP = 16
RWKV_CHUNKS_PER_GROUP = 4
HEAD_SUM_TILE = 256

NN = ((1,), (0,))
NT = ((1,), (1,))
TN = ((0,), (0,))


def _dot(a, b, precision=None):
    return jnp.dot(a, b, preferred_element_type=F32, precision=precision)


def _split2(x):
    hi = x.astype(BF16)
    return hi, (x - hi.astype(F32)).astype(BF16)


def _mm(a, b, dims=NN):
    return lax.dot_general(a.astype(BF16), b.astype(BF16), (dims, ((), ())), preferred_element_type=F32)


def _dot_ones_lhs(ones_bf16, x):
    hi, lo = _split2(x)
    return _dot(ones_bf16, hi) + _dot(ones_bf16, lo)


def _softplus(x):
    return jnp.maximum(x, 0.0) + jnp.log(1.0 + jnp.exp(-jnp.abs(x)))


def _rmsnorm(x, g):
    return x * lax.rsqrt(jnp.mean(x * x, axis=-1, keepdims=True) + EPS) * g


def _swiglu_half_step(x, ln_g, wg_ref, wu_ref, wd_ref):
    h = _rmsnorm(x, ln_g).astype(BF16)
    g = _dot(h, wg_ref[...])
    u = _dot(h, wu_ref[...])
    act = (g * jax.nn.sigmoid(g) * u).astype(BF16)
    return x + 0.5 * _dot(act, wd_ref[...])


def _iota2(n, m):
    return lax.broadcasted_iota(jnp.int32, (n, m), 0), lax.broadcasted_iota(jnp.int32, (n, m), 1)


def _cat(xs, axis):
    return jnp.concatenate(xs, axis=axis)


def _emit(*tasks):
    tasks = list(tasks)
    while tasks:
        for task in list(tasks):
            if next(task, True):
                tasks.remove(task)


def _head_sum(x, eseg):
    w = eseg.shape[0]
    return _cat([_dot(x[:, s:s + w].astype(BF16), eseg) for s in range(0, x.shape[1], w)], 1)


OPERANDS = ("a_hat", "k_hat", "b_hat", "kbar", "bbar", "v", "r_til", "bonus", "g")


def _pre_kernel(c_len, n_tiles, x_ref, shift_ref, ln1_ref, wg_ref, wu_ref, wd_ref, lnm_ref, wia_ref, wib_ref, mu_ref,
                w0_ref, wup_ref, a0_ref, aup_ref, gup_ref, kk_ref, ka_ref, rk_ref, eseg_ref,
                x1_ref, pb_ref, ops_ref, wend_ref, last_ref, pa_scr, carry_ref):
    step = pl.program_id(0)
    tm = x_ref.shape[0]
    d_a = w0_ref.shape[1]
    n_chunks = tm // c_len
    eseg = eseg_ref[...]
    row1 = lax.broadcasted_iota(jnp.int32, (c_len, 1), 0)
    rt, ct = _iota2(c_len, c_len)
    tri = jnp.where(ct <= rt, 1.0, 0.0).astype(BF16)

    def front_end(load_p, load_prev, chunk):
        p = load_p()
        shifted = jnp.where(row1 == 0, load_prev(), pltpu.roll(p, 1, axis=0))
        last_ref[chunk] = p[c_len - 1:c_len, :]
        xs = p + (shifted - p) * mu_ref[...]
        r = xs[:, 0:d_a]
        k = xs[:, d_a:2 * d_a]
        v = xs[:, 2 * d_a:3 * d_a]
        lora_in = xs[:, 3 * d_a:3 * d_a + LANE]
        lw = _dot(jnp.tanh(lora_in).astype(BF16), wup_ref[...])
        la = _dot(lora_in.astype(BF16), aup_ref[...])
        g = _dot(jax.nn.sigmoid(lora_in).astype(BF16), gup_ref[...])
        yield
        kk = k * kk_ref[...]
        kk_sq = _head_sum(kk * kk, eseg)
        yield
        logw = (-math.exp(-0.5) * math.log2(math.e)) * jax.nn.sigmoid(w0_ref[...] + lw)
        gc = _dot_ones_lhs(tri, logw)
        yield
        a = jax.nn.sigmoid(a0_ref[...] + la)
        kk = kk * lax.rsqrt(jnp.maximum(kk_sq, 1e-24))
        kmod = k * (1.0 + (a - 1.0) * ka_ref[...])
        kka = kk * a
        bonus = _head_sum(r * kmod * rk_ref[...], eseg) * v
        yield
        gc_last = gc[c_len - 1:c_len, :]
        e_inv = jnp.exp2(-gc)
        e_end = jnp.exp2(gc_last - gc)
        ops = dict(a_hat=-kk * jnp.exp2(gc - logw), k_hat=kmod * e_inv, b_hat=kka * e_inv, kbar=kmod * e_end,
                   bbar=kka * e_end, v=v, r_til=r * jnp.exp2(gc), bonus=bonus, g=g)
        rows = slice(chunk * c_len, (chunk + 1) * c_len)
        for n, name in enumerate(OPERANDS):
            ops_ref[rows, n * d_a:(n + 1) * d_a] = ops[name].astype(BF16)
        wend_ref[chunk] = jnp.exp2(gc_last)

    def ffn(out):
        x = x_ref[...]
        h = _rmsnorm(x, ln1_ref[...]).astype(BF16)
        g = _dot(h, wg_ref[...])
        yield
        u = _dot(h, wu_ref[...])
        yield
        act = (g * jax.nn.sigmoid(g) * u).astype(BF16)
        x1 = x + 0.5 * _dot(act, wd_ref[...])
        x1_ref[...] = x1
        yield
        h = _rmsnorm(x1, lnm_ref[...]).astype(BF16)
        out["pa"] = _dot(h, wia_ref[...])
        yield
        pb_ref[...] = _dot(h, wib_ref[...])

    if n_tiles == 1:
        out = {}
        _emit(ffn(out))
        pa = out["pa"]
        _emit(*[front_end(lambda c=c: pa[c * c_len:(c + 1) * c_len], lambda c=c: shift_ref[c], c)
                for c in range(n_chunks)])
        return

    @pl.when(step == 0)
    def _():
        pa_scr[...] = jnp.zeros(pa_scr.shape, F32)
        carry_ref[...] = shift_ref[0]

    def previous_tile():
        return [front_end(lambda c=c: pa_scr[c * c_len:(c + 1) * c_len, :],
                          lambda c=c: carry_ref[...] if c == 0 else pa_scr[c * c_len - 1:c * c_len, :], c)
                for c in range(n_chunks)]

    @pl.when(step < n_tiles)
    def _():
        out = {}
        _emit(*previous_tile(), ffn(out))
        carry_ref[...] = jnp.where(step > 0, pa_scr[tm - 1:tm, :], carry_ref[...])
        pa_scr[...] = out["pa"]

    @pl.when(step == n_tiles)
    def _():
        _emit(*previous_tile())


def _resident(shape):
    return pl.BlockSpec(shape, lambda *_: (0,) * len(shape), pipeline_mode=pl.Buffered(1))


def _token_tile(n, tile):
    tm = min(tile, n)
    assert n % tm == 0
    return tm


def _pre_call(x, shift, c_len, ln1, wg, wu, wd, lnm, wia, wib, *front_consts):
    n, d = x.shape
    a_cols, b_cols = wia.shape[1], wib.shape[1]
    d_a = front_consts[1].shape[1]
    tm = _token_tile(n, PRE_TILE)
    n_tiles = n // tm
    n_chunks = tm // c_len
    assert (shift.shape[0] == 1) if n_tiles > 1 else (shift.shape[0] == n_chunks)
    lag = 1 if n_tiles > 1 else 0
    cur = lambda w: pl.BlockSpec((tm, w), lambda i: (jnp.minimum(i, n_tiles - 1), 0))
    late = lambda shp: pl.BlockSpec(shp, lambda i: (jnp.maximum(i - lag, 0),) + (0,) * (len(shp) - 1))
    consts = [ln1, wg, wu, wd, lnm, wia, wib, *front_consts]
    return pl.pallas_call(
        functools.partial(_pre_kernel, c_len, n_tiles),
        grid=(n_tiles + lag,),
        in_specs=[cur(d), _resident(shift.shape)] + [_resident(w.shape) for w in consts],
        out_specs=[cur(d), cur(b_cols), late((tm, len(OPERANDS) * d_a)), late((n_chunks, 1, d_a)),
                   late((n_chunks, 1, a_cols))],
        out_shape=[jax.ShapeDtypeStruct((n, d), F32), jax.ShapeDtypeStruct((n, b_cols), F32),
                   jax.ShapeDtypeStruct((n, len(OPERANDS) * d_a), BF16),
                   jax.ShapeDtypeStruct((n // c_len, 1, d_a), F32),
                   jax.ShapeDtypeStruct((n // c_len, 1, a_cols), F32)],
        scratch_shapes=[pltpu.VMEM((tm, a_cols), F32), pltpu.VMEM((1, a_cols), F32)],
        compiler_params=pltpu.CompilerParams(dimension_semantics=("arbitrary",), vmem_limit_bytes=VMEM_LIMIT),
        name="pre_ffn_inproj",
    )(x, shift, *consts)


def _post_kernel(final_norm, x1_ref, o_ref, wo_ref, ln2_ref, wg_ref, wu_ref, wd_ref, lnf_ref, y_ref):
    x2 = x1_ref[...] + _dot(o_ref[...].astype(BF16), wo_ref[...])
    x3 = _swiglu_half_step(x2, ln2_ref[...], wg_ref, wu_ref, wd_ref)
    y_ref[...] = _rmsnorm(x3, lnf_ref[...]) if final_norm else x3


def _post_call(x1, o, wo, ln2, wg, wu, wd, lnf, final_norm):
    n, d = x1.shape
    tm = _token_tile(n, POST_TILE)
    tok = lambda w: pl.BlockSpec((tm, w), lambda i: (i, 0))
    consts = [wo, ln2, wg, wu, wd, lnf]
    return pl.pallas_call(
        functools.partial(_post_kernel, final_norm),
        grid=(n // tm,),
        in_specs=[tok(d), tok(o.shape[1])] + [_resident(w.shape) for w in consts],
        out_specs=tok(d),
        out_shape=jax.ShapeDtypeStruct((n, d), F32),
        compiler_params=pltpu.CompilerParams(dimension_semantics=("arbitrary",), vmem_limit_bytes=VMEM_LIMIT),
        name="post_outproj_ffn",
    )(x1, o, *consts)


def _mixer_kernel(c_len, n_sub, chained, ops_ref, wend_ref, pb_ref, wkv0_ref, gla0_ref, lnw_ref, lnb_ref, eseg_ref,
                  gkup_ref, gkb_ref, ng_ref, o_ref, wkv_ref, gla_ref, st_ref):
    step = pl.program_id(1)
    n_pairs = wkv_ref.shape[1] // 2
    d_a = n_pairs * LANE
    c2 = 2 * c_len
    cat = _cat

    def load_state(i, j):
        both = cat([wkv0_ref[i, 2 * j], wkv0_ref[i, 2 * j + 1]], 0)
        return jnp.transpose(cat([both, jnp.zeros((LANE, LANE - A_HEAD), F32)], 1))[:A_HEAD]

    def store_state(i, j, s):
        t = jnp.transpose(cat([s, jnp.zeros((LANE - A_HEAD, LANE), F32)], 0))
        wkv_ref[i, 2 * j] = t[:A_HEAD, :A_HEAD]
        wkv_ref[i, 2 * j + 1] = t[A_HEAD:, :A_HEAD]

    if chained:
        @pl.when(step == 0)
        def _():
            gla_ref[...] = gla0_ref[...]
            for j in range(n_pairs):
                st_ref[j] = load_state(0, j)

    eseg = eseg_ref[...]

    def operand(name, i, j=None):
        col = OPERANDS.index(name) * d_a
        cols = slice(col, col + d_a) if j is None else slice(col + j * LANE, col + (j + 1) * LANE)
        return ops_ref[0, i * c_len:(i + 1) * c_len, cols]

    rr, cc = _iota2(c_len, c2)
    col_t = cc & (c_len - 1)
    strict = col_t < rr
    lower = col_t <= rr
    eye_tt = jnp.where(col_t == rr, 1.0, 0.0).astype(F32)
    rk, ck = _iota2(A_HEAD, LANE)
    eye_kk = jnp.where((ck & (A_HEAD - 1)) == rk, 1.0, 0.0).astype(F32)
    half_k = lax.broadcasted_iota(jnp.int32, (1, LANE), 1) < A_HEAD
    half_t = lax.broadcasted_iota(jnp.int32, (1, c2), 1) < c_len
    n_double = int(math.log2(c_len))
    assert 2 ** n_double == c_len and n_double >= 2

    def bd(x, half):
        xb = x.astype(BF16)
        return cat([xb * jnp.where(half, 1.0, 0.0).astype(BF16), xb * jnp.where(half, 0.0, 1.0).astype(BF16)], 0)

    bd_k = lambda x: bd(x, half_k)
    bd_t = lambda x: bd(x, half_t)
    st = [st_ref[j] for j in range(n_pairs)] if chained else None
    mixed = {}

    def mix(chunks):
        chains = [(i, j) for i in chunks for j in range(n_pairs)]
        each = lambda f: [f(c) for c in range(len(chains))]
        at = lambda name: [operand(name, i, j) for i, j in chains]
        a_t, r_t, k_t, b_t, v_t = at("a_hat"), at("r_til"), at("k_hat"), at("b_hat"), at("v")
        kbar_t, bbar_t = at("kbar"), at("bbar")
        w_end = [wend_ref[0, i, :, j * LANE:(j + 1) * LANE] for i, j in chains]
        bd_b = each(lambda c: bd_k(b_t[c]))
        bd_kh = each(lambda c: bd_k(k_t[c]))
        xa = each(lambda c: _mm(a_t[c], cat([bd_b[c], bd_kh[c]], 0), NT))
        yield
        xr = each(lambda c: _mm(r_t[c], cat([bd_kh[c], bd_b[c]], 0), NT))
        l_mat = each(lambda c: jnp.where(strict, xa[c][:, :c2], 0.0))
        a_ak = each(lambda c: jnp.where(strict, xa[c][:, c2:], 0.0))
        yield
        t_inv = each(lambda c: eye_tt + l_mat[c])
        pw = each(lambda c: _mm(l_mat[c], bd_t(l_mat[c])))
        a_rk_rb = each(lambda c: cat([jnp.where(lower, xr[c][:, :c2], 0.0), jnp.where(lower, xr[c][:, c2:], 0.0)], 1))
        yield
        for _ in range(1, n_double - 1):
            y = each(lambda c: _mm(pw[c], cat([bd_t(pw[c]), bd_t(t_inv[c])], 1)))
            t_inv = each(lambda c: t_inv[c] + y[c][:, c2:])
            pw = each(lambda c: y[c][:, :c2])
            yield
        t_inv = each(lambda c: t_inv[c] + _mm(pw[c], bd_t(t_inv[c])))
        bd_v = each(lambda c: bd_k(v_t[c]))
        a_ak_v = each(lambda c: _mm(a_ak[c], bd_v[c]))
        yield
        pq = each(lambda c: _mm(t_inv[c], cat([bd_k(a_t[c]), bd_k(a_ak_v[c])], 1)))
        p_t = each(lambda c: pq[c][:, :LANE])
        q_t = each(lambda c: pq[c][:, LANE:])
        yield
        hg = each(lambda c: _mm(a_rk_rb[c], cat([cat([bd_v[c], jnp.zeros((c2, LANE), BF16)], 1),
                                                 cat([bd_k(q_t[c]), bd_k(p_t[c])], 1)], 0)))
        yield
        nm = each(lambda c: _mm(cat([kbar_t[c], bbar_t[c]], 0),
                                cat([cat([v_t[c], jnp.zeros((c_len, LANE), BF16)], 1),
                                     cat([q_t[c], p_t[c]], 1).astype(BF16)], 0), TN))
        n_sbs = each(lambda c: jnp.where(half_k, nm[c][:A_HEAD, :LANE], nm[c][A_HEAD:, :LANE]))
        m_sbs = each(lambda c: jnp.where(half_k, nm[c][:A_HEAD, LANE:], nm[c][A_HEAD:, LANE:]) + eye_kk * w_end[c])
        g_pair = each(lambda c: r_t[c] + hg[c][:, LANE:])
        h_pair = each(lambda c: hg[c][:, :LANE])
        yield
        for n, i in enumerate(chunks):
            out_tiles = []
            for j in range(n_pairs):
                c = n * n_pairs + j
                both = _mm(cat([m_sbs[c], g_pair[c]], 0), bd_k(st[j] if chained else load_state(i, j)))
                out_tiles.append(both[A_HEAD:] + h_pair[c])
                if chained:
                    st[j] = both[:A_HEAD] + n_sbs[c]
                else:
                    store_state(i, j, both[:A_HEAD] + n_sbs[c])
            mixed[i] = cat(out_tiles, 1)
            yield

    def post(i):
        o = mixed[i]
        inv_n = 1.0 / A_HEAD
        mean = _head_sum(o, eseg) * inv_n
        cen = o - mean
        yield
        var = _head_sum(cen * cen, eseg) * inv_n
        on = cen * lax.rsqrt(var + LNX_EPS) * lnw_ref[...] + lnb_ref[...]
        yield
        o_ref[0, i * c_len:(i + 1) * c_len, 0:d_a] = (on + operand("bonus", i)) * operand("g", i)

    groups = [list(range(s, min(s + RWKV_CHUNKS_PER_GROUP, n_sub))) for s in range(0, n_sub, RWKV_CHUNKS_PER_GROUP)]
    for n, grp in enumerate(groups):
        gla = _gla_part(c_len, grp, chained, pb_ref, gla0_ref, gkup_ref, gkb_ref, ng_ref, o_ref, d_a, gla_ref)
        _emit(mix(grp), gla, *([post(i) for i in groups[n - 1]] if n > 0 else []))
    _emit(*[post(i) for i in groups[-1]])
    if chained:
        for j in range(n_pairs):
            st_ref[j] = st[j]

        @pl.when(step == pl.num_programs(1) - 1)
        def _():
            for j in range(n_pairs):
                store_state(0, j, st_ref[j])


def _mixer_call(ops, wend, pb, wkv0, gla0, c_len, lnw, lnb, eseg, gkup, gkb, ng):
    b, t, ops_cols = ops.shape
    d_a = wkv0.shape[1] * A_HEAD
    n_heads_b, dk, dv = gla0.shape[1:]
    gla_shape = (n_heads_b // 2, 2 * dk, dv)
    n_chunks = t // c_len
    chained = n_chunks > 1
    if chained:
        n_sub = math.gcd(n_chunks, MIXER_CHUNKS_PER_STEP)
        grid = (b, n_chunks // n_sub)
        per_seq = lambda shp: pl.BlockSpec((1,) + shp, lambda i, j: (i,) + (0,) * len(shp))
    else:
        n_sub = math.gcd(b, MIXER_SEQUENCES_PER_STEP)
        grid = (1, b // n_sub)
        ops, wend, pb = ops.reshape(1, b * t, ops_cols), wend.reshape(1, b, 1, d_a), pb.reshape(1, b * t, -1)
        per_seq = lambda shp: pl.BlockSpec((n_sub,) + shp, lambda i, j: (j,) + (0,) * len(shp))
    seq = lambda w: pl.BlockSpec((1, c_len * n_sub, w), lambda i, j: (i, j, 0))
    consts = [lnw, lnb, eseg, gkup, gkb, ng]
    d_o = d_a + n_heads_b * dv
    o, wkv, gla = pl.pallas_call(
        functools.partial(_mixer_kernel, c_len, n_sub, chained),
        grid=grid,
        in_specs=[seq(ops_cols), pl.BlockSpec((1, n_sub, 1, d_a), lambda i, j: (i, j, 0, 0)), seq(pb.shape[2]),
                  per_seq(wkv0.shape[1:]), per_seq(gla_shape)] + [_resident(w.shape) for w in consts],
        out_specs=[seq(d_o), per_seq(wkv0.shape[1:]), per_seq(gla_shape)],
        out_shape=[jax.ShapeDtypeStruct(ops.shape[:2] + (d_o,), F32), jax.ShapeDtypeStruct(wkv0.shape, F32),
                   jax.ShapeDtypeStruct((b,) + gla_shape, F32)],
        scratch_shapes=[pltpu.VMEM((d_a // LANE, A_HEAD, LANE), F32)],
        compiler_params=pltpu.CompilerParams(dimension_semantics=("arbitrary", "arbitrary"),
                                             vmem_limit_bytes=VMEM_LIMIT),
        name="mixers",
    )(ops, wend, pb, wkv0, gla0.reshape((b,) + gla_shape), *consts)
    return o.reshape(b, t, d_o), wkv, gla.reshape(gla0.shape)


def _gla_part(c_len, chunks, chained, pb_ref, gla0_ref, gkup_ref, gkb_ref, ng_ref, o_ref, o_col, st_ref):
    n_pairs = st_ref.shape[1]
    dk = st_ref.shape[2] // 2
    dv = st_ref.shape[3]
    assert 2 * dk == LANE and dv == LANE
    n_heads = 2 * n_pairs
    nk = n_heads * dk
    nv = n_heads * dv
    n_chunks = len(chunks)
    tb = c_len * n_chunks
    row0 = chunks[0] * c_len
    cat = _cat

    p = pb_ref[0, row0:row0 + tb, :]
    q = p[:, 0:nk] * (dk ** -0.5)
    k = p[:, nk:2 * nk]
    v = p[:, 2 * nk:2 * nk + nv]
    og = p[:, 2 * nk + nv:2 * nk + 2 * nv]
    xgk = p[:, 2 * nk + 2 * nv:2 * nk + 2 * nv + LANE]
    z = _dot(xgk.astype(BF16), gkup_ref[...]) + gkb_ref[...]
    log_a = -_softplus(-z) / B_TAU

    sub = min(SUB, c_len)
    n_sub = c_len // sub
    row, col = _iota2(tb, tb)
    in_chunk = col >= (row & -c_len)
    tri = jnp.where((col <= row) & in_chunk, 1.0, 0.0).astype(BF16)
    b = _dot_ones_lhs(tri, log_a)
    if n_sub > 1:
        tri_start = jnp.where((col < (row & -sub)) & in_chunk, 1.0, 0.0).astype(BF16)
        b_start = _dot_ones_lhs(tri_start, log_a)
    else:
        b_start = jnp.zeros_like(b)
    yield
    q_inter = q * jnp.exp(b)
    q_local = q * jnp.exp(b - b_start)
    rows_of = lambda x, i: x[i * c_len:(i + 1) * c_len]
    b_last = [b[(i + 1) * c_len - 1:(i + 1) * c_len, :] for i in range(n_chunks)]
    k_state = cat([rows_of(k, i) * jnp.exp(b_last[i] - rows_of(b, i)) for i in range(n_chunks)], 0)
    row1 = lax.broadcasted_iota(jnp.int32, (c_len, 1), 0)
    k_local = []
    for i in range(n_chunks):
        per_sub = []
        for s in range(n_sub):
            ref = b_start[i * c_len + s * sub:i * c_len + s * sub + 1, :]
            expo = jnp.where(row1 < (s + 1) * sub, ref - rows_of(b, i), -1e30)
            per_sub.append(rows_of(k, i) * jnp.exp(expo))
        k_local.append(per_sub)

    half0 = lax.broadcasted_iota(jnp.int32, (1, LANE), 1) < dk
    rc, cc = _iota2(c_len, c_len)
    causal = rc >= cc
    top_rows = lax.broadcasted_iota(jnp.int32, (LANE, 1), 0) < dk
    kt = lambda x, i, j: x[i * c_len:(i + 1) * c_len, j * LANE:(j + 1) * LANE]
    vt = lambda x, i, h: x[i * c_len:(i + 1) * c_len, h * dv:(h + 1) * dv]
    pairs = [(i, j) for i in range(n_chunks) for j in range(n_pairs)]
    heads = [(i, h) for i in range(n_chunks) for h in range(n_heads)]

    yield

    def a_rows(i, j, s):
        ql = kt(q_local, i, j)[s * sub:(s + 1) * sub]
        lhs = cat([jnp.where(half0, ql, 0.0), jnp.where(half0, 0.0, ql)], 0)
        return _mm(lhs, kt(k_local[i][s], 0, j), NT)
    a_blk = {(i, j): [a_rows(i, j, s) for s in range(n_sub)] for i, j in pairs}
    yield
    a_mat = {}
    for i, h in heads:
        blocks = [a_blk[(i, h // 2)][s][(h % 2) * sub:(h % 2 + 1) * sub] for s in range(n_sub)]
        a_mat[(i, h)] = jnp.where(causal, cat(blocks, 0) if n_sub > 1 else blocks[0], 0.0)
    o_intra = {(i, h): _mm(a_mat[(i, h)], vt(v, i, h)) for i, h in heads}
    yield
    upd = {}
    dec = {}
    for i, j in pairs:
        kv = _mm(kt(k_state, i, j), v[i * c_len:(i + 1) * c_len, 2 * j * dv:(2 * j + 2) * dv], TN)
        upd[(i, j)] = jnp.where(top_rows, kv[:, :dv], kv[:, dv:])
        dec[(i, j)] = jnp.exp(jnp.transpose(jnp.broadcast_to(kt(b_last[i], 0, j)[0:1], (LANE, LANE))))
    yield
    states = {}
    for j in range(n_pairs):
        if chained:
            s_cur = st_ref[0, j]
            for i in range(n_chunks):
                states[(i, j)] = s_cur
                s_cur = dec[(i, j)] * s_cur + upd[(i, j)]
            st_ref[0, j] = s_cur
        else:
            for i in range(n_chunks):
                states[(i, j)] = gla0_ref[chunks[i], j]
                st_ref[chunks[i], j] = dec[(i, j)] * states[(i, j)] + upd[(i, j)]
    for i in range(n_chunks):
        outs = []
        for h in range(n_heads):
            qi = kt(q_inter, i, h // 2)
            qi = jnp.where(half0, qi, 0.0) if h % 2 == 0 else jnp.where(half0, 0.0, qi)
            o_h = o_intra[(i, h)] + _mm(qi, states[(i, h // 2)])
            o_h = o_h * lax.rsqrt(jnp.mean(o_h * o_h, axis=-1, keepdims=True) + EPS) * ng_ref[...]
            og_h = vt(og, i, h)
            outs.append(o_h * (og_h * jax.nn.sigmoid(og_h)))
        o_ref[0, row0 + i * c_len:row0 + (i + 1) * c_len, o_col:o_col + nv] = cat(outs, 1)
        yield


def _chunk_len(t):
    return GLA_CHUNK if t % GLA_CHUNK == 0 else t


def _pad_rows(w, start, total):
    return jnp.zeros((total, w.shape[1]), w.dtype).at[start:start + w.shape[0]].set(w)


def _prep_layer(l, ln1_g, ffn1_wg, ffn1_wu, ffn1_wd, ln_mix_g, w_in, mu_shift, w0, w_lora_up, a0, a_lora_up,
                g_lora_up, k_k, k_a, r_k, lnx_w, lnx_b, gk_up, gk_b, gla_norm_g, w_out, ln2_g, ffn2_wg, ffn2_wu,
                ffn2_wd):
    d_a = w0.shape[1]
    a_cols = mu_shift.shape[1]
    nk = gk_b.shape[1]
    d_b = w_out.shape[1] - d_a
    row = lambda x: x[l].reshape(1, -1)
    wi = w_in[l]
    wi_b = wi[:, a_cols:]
    q_k_v = wi_b[:, :2 * nk + d_b]
    xgk_w = wi_b[:, 2 * nk + d_b:2 * nk + d_b + B_GATE_RANK]
    og_w = wi_b[:, 2 * nk + d_b + B_GATE_RANK:]
    pad = jnp.zeros((wi.shape[0], LANE - B_GATE_RANK), wi.dtype)
    wib = jnp.concatenate([q_k_v, og_w, xgk_w, pad], axis=1)
    eseg = jnp.kron(jnp.eye(HEAD_SUM_TILE // A_HEAD, dtype=F32), jnp.ones((A_HEAD, A_HEAD), F32)).astype(BF16)
    return dict(
        pre=(row(ln1_g), ffn1_wg[l].astype(BF16), ffn1_wu[l].astype(BF16), ffn1_wd[l].astype(BF16), row(ln_mix_g),
             wi[:, :a_cols].astype(BF16), wib.astype(BF16),
             row(mu_shift), row(w0), _pad_rows(w_lora_up[l], 0, LANE).astype(BF16), row(a0),
             _pad_rows(a_lora_up[l], A_W_RANK, LANE).astype(BF16),
             _pad_rows(g_lora_up[l], A_W_RANK + A_A_RANK, LANE).astype(BF16), row(k_k), row(k_a), row(r_k), eseg),
        rwkv=(row(lnx_w), row(lnx_b), eseg),
        gla=(_pad_rows(gk_up[l], 0, LANE).astype(BF16), row(gk_b), row(gla_norm_g)),
        post=(w_out[l].astype(BF16), row(ln2_g), ffn2_wg[l].astype(BF16), ffn2_wu[l].astype(BF16),
              ffn2_wd[l].astype(BF16)),
    )


def _trunk(x, shift, wkv, gla, layers, ln_f):
    b, t, d = x.shape
    c_len = _chunk_len(t)
    xf = x.reshape(b * t, d)
    new_shift, new_wkv, new_gla = [], [], []
    for l, lw in enumerate(layers):
        x1, pb, ops, wend, last = _pre_call(xf, shift[l], c_len, *lw["pre"])
        o, s_a, s_b = _mixer_call(ops.reshape(b, t, -1), wend.reshape(b, t // c_len, 1, -1), pb.reshape(b, t, -1),
                                  wkv[l], gla[l], c_len, *lw["rwkv"], *lw["gla"])
        xf = _post_call(x1, o.reshape(b * t, -1), *lw["post"], ln_f, final_norm=(l == len(layers) - 1))
        new_shift.append(last.reshape(b, t // c_len, 1, -1)[:, -1])
        new_wkv.append(s_a)
        new_gla.append(s_b)
    return xf.reshape(b, t, d), jnp.stack(new_shift), jnp.stack(new_wkv), jnp.stack(new_gla)


def kernel(x_prompt, x_sample, state_shift, state_wkv, state_gla, ln1_g, ffn1_wg, ffn1_wu, ffn1_wd, ln_mix_g, w_in,
           mu_shift, w0, w_lora_up, a0, a_lora_up, g_lora_up, k_k, k_a, r_k, lnx_w, lnx_b, gk_up, gk_b, gla_norm_g,
           w_out, ln2_g, ffn2_wg, ffn2_wu, ffn2_wd, ln_f_g):
    depth = ln1_g.shape[0]
    per_layer = (ln1_g, ffn1_wg, ffn1_wu, ffn1_wd, ln_mix_g, w_in, mu_shift, w0, w_lora_up, a0, a_lora_up, g_lora_up,
                 k_k, k_a, r_k.reshape(depth, -1), lnx_w, lnx_b, gk_up, gk_b, gla_norm_g, w_out, ln2_g, ffn2_wg,
                 ffn2_wu, ffn2_wd)
    layers = [_prep_layer(l, *per_layer) for l in range(depth)]
    ln_f = ln_f_g.reshape(1, -1)
    bp = x_prompt.shape[0]
    shift0 = jnp.zeros((depth, bp) + state_shift.shape[2:], F32)
    wkv0 = jnp.zeros((depth, bp) + state_wkv.shape[2:], F32)
    gla0 = jnp.zeros((depth, bp) + state_gla.shape[2:], F32)
    y_p, shift_p, wkv_p, gla_p = _trunk(x_prompt, shift0, wkv0, gla0, layers, ln_f)
    y_s, shift_s, wkv_s, gla_s = _trunk(x_sample, state_shift, state_wkv, state_gla, layers, ln_f)
    return (y_p, y_s, shift_p, wkv_p, gla_p, shift_s, wkv_s, gla_s)
```

```python
import functools
import math

import jax
import jax.numpy as jnp
from jax import lax
from jax.experimental import pallas as pl
from jax.experimental.pallas import tpu as pltpu

F32 = jnp.float32
BF16 = jnp.bfloat16
HIGHEST = lax.Precision.HIGHEST

EPS = 1e-6
LNX_EPS = 64e-5
A_HEAD = 64
A_W_RANK = 32
A_A_RANK = 32
A_G_RANK = 64
B_HEADS = 4
B_GATE_RANK = 16
B_TAU = 16.0
GLA_CHUNK = 64
SUB = 16
LANE = 128
VMEM_LIMIT = 56 * 1024 * 1024
PRE_TILE = 256
POST_TILE = 512
MIXER_CHUNKS_PER_STEP = 8
MIXER_SEQUENCES_PER_STEP = 16
RWKV_CHUNKS_PER_GROUP = 4
WEIGHT_LOAD_CHUNKS = 16
HEAD_SUM_TILE = 256

NN = ((1,), (0,))
NT = ((1,), (1,))
TN = ((0,), (0,))


def _dot(a, b, precision=None):
    return jnp.dot(a, b, preferred_element_type=F32, precision=precision)


def _split2(x):
    hi = x.astype(BF16)
    return hi, (x - hi.astype(F32)).astype(BF16)


def _mm(a, b, dims=NN):
    return lax.dot_general(a.astype(BF16), b.astype(BF16), (dims, ((), ())), preferred_element_type=F32)


def _dot_ones_lhs(ones_bf16, x):
    hi, lo = _split2(x)
    return _dot(ones_bf16, hi) + _dot(ones_bf16, lo)


def _softplus(x):
    return jnp.maximum(x, 0.0) + jnp.log(1.0 + jnp.exp(-jnp.abs(x)))


def _rmsnorm(x, g):
    return x * lax.rsqrt(jnp.mean(x * x, axis=-1, keepdims=True) + EPS) * g


def _load_as_bf16(src_hbm, dst_ref, stage_ref, sem_ref):
    chunk = stage_ref.shape[1]
    n = src_hbm.shape[0] // chunk
    assert n * chunk == src_hbm.shape[0]

    def copy(c, slot):
        return pltpu.make_async_copy(src_hbm.at[pl.ds(c * chunk, chunk)], stage_ref.at[slot], sem_ref.at[slot])

    copy(0, 0).start()

    def body(c, carry):
        slot = c % 2

        @pl.when(c + 1 < n)
        def _():
            copy(c + 1, 1 - slot).start()

        copy(c, slot).wait()
        dst_ref[pl.ds(pl.multiple_of(c * chunk, chunk), chunk), :] = stage_ref[slot].astype(BF16)
        return carry

    lax.fori_loop(0, n, body, 0)


def _load_ffn_weights(wg_hbm, wu_hbm, wd_hbm, wg_ref, wu_ref, wd_ref, stage_in_ref, stage_out_ref, sem_ref):
    _load_as_bf16(wg_hbm, wg_ref, stage_in_ref, sem_ref)
    _load_as_bf16(wu_hbm, wu_ref, stage_in_ref, sem_ref)
    _load_as_bf16(wd_hbm, wd_ref, stage_out_ref, sem_ref)


def _ffn_weight_scratch(wg, wd):
    (d, ff) = wg.shape
    return [pltpu.VMEM((d, ff), BF16), pltpu.VMEM((d, ff), BF16), pltpu.VMEM((ff, d), BF16),
            pltpu.VMEM((2, d // WEIGHT_LOAD_CHUNKS, ff), F32), pltpu.VMEM((2, ff // WEIGHT_LOAD_CHUNKS, d), F32),
            pltpu.SemaphoreType.DMA((2,))]


def _swiglu_half_step(x, ln_g, wg_ref, wu_ref, wd_ref):
    h = _rmsnorm(x, ln_g).astype(BF16)
    g = _dot(h, wg_ref[...])
    u = _dot(h, wu_ref[...])
    act = (g * jax.nn.sigmoid(g) * u).astype(BF16)
    return x + 0.5 * _dot(act, wd_ref[...])


def _iota2(n, m):
    return lax.broadcasted_iota(jnp.int32, (n, m), 0), lax.broadcasted_iota(jnp.int32, (n, m), 1)


def _cat(xs, axis):
    return jnp.concatenate(xs, axis=axis)


def _emit(*tasks):
    tasks = list(tasks)
    while tasks:
        for task in list(tasks):
            if next(task, True):
                tasks.remove(task)


def _head_sum(x, eseg):
    w = eseg.shape[0]
    return _cat([_dot(x[:, s:s + w].astype(BF16), eseg) for s in range(0, x.shape[1], w)], 1)


OPERANDS = ("a_hat", "k_hat", "b_hat", "kbar", "bbar", "v", "r_til", "bonus", "g")


def _pre_kernel(c_len, n_tiles, x_ref, shift_ref, wg_hbm, wu_hbm, wd_hbm, ln1_ref, lnm_ref, wia_ref, wib_ref, mu_ref,
                w0_ref, wup_ref, a0_ref, aup_ref, gup_ref, kk_ref, ka_ref, rk_ref, eseg_ref,
                x1_ref, pb_ref, ops_ref, wend_ref, last_ref, pa_scr, carry_ref,
                wg_ref, wu_ref, wd_ref, stage_in_ref, stage_out_ref, wsem_ref):
    step = pl.program_id(0)
    tm = x_ref.shape[0]
    d_a = w0_ref.shape[1]
    n_chunks = tm // c_len

    @pl.when(step == 0)
    def _():
        _load_ffn_weights(wg_hbm, wu_hbm, wd_hbm, wg_ref, wu_ref, wd_ref, stage_in_ref, stage_out_ref, wsem_ref)

    eseg = eseg_ref[...]
    row1 = lax.broadcasted_iota(jnp.int32, (c_len, 1), 0)
    rt, ct = _iota2(c_len, c_len)
    tri = jnp.where(ct <= rt, 1.0, 0.0).astype(BF16)

    def front_end(load_p, load_prev, chunk):
        p = load_p()
        shifted = jnp.where(row1 == 0, load_prev(), pltpu.roll(p, 1, axis=0))
        last_ref[chunk] = p[c_len - 1:c_len, :]
        xs = p + (shifted - p) * mu_ref[...]
        r = xs[:, 0:d_a]
        k = xs[:, d_a:2 * d_a]
        v = xs[:, 2 * d_a:3 * d_a]
        lora_in = xs[:, 3 * d_a:3 * d_a + LANE]
        lw = _dot(jnp.tanh(lora_in).astype(BF16), wup_ref[...])
        la = _dot(lora_in.astype(BF16), aup_ref[...])
        g = _dot(jax.nn.sigmoid(lora_in).astype(BF16), gup_ref[...])
        yield
        kk = k * kk_ref[...]
        kk_sq = _head_sum(kk * kk, eseg)
        yield
        logw = (-math.exp(-0.5) * math.log2(math.e)) * jax.nn.sigmoid(w0_ref[...] + lw)
        gc = _dot_ones_lhs(tri, logw)
        yield
        a = jax.nn.sigmoid(a0_ref[...] + la)
        kk = kk * lax.rsqrt(jnp.maximum(kk_sq, 1e-24))
        kmod = k * (1.0 + (a - 1.0) * ka_ref[...])
        kka = kk * a
        bonus = _head_sum(r * kmod * rk_ref[...], eseg) * v
        yield
        gc_last = gc[c_len - 1:c_len, :]
        e_inv = jnp.exp2(-gc)
        e_end = jnp.exp2(gc_last - gc)
        ops = dict(a_hat=-kk * jnp.exp2(gc - logw), k_hat=kmod * e_inv, b_hat=kka * e_inv, kbar=kmod * e_end,
                   bbar=kka * e_end, v=v, r_til=r * jnp.exp2(gc), bonus=bonus, g=g)
        rows = slice(chunk * c_len, (chunk + 1) * c_len)
        for n, name in enumerate(OPERANDS):
            ops_ref[rows, n * d_a:(n + 1) * d_a] = ops[name].astype(BF16)
        wend_ref[chunk] = jnp.exp2(gc_last)

    def ffn(out):
        x = x_ref[...]
        h = _rmsnorm(x, ln1_ref[...]).astype(BF16)
        g = _dot(h, wg_ref[...])
        yield
        u = _dot(h, wu_ref[...])
        yield
        act = (g * jax.nn.sigmoid(g) * u).astype(BF16)
        x1 = x + 0.5 * _dot(act, wd_ref[...])
        x1_ref[...] = x1
        yield
        h = _rmsnorm(x1, lnm_ref[...]).astype(BF16)
        out["pa"] = _dot(h, wia_ref[...])
        yield
        pb_ref[...] = _dot(h, wib_ref[...])

    if n_tiles == 1:
        out = {}
        _emit(ffn(out))
        pa = out["pa"]
        _emit(*[front_end(lambda c=c: pa[c * c_len:(c + 1) * c_len], lambda c=c: shift_ref[c], c)
                for c in range(n_chunks)])
        return

    @pl.when(step == 0)
    def _():
        pa_scr[...] = jnp.zeros(pa_scr.shape, F32)
        carry_ref[...] = shift_ref[0]

    def previous_tile():
        return [front_end(lambda c=c: pa_scr[c * c_len:(c + 1) * c_len, :],
                          lambda c=c: carry_ref[...] if c == 0 else pa_scr[c * c_len - 1:c * c_len, :], c)
                for c in range(n_chunks)]

    @pl.when(step < n_tiles)
    def _():
        out = {}
        _emit(*previous_tile(), ffn(out))
        carry_ref[...] = jnp.where(step > 0, pa_scr[tm - 1:tm, :], carry_ref[...])
        pa_scr[...] = out["pa"]

    @pl.when(step == n_tiles)
    def _():
        _emit(*previous_tile())


def _resident(shape):
    return pl.BlockSpec(shape, lambda *_: (0,) * len(shape), pipeline_mode=pl.Buffered(1))


def _token_tile(n, tile):
    tm = min(tile, n)
    assert n % tm == 0
    return tm


def _pre_call(x, shift, c_len, ln1, wg, wu, wd, lnm, wia, wib, *front_consts):
    n, d = x.shape
    a_cols, b_cols = wia.shape[1], wib.shape[1]
    d_a = front_consts[1].shape[1]
    tm = _token_tile(n, PRE_TILE)
    n_tiles = n // tm
    n_chunks = tm // c_len
    assert (shift.shape[0] == 1) if n_tiles > 1 else (shift.shape[0] == n_chunks)
    lag = 1 if n_tiles > 1 else 0
    cur = lambda w: pl.BlockSpec((tm, w), lambda i: (jnp.minimum(i, n_tiles - 1), 0))
    late = lambda shp: pl.BlockSpec(shp, lambda i: (jnp.maximum(i - lag, 0),) + (0,) * (len(shp) - 1))
    consts = [ln1, lnm, wia, wib, *front_consts]
    in_hbm = pl.BlockSpec(memory_space=pl.ANY)
    return pl.pallas_call(
        functools.partial(_pre_kernel, c_len, n_tiles),
        grid=(n_tiles + lag,),
        in_specs=[cur(d), _resident(shift.shape), in_hbm, in_hbm, in_hbm] + [_resident(w.shape) for w in consts],
        out_specs=[cur(d), cur(b_cols), late((tm, len(OPERANDS) * d_a)), late((n_chunks, 1, d_a)),
                   late((n_chunks, 1, a_cols))],
        out_shape=[jax.ShapeDtypeStruct((n, d), F32), jax.ShapeDtypeStruct((n, b_cols), F32),
                   jax.ShapeDtypeStruct((n, len(OPERANDS) * d_a), BF16),
                   jax.ShapeDtypeStruct((n // c_len, 1, d_a), F32),
                   jax.ShapeDtypeStruct((n // c_len, 1, a_cols), F32)],
        scratch_shapes=[pltpu.VMEM((tm, a_cols), F32), pltpu.VMEM((1, a_cols), F32)] + _ffn_weight_scratch(wg, wd),
        compiler_params=pltpu.CompilerParams(dimension_semantics=("arbitrary",), vmem_limit_bytes=VMEM_LIMIT),
        name="pre_ffn_inproj",
    )(x, shift, wg, wu, wd, *consts)


def _post_kernel(final_norm, x1_ref, o_ref, wg_hbm, wu_hbm, wd_hbm, wo_ref, ln2_ref, lnf_ref, y_ref,
                 wg_ref, wu_ref, wd_ref, stage_in_ref, stage_out_ref, wsem_ref):
    @pl.when(pl.program_id(0) == 0)
    def _():
        _load_ffn_weights(wg_hbm, wu_hbm, wd_hbm, wg_ref, wu_ref, wd_ref, stage_in_ref, stage_out_ref, wsem_ref)

    x2 = x1_ref[...] + _dot(o_ref[...].astype(BF16), wo_ref[...])
    x3 = _swiglu_half_step(x2, ln2_ref[...], wg_ref, wu_ref, wd_ref)
    y_ref[...] = _rmsnorm(x3, lnf_ref[...]) if final_norm else x3


def _post_call(x1, o, wg, wu, wd, wo, ln2, lnf, final_norm):
    n, d = x1.shape
    tm = _token_tile(n, POST_TILE)
    tok = lambda w: pl.BlockSpec((tm, w), lambda i: (i, 0))
    consts = [wo, ln2, lnf]
    in_hbm = pl.BlockSpec(memory_space=pl.ANY)
    return pl.pallas_call(
        functools.partial(_post_kernel, final_norm),
        grid=(n // tm,),
        in_specs=[tok(d), tok(o.shape[1]), in_hbm, in_hbm, in_hbm] + [_resident(w.shape) for w in consts],
        out_specs=tok(d),
        out_shape=jax.ShapeDtypeStruct((n, d), F32),
        scratch_shapes=_ffn_weight_scratch(wg, wd),
        compiler_params=pltpu.CompilerParams(dimension_semantics=("arbitrary",), vmem_limit_bytes=VMEM_LIMIT),
        name="post_outproj_ffn",
    )(x1, o, wg, wu, wd, *consts)


def _mixer_kernel(c_len, n_sub, chained, ops_ref, wend_ref, pb_ref, wkv0_ref, gla0_ref, lnw_ref, lnb_ref, eseg_ref,
                  gkup_ref, gkb_ref, ng_ref, o_ref, wkv_ref, gla_ref, st_ref):
    step = pl.program_id(1)
    n_pairs = wkv_ref.shape[1] // 2
    d_a = n_pairs * LANE
    c2 = 2 * c_len
    cat = _cat

    def load_state(i, j):
        both = cat([wkv0_ref[i, 2 * j], wkv0_ref[i, 2 * j + 1]], 0)
        return jnp.transpose(cat([both, jnp.zeros((LANE, LANE - A_HEAD), F32)], 1))[:A_HEAD]

    def store_state(i, j, s):
        t = jnp.transpose(cat([s, jnp.zeros((LANE - A_HEAD, LANE), F32)], 0))
        wkv_ref[i, 2 * j] = t[:A_HEAD, :A_HEAD]
        wkv_ref[i, 2 * j + 1] = t[A_HEAD:, :A_HEAD]

    if chained:
        @pl.when(step == 0)
        def _():
            gla_ref[...] = gla0_ref[...]
            for j in range(n_pairs):
                st_ref[j] = load_state(0, j)

    eseg = eseg_ref[...]

    def operand(name, i, j=None):
        col = OPERANDS.index(name) * d_a
        cols = slice(col, col + d_a) if j is None else slice(col + j * LANE, col + (j + 1) * LANE)
        return ops_ref[0, i * c_len:(i + 1) * c_len, cols]

    rr, cc = _iota2(c_len, c2)
    col_t = cc & (c_len - 1)
    strict = col_t < rr
    lower = col_t <= rr
    eye_tt = jnp.where(col_t == rr, 1.0, 0.0).astype(F32)
    rk, ck = _iota2(A_HEAD, LANE)
    eye_kk = jnp.where((ck & (A_HEAD - 1)) == rk, 1.0, 0.0).astype(F32)
    half_k = lax.broadcasted_iota(jnp.int32, (1, LANE), 1) < A_HEAD
    half_t = lax.broadcasted_iota(jnp.int32, (1, c2), 1) < c_len
    n_double = int(math.log2(c_len))
    assert 2 ** n_double == c_len and n_double >= 2

    def bd(x, half):
        xb = x.astype(BF16)
        return cat([xb * jnp.where(half, 1.0, 0.0).astype(BF16), xb * jnp.where(half, 0.0, 1.0).astype(BF16)], 0)

    bd_k = lambda x: bd(x, half_k)
    bd_t = lambda x: bd(x, half_t)
    st = [st_ref[j] for j in range(n_pairs)] if chained else None
    mixed = {}

    def mix(chunks):
        chains = [(i, j) for i in chunks for j in range(n_pairs)]
        each = lambda f: [f(c) for c in range(len(chains))]
        at = lambda name: [operand(name, i, j) for i, j in chains]
        a_t, r_t, k_t, b_t, v_t = at("a_hat"), at("r_til"), at("k_hat"), at("b_hat"), at("v")
        kbar_t, bbar_t = at("kbar"), at("bbar")
        w_end = [wend_ref[0, i, :, j * LANE:(j + 1) * LANE] for i, j in chains]
        bd_b = each(lambda c: bd_k(b_t[c]))
        bd_kh = each(lambda c: bd_k(k_t[c]))
        xa = each(lambda c: _mm(a_t[c], cat([bd_b[c], bd_kh[c]], 0), NT))
        yield
        xr = each(lambda c: _mm(r_t[c], cat([bd_kh[c], bd_b[c]], 0), NT))
        l_mat = each(lambda c: jnp.where(strict, xa[c][:, :c2], 0.0))
        a_ak = each(lambda c: jnp.where(strict, xa[c][:, c2:], 0.0))
        yield
        t_inv = each(lambda c: eye_tt + l_mat[c])
        pw = each(lambda c: _mm(l_mat[c], bd_t(l_mat[c])))
        a_rk_rb = each(lambda c: cat([jnp.where(lower, xr[c][:, :c2], 0.0), jnp.where(lower, xr[c][:, c2:], 0.0)], 1))
        yield
        for _ in range(1, n_double - 1):
            y = each(lambda c: _mm(pw[c], cat([bd_t(pw[c]), bd_t(t_inv[c])], 1)))
            t_inv = each(lambda c: t_inv[c] + y[c][:, c2:])
            pw = each(lambda c: y[c][:, :c2])
            yield
        t_inv = each(lambda c: t_inv[c] + _mm(pw[c], bd_t(t_inv[c])))
        bd_v = each(lambda c: bd_k(v_t[c]))
        a_ak_v = each(lambda c: _mm(a_ak[c], bd_v[c]))
        yield
        pq = each(lambda c: _mm(t_inv[c], cat([bd_k(a_t[c]), bd_k(a_ak_v[c])], 1)))
        p_t = each(lambda c: pq[c][:, :LANE])
        q_t = each(lambda c: pq[c][:, LANE:])
        yield
        hg = each(lambda c: _mm(a_rk_rb[c], cat([cat([bd_v[c], jnp.zeros((c2, LANE), BF16)], 1),
                                                 cat([bd_k(q_t[c]), bd_k(p_t[c])], 1)], 0)))
        yield
        nm = each(lambda c: _mm(cat([kbar_t[c], bbar_t[c]], 0),
                                cat([cat([v_t[c], jnp.zeros((c_len, LANE), BF16)], 1),
                                     cat([q_t[c], p_t[c]], 1).astype(BF16)], 0), TN))
        n_sbs = each(lambda c: jnp.where(half_k, nm[c][:A_HEAD, :LANE], nm[c][A_HEAD:, :LANE]))
        m_sbs = each(lambda c: jnp.where(half_k, nm[c][:A_HEAD, LANE:], nm[c][A_HEAD:, LANE:]) + eye_kk * w_end[c])
        g_pair = each(lambda c: r_t[c] + hg[c][:, LANE:])
        h_pair = each(lambda c: hg[c][:, :LANE])
        yield
        for n, i in enumerate(chunks):
            out_tiles = []
            for j in range(n_pairs):
                c = n * n_pairs + j
                both = _mm(cat([m_sbs[c], g_pair[c]], 0), bd_k(st[j] if chained else load_state(i, j)))
                out_tiles.append(both[A_HEAD:] + h_pair[c])
                if chained:
                    st[j] = both[:A_HEAD] + n_sbs[c]
                else:
                    store_state(i, j, both[:A_HEAD] + n_sbs[c])
            mixed[i] = cat(out_tiles, 1)
            yield

    def post(i):
        o = mixed[i]
        inv_n = 1.0 / A_HEAD
        mean = _head_sum(o, eseg) * inv_n
        cen = o - mean
        yield
        var = _head_sum(cen * cen, eseg) * inv_n
        on = cen * lax.rsqrt(var + LNX_EPS) * lnw_ref[...] + lnb_ref[...]
        yield
        o_ref[0, i * c_len:(i + 1) * c_len, 0:d_a] = (on + operand("bonus", i)) * operand("g", i)

    groups = [list(range(s, min(s + RWKV_CHUNKS_PER_GROUP, n_sub))) for s in range(0, n_sub, RWKV_CHUNKS_PER_GROUP)]
    for n, grp in enumerate(groups):
        gla = _gla_part(c_len, grp, chained, pb_ref, gla0_ref, gkup_ref, gkb_ref, ng_ref, o_ref, d_a, gla_ref)
        _emit(mix(grp), gla, *([post(i) for i in groups[n - 1]] if n > 0 else []))
    _emit(*[post(i) for i in groups[-1]])
    if chained:
        for j in range(n_pairs):
            st_ref[j] = st[j]

        @pl.when(step == pl.num_programs(1) - 1)
        def _():
            for j in range(n_pairs):
                store_state(0, j, st_ref[j])


def _mixer_call(ops, wend, pb, wkv0, gla0, c_len, lnw, lnb, eseg, gkup, gkb, ng):
    b, t, ops_cols = ops.shape
    d_a = wkv0.shape[1] * A_HEAD
    n_heads_b, dk, dv = gla0.shape[1:]
    gla_shape = (n_heads_b // 2, 2 * dk, dv)
    n_chunks = t // c_len
    chained = n_chunks > 1
    if chained:
        n_sub = math.gcd(n_chunks, MIXER_CHUNKS_PER_STEP)
        grid = (b, n_chunks // n_sub)
        per_seq = lambda shp: pl.BlockSpec((1,) + shp, lambda i, j: (i,) + (0,) * len(shp))
    else:
        n_sub = math.gcd(b, MIXER_SEQUENCES_PER_STEP)
        grid = (1, b // n_sub)
        ops, wend, pb = ops.reshape(1, b * t, ops_cols), wend.reshape(1, b, 1, d_a), pb.reshape(1, b * t, -1)
        per_seq = lambda shp: pl.BlockSpec((n_sub,) + shp, lambda i, j: (j,) + (0,) * len(shp))
    seq = lambda w: pl.BlockSpec((1, c_len * n_sub, w), lambda i, j: (i, j, 0))
    consts = [lnw, lnb, eseg, gkup, gkb, ng]
    d_o = d_a + n_heads_b * dv
    o, wkv, gla = pl.pallas_call(
        functools.partial(_mixer_kernel, c_len, n_sub, chained),
        grid=grid,
        in_specs=[seq(ops_cols), pl.BlockSpec((1, n_sub, 1, d_a), lambda i, j: (i, j, 0, 0)), seq(pb.shape[2]),
                  per_seq(wkv0.shape[1:]), per_seq(gla_shape)] + [_resident(w.shape) for w in consts],
        out_specs=[seq(d_o), per_seq(wkv0.shape[1:]), per_seq(gla_shape)],
        out_shape=[jax.ShapeDtypeStruct(ops.shape[:2] + (d_o,), F32), jax.ShapeDtypeStruct(wkv0.shape, F32),
                   jax.ShapeDtypeStruct((b,) + gla_shape, F32)],
        scratch_shapes=[pltpu.VMEM((d_a // LANE, A_HEAD, LANE), F32)],
        compiler_params=pltpu.CompilerParams(dimension_semantics=("arbitrary", "arbitrary"),
                                             vmem_limit_bytes=VMEM_LIMIT),
        name="mixers",
    )(ops, wend, pb, wkv0, gla0.reshape((b,) + gla_shape), *consts)
    return o.reshape(b, t, d_o), wkv, gla.reshape(gla0.shape)


def _gla_part(c_len, chunks, chained, pb_ref, gla0_ref, gkup_ref, gkb_ref, ng_ref, o_ref, o_col, st_ref):
    n_pairs = st_ref.shape[1]
    dk = st_ref.shape[2] // 2
    dv = st_ref.shape[3]
    assert 2 * dk == LANE and dv == LANE
    n_heads = 2 * n_pairs
    nk = n_heads * dk
    nv = n_heads * dv
    n_chunks = len(chunks)
    tb = c_len * n_chunks
    row0 = chunks[0] * c_len
    cat = _cat

    p = pb_ref[0, row0:row0 + tb, :]
    q = p[:, 0:nk] * (dk ** -0.5)
    k = p[:, nk:2 * nk]
    v = p[:, 2 * nk:2 * nk + nv]
    og = p[:, 2 * nk + nv:2 * nk + 2 * nv]
    xgk = p[:, 2 * nk + 2 * nv:2 * nk + 2 * nv + LANE]
    z = _dot(xgk.astype(BF16), gkup_ref[...]) + gkb_ref[...]
    log_a = -_softplus(-z) / B_TAU

    sub = min(SUB, c_len)
    n_sub = c_len // sub
    row, col = _iota2(tb, tb)
    in_chunk = col >= (row & -c_len)
    tri = jnp.where((col <= row) & in_chunk, 1.0, 0.0).astype(BF16)
    b = _dot_ones_lhs(tri, log_a)
    if n_sub > 1:
        tri_start = jnp.where((col < (row & -sub)) & in_chunk, 1.0, 0.0).astype(BF16)
        b_start = _dot_ones_lhs(tri_start, log_a)
    else:
        b_start = jnp.zeros_like(b)
    yield
    q_inter = q * jnp.exp(b)
    q_local = q * jnp.exp(b - b_start)
    rows_of = lambda x, i: x[i * c_len:(i + 1) * c_len]
    b_last = [b[(i + 1) * c_len - 1:(i + 1) * c_len, :] for i in range(n_chunks)]
    k_state = cat([rows_of(k, i) * jnp.exp(b_last[i] - rows_of(b, i)) for i in range(n_chunks)], 0)
    row1 = lax.broadcasted_iota(jnp.int32, (c_len, 1), 0)
    k_local = []
    for i in range(n_chunks):
        per_sub = []
        for s in range(n_sub):
            ref = b_start[i * c_len + s * sub:i * c_len + s * sub + 1, :]
            expo = jnp.where(row1 < (s + 1) * sub, ref - rows_of(b, i), -1e30)
            per_sub.append(rows_of(k, i) * jnp.exp(expo))
        k_local.append(per_sub)

    half0 = lax.broadcasted_iota(jnp.int32, (1, LANE), 1) < dk
    rc, cc = _iota2(c_len, c_len)
    causal = rc >= cc
    top_rows = lax.broadcasted_iota(jnp.int32, (LANE, 1), 0) < dk
    kt = lambda x, i, j: x[i * c_len:(i + 1) * c_len, j * LANE:(j + 1) * LANE]
    vt = lambda x, i, h: x[i * c_len:(i + 1) * c_len, h * dv:(h + 1) * dv]
    pairs = [(i, j) for i in range(n_chunks) for j in range(n_pairs)]
    heads = [(i, h) for i in range(n_chunks) for h in range(n_heads)]

    yield

    def a_rows(i, j, s):
        ql = kt(q_local, i, j)[s * sub:(s + 1) * sub]
        lhs = cat([jnp.where(half0, ql, 0.0), jnp.where(half0, 0.0, ql)], 0)
        return _mm(lhs, kt(k_local[i][s], 0, j), NT)
    a_blk = {(i, j): [a_rows(i, j, s) for s in range(n_sub)] for i, j in pairs}
    yield
    a_mat = {}
    for i, h in heads:
        blocks = [a_blk[(i, h // 2)][s][(h % 2) * sub:(h % 2 + 1) * sub] for s in range(n_sub)]
        a_mat[(i, h)] = jnp.where(causal, cat(blocks, 0) if n_sub > 1 else blocks[0], 0.0)
    o_intra = {(i, h): _mm(a_mat[(i, h)], vt(v, i, h)) for i, h in heads}
    yield
    upd = {}
    dec = {}
    for i, j in pairs:
        kv = _mm(kt(k_state, i, j), v[i * c_len:(i + 1) * c_len, 2 * j * dv:(2 * j + 2) * dv], TN)
        upd[(i, j)] = jnp.where(top_rows, kv[:, :dv], kv[:, dv:])
        dec[(i, j)] = jnp.exp(jnp.transpose(jnp.broadcast_to(kt(b_last[i], 0, j)[0:1], (LANE, LANE))))
    yield
    states = {}
    for j in range(n_pairs):
        if chained:
            s_cur = st_ref[0, j]
            for i in range(n_chunks):
                states[(i, j)] = s_cur
                s_cur = dec[(i, j)] * s_cur + upd[(i, j)]
            st_ref[0, j] = s_cur
        else:
            for i in range(n_chunks):
                states[(i, j)] = gla0_ref[chunks[i], j]
                st_ref[chunks[i], j] = dec[(i, j)] * states[(i, j)] + upd[(i, j)]
    for i in range(n_chunks):
        outs = []
        for h in range(n_heads):
            qi = kt(q_inter, i, h // 2)
            qi = jnp.where(half0, qi, 0.0) if h % 2 == 0 else jnp.where(half0, 0.0, qi)
            o_h = o_intra[(i, h)] + _mm(qi, states[(i, h // 2)])
            o_h = o_h * lax.rsqrt(jnp.mean(o_h * o_h, axis=-1, keepdims=True) + EPS) * ng_ref[...]
            og_h = vt(og, i, h)
            outs.append(o_h * (og_h * jax.nn.sigmoid(og_h)))
        o_ref[0, row0 + i * c_len:row0 + (i + 1) * c_len, o_col:o_col + nv] = cat(outs, 1)
        yield


def _chunk_len(t):
    return GLA_CHUNK if t % GLA_CHUNK == 0 else t


def _pad_rows(w, start, total):
    return jnp.zeros((total, w.shape[1]), w.dtype).at[start:start + w.shape[0]].set(w)


def _prep_layer(l, ln1_g, ffn1_wg, ffn1_wu, ffn1_wd, ln_mix_g, w_in, mu_shift, w0, w_lora_up, a0, a_lora_up,
                g_lora_up, k_k, k_a, r_k, lnx_w, lnx_b, gk_up, gk_b, gla_norm_g, w_out, ln2_g, ffn2_wg, ffn2_wu,
                ffn2_wd):
    d_a = w0.shape[1]
    a_cols = mu_shift.shape[1]
    nk = gk_b.shape[1]
    d_b = w_out.shape[1] - d_a
    row = lambda x: x[l].reshape(1, -1)
    wi = w_in[l]
    wi_b = wi[:, a_cols:]
    q_k_v = wi_b[:, :2 * nk + d_b]
    xgk_w = wi_b[:, 2 * nk + d_b:2 * nk + d_b + B_GATE_RANK]
    og_w = wi_b[:, 2 * nk + d_b + B_GATE_RANK:]
    pad = jnp.zeros((wi.shape[0], LANE - B_GATE_RANK), wi.dtype)
    wib = jnp.concatenate([q_k_v, og_w, xgk_w, pad], axis=1)
    eseg = jnp.kron(jnp.eye(HEAD_SUM_TILE // A_HEAD, dtype=F32), jnp.ones((A_HEAD, A_HEAD), F32)).astype(BF16)
    return dict(
        pre=(row(ln1_g), ffn1_wg[l], ffn1_wu[l], ffn1_wd[l], row(ln_mix_g),
             wi[:, :a_cols].astype(BF16), wib.astype(BF16),
             row(mu_shift), row(w0), _pad_rows(w_lora_up[l], 0, LANE).astype(BF16), row(a0),
             _pad_rows(a_lora_up[l], A_W_RANK, LANE).astype(BF16),
             _pad_rows(g_lora_up[l], A_W_RANK + A_A_RANK, LANE).astype(BF16), row(k_k), row(k_a), row(r_k), eseg),
        rwkv=(row(lnx_w), row(lnx_b), eseg),
        gla=(_pad_rows(gk_up[l], 0, LANE).astype(BF16), row(gk_b), row(gla_norm_g)),
        post=(ffn2_wg[l], ffn2_wu[l], ffn2_wd[l], w_out[l].astype(BF16), row(ln2_g)),
    )


def _trunk(x, shift, wkv, gla, layers, ln_f):
    b, t, d = x.shape
    c_len = _chunk_len(t)
    xf = x.reshape(b * t, d)
    new_shift, new_wkv, new_gla = [], [], []
    for l, lw in enumerate(layers):
        x1, pb, ops, wend, last = _pre_call(xf, shift[l], c_len, *lw["pre"])
        o, s_a, s_b = _mixer_call(ops.reshape(b, t, -1), wend.reshape(b, t // c_len, 1, -1), pb.reshape(b, t, -1),
                                  wkv[l], gla[l], c_len, *lw["rwkv"], *lw["gla"])
        xf = _post_call(x1, o.reshape(b * t, -1), *lw["post"], ln_f, final_norm=(l == len(layers) - 1))
        new_shift.append(last.reshape(b, t // c_len, 1, -1)[:, -1])
        new_wkv.append(s_a)
        new_gla.append(s_b)
    return xf.reshape(b, t, d), jnp.stack(new_shift), jnp.stack(new_wkv), jnp.stack(new_gla)


def kernel(x_prompt, x_sample, state_shift, state_wkv, state_gla, ln1_g, ffn1_wg, ffn1_wu, ffn1_wd, ln_mix_g, w_in,
           mu_shift, w0, w_lora_up, a0, a_lora_up, g_lora_up, k_k, k_a, r_k, lnx_w, lnx_b, gk_up, gk_b, gla_norm_g,
           w_out, ln2_g, ffn2_wg, ffn2_wu, ffn2_wd, ln_f_g):
    depth = ln1_g.shape[0]
    per_layer = (ln1_g, ffn1_wg, ffn1_wu, ffn1_wd, ln_mix_g, w_in, mu_shift, w0, w_lora_up, a0, a_lora_up, g_lora_up,
                 k_k, k_a, r_k.reshape(depth, -1), lnx_w, lnx_b, gk_up, gk_b, gla_norm_g, w_out, ln2_g, ffn2_wg,
                 ffn2_wu, ffn2_wd)
    layers = [_prep_layer(l, *per_layer) for l in range(depth)]
    ln_f = ln_f_g.reshape(1, -1)
    bp = x_prompt.shape[0]
    shift0 = jnp.zeros((depth, bp) + state_shift.shape[2:], F32)
    wkv0 = jnp.zeros((depth, bp) + state_wkv.shape[2:], F32)
    gla0 = jnp.zeros((depth, bp) + state_gla.shape[2:], F32)
    y_p, shift_p, wkv_p, gla_p = _trunk(x_prompt, shift0, wkv0, gla0, layers, ln_f)
    y_s, shift_s, wkv_s, gla_s = _trunk(x_sample, state_shift, state_wkv, state_gla, layers, ln_f)
    return (y_p, y_s, shift_p, wkv_p, gla_p, shift_s, wkv_s, gla_s)
```

```python
import functools
import math

import jax
import jax.numpy as jnp
from jax import lax
from jax.experimental import pallas as pl
from jax.experimental.pallas import tpu as pltpu

F32 = jnp.float32
BF16 = jnp.bfloat16
HIGHEST = lax.Precision.HIGHEST

EPS = 1e-6
LNX_EPS = 64e-5
A_HEAD = 64
A_W_RANK = 32
A_A_RANK = 32
A_G_RANK = 64
B_HEADS = 4
B_GATE_RANK = 16
B_TAU = 16.0
GLA_CHUNK = 64
SUB = 16
LANE = 128
VMEM_LIMIT = 56 * 1024 * 1024
PRE_TILE = 256
POST_TILE = 512
MIXER_CHUNKS_PER_STEP = 16
MIXER_SEQUENCES_PER_STEP = 16
RWKV_CHUNKS_PER_GROUP = 8
HEAD_SUM_TILE = 256

NN = ((1,), (0,))
NT = ((1,), (1,))
TN = ((0,), (0,))


def _dot(a, b, precision=None):
    return jnp.dot(a, b, preferred_element_type=F32, precision=precision)


def _split2(x):
    hi = x.astype(BF16)
    return hi, (x - hi.astype(F32)).astype(BF16)


def _mm(a, b, dims=NN):
    return lax.dot_general(a.astype(BF16), b.astype(BF16), (dims, ((), ())), preferred_element_type=F32)


def _dot_ones_lhs(ones_bf16, x):
    hi, lo = _split2(x)
    return _dot(ones_bf16, hi) + _dot(ones_bf16, lo)


def _softplus(x):
    return jnp.maximum(x, 0.0) + jnp.log(1.0 + jnp.exp(-jnp.abs(x)))


def _rmsnorm(x, g):
    return x * lax.rsqrt(jnp.mean(x * x, axis=-1, keepdims=True) + EPS) * g


def _swiglu_half_step(x, ln_g, wg_ref, wu_ref, wd_ref):
    h = _rmsnorm(x, ln_g).astype(BF16)
    g = _dot(h, wg_ref[...])
    u = _dot(h, wu_ref[...])
    act = (g * jax.nn.sigmoid(g) * u).astype(BF16)
    return x + 0.5 * _dot(act, wd_ref[...])


def _iota2(n, m):
    return lax.broadcasted_iota(jnp.int32, (n, m), 0), lax.broadcasted_iota(jnp.int32, (n, m), 1)


def _cat(xs, axis):
    return jnp.concatenate(xs, axis=axis)


def _emit(*tasks):
    tasks = list(tasks)
    while tasks:
        for task in list(tasks):
            if next(task, True):
                tasks.remove(task)


def _head_sum(x, eseg):
    w = eseg.shape[0]
    return _cat([_dot(x[:, s:s + w].astype(BF16), eseg) for s in range(0, x.shape[1], w)], 1)


OPERANDS = ("a_hat", "k_hat", "b_hat", "kbar", "bbar", "v", "r_til", "bonus", "g")


def _pre_kernel(c_len, n_tiles, x_ref, shift_ref, ln1_ref, wg_ref, wu_ref, wd_ref, lnm_ref, win_ref, mu_ref,
                w0_ref, wup_ref, a0_ref, aup_ref, gup_ref, kk_ref, ka_ref, rk_ref, eseg_ref,
                x1_ref, pb_ref, ops_ref, wend_ref, last_ref, pa_scr, carry_ref):
    step = pl.program_id(0)
    tm = x_ref.shape[0]
    d_a = w0_ref.shape[1]
    n_chunks = tm // c_len
    eseg = eseg_ref[...]
    row1 = lax.broadcasted_iota(jnp.int32, (c_len, 1), 0)
    rt, ct = _iota2(c_len, c_len)
    tri = jnp.where(ct <= rt, 1.0, 0.0).astype(BF16)

    def front_end(load_p, load_prev, chunk):
        p = load_p()
        shifted = jnp.where(row1 == 0, load_prev(), pltpu.roll(p, 1, axis=0))
        last_ref[chunk] = p[c_len - 1:c_len, :]
        xs = p + (shifted - p) * mu_ref[...]
        r = xs[:, 0:d_a]
        k = xs[:, d_a:2 * d_a]
        v = xs[:, 2 * d_a:3 * d_a]
        lora_in = xs[:, 3 * d_a:3 * d_a + LANE]
        lw = _dot(jnp.tanh(lora_in).astype(BF16), wup_ref[...])
        la = _dot(lora_in.astype(BF16), aup_ref[...])
        g = _dot(jax.nn.sigmoid(lora_in).astype(BF16), gup_ref[...])
        yield
        kk = k * kk_ref[...]
        kk_sq = _head_sum(kk * kk, eseg)
        yield
        logw = (-math.exp(-0.5) * math.log2(math.e)) * jax.nn.sigmoid(w0_ref[...] + lw)
        gc = _dot_ones_lhs(tri, logw)
        yield
        a = jax.nn.sigmoid(a0_ref[...] + la)
        kk = kk * lax.rsqrt(jnp.maximum(kk_sq, 1e-24))
        kmod = k * (1.0 + (a - 1.0) * ka_ref[...])
        kka = kk * a
        bonus = _head_sum(r * kmod * rk_ref[...], eseg) * v
        yield
        gc_last = gc[c_len - 1:c_len, :]
        e_inv = jnp.exp2(-gc)
        e_end = jnp.exp2(gc_last - gc)
        ops = dict(a_hat=-kk * jnp.exp2(gc - logw), k_hat=kmod * e_inv, b_hat=kka * e_inv, kbar=kmod * e_end,
                   bbar=kka * e_end, v=v, r_til=r * jnp.exp2(gc), bonus=bonus, g=g)
        rows = slice(chunk * c_len, (chunk + 1) * c_len)
        for n, name in enumerate(OPERANDS):
            ops_ref[rows, n * d_a:(n + 1) * d_a] = ops[name].astype(BF16)
        wend_ref[chunk] = jnp.exp2(gc_last)

    def ffn(out):
        x = x_ref[...]
        h = _rmsnorm(x, ln1_ref[...]).astype(BF16)
        g = _dot(h, wg_ref[...])
        yield
        u = _dot(h, wu_ref[...])
        yield
        act = (g * jax.nn.sigmoid(g) * u).astype(BF16)
        x1 = x + 0.5 * _dot(act, wd_ref[...])
        x1_ref[...] = x1
        yield
        h = _rmsnorm(x1, lnm_ref[...]).astype(BF16)
        a_cols = mu_ref.shape[1]
        out["pa"] = _dot(h, win_ref[:, :a_cols])
        yield
        pb_ref[...] = _dot(h, win_ref[:, a_cols:])

    if n_tiles == 1:
        out = {}
        _emit(ffn(out))
        pa = out["pa"]
        _emit(*[front_end(lambda c=c: pa[c * c_len:(c + 1) * c_len], lambda c=c: shift_ref[c], c)
                for c in range(n_chunks)])
        return

    @pl.when(step == 0)
    def _():
        pa_scr[...] = jnp.zeros(pa_scr.shape, F32)
        carry_ref[...] = shift_ref[0]

    def previous_tile():
        return [front_end(lambda c=c: pa_scr[c * c_len:(c + 1) * c_len, :],
                          lambda c=c: carry_ref[...] if c == 0 else pa_scr[c * c_len - 1:c * c_len, :], c)
                for c in range(n_chunks)]

    @pl.when(step < n_tiles)
    def _():
        out = {}
        _emit(*previous_tile(), ffn(out))
        carry_ref[...] = jnp.where(step > 0, pa_scr[tm - 1:tm, :], carry_ref[...])
        pa_scr[...] = out["pa"]

    @pl.when(step == n_tiles)
    def _():
        _emit(*previous_tile())


def _resident(shape):
    return pl.BlockSpec(shape, lambda *_: (0,) * len(shape), pipeline_mode=pl.Buffered(1))


def _token_tile(n, tile):
    tm = min(tile, n)
    assert n % tm == 0
    return tm


def _pre_call(x, shift, c_len, ln1, wg, wu, wd, lnm, win, *front_consts):
    n, d = x.shape
    a_cols = front_consts[0].shape[1]
    b_cols = win.shape[1] - a_cols
    d_a = front_consts[1].shape[1]
    tm = _token_tile(n, PRE_TILE)
    n_tiles = n // tm
    n_chunks = tm // c_len
    assert (shift.shape[0] == 1) if n_tiles > 1 else (shift.shape[0] == n_chunks)
    lag = 1 if n_tiles > 1 else 0
    cur = lambda w: pl.BlockSpec((tm, w), lambda i: (jnp.minimum(i, n_tiles - 1), 0))
    late = lambda shp: pl.BlockSpec(shp, lambda i: (jnp.maximum(i - lag, 0),) + (0,) * (len(shp) - 1))
    consts = [ln1, wg, wu, wd, lnm, win, *front_consts]
    return pl.pallas_call(
        functools.partial(_pre_kernel, c_len, n_tiles),
        grid=(n_tiles + lag,),
        in_specs=[cur(d), _resident(shift.shape)] + [_resident(w.shape) for w in consts],
        out_specs=[cur(d), cur(b_cols), late((tm, len(OPERANDS) * d_a)), late((n_chunks, 1, d_a)),
                   late((n_chunks, 1, a_cols))],
        out_shape=[jax.ShapeDtypeStruct((n, d), F32), jax.ShapeDtypeStruct((n, b_cols), F32),
                   jax.ShapeDtypeStruct((n, len(OPERANDS) * d_a), BF16),
                   jax.ShapeDtypeStruct((n // c_len, 1, d_a), F32),
                   jax.ShapeDtypeStruct((n // c_len, 1, a_cols), F32)],
        scratch_shapes=[pltpu.VMEM((tm, a_cols), F32), pltpu.VMEM((1, a_cols), F32)],
        compiler_params=pltpu.CompilerParams(dimension_semantics=("arbitrary",), vmem_limit_bytes=VMEM_LIMIT),
        name="pre_ffn_inproj",
    )(x, shift, *consts)


def _post_kernel(final_norm, x1_ref, o_ref, wo_ref, ln2_ref, wg_ref, wu_ref, wd_ref, lnf_ref, y_ref):
    x2 = x1_ref[...] + _dot(o_ref[...].astype(BF16), wo_ref[...])
    x3 = _swiglu_half_step(x2, ln2_ref[...], wg_ref, wu_ref, wd_ref)
    y_ref[...] = _rmsnorm(x3, lnf_ref[...]) if final_norm else x3


def _post_call(x1, o, wo, ln2, wg, wu, wd, lnf, final_norm):
    n, d = x1.shape
    tm = _token_tile(n, POST_TILE)
    tok = lambda w: pl.BlockSpec((tm, w), lambda i: (i, 0))
    consts = [wo, ln2, wg, wu, wd, lnf]
    return pl.pallas_call(
        functools.partial(_post_kernel, final_norm),
        grid=(n // tm,),
        in_specs=[tok(d), tok(o.shape[1])] + [_resident(w.shape) for w in consts],
        out_specs=tok(d),
        out_shape=jax.ShapeDtypeStruct((n, d), F32),
        compiler_params=pltpu.CompilerParams(dimension_semantics=("arbitrary",), vmem_limit_bytes=VMEM_LIMIT),
        name="post_outproj_ffn",
    )(x1, o, *consts)


def _mixer_kernel(c_len, n_sub, chained, ops_ref, wend_ref, pb_ref, wkv0_ref, gla0_ref, lnw_ref, lnb_ref, eseg_ref,
                  gkup_ref, gkb_ref, ng_ref, o_ref, wkv_ref, gla_ref, st_ref):
    step = pl.program_id(1)
    n_pairs = wkv_ref.shape[1] // 2
    d_a = n_pairs * LANE
    c2 = 2 * c_len
    cat = _cat

    def load_state(i, j):
        both = cat([wkv0_ref[i, 2 * j], wkv0_ref[i, 2 * j + 1]], 0)
        return jnp.transpose(cat([both, jnp.zeros((LANE, LANE - A_HEAD), F32)], 1))[:A_HEAD]

    def store_state(i, j, s):
        t = jnp.transpose(cat([s, jnp.zeros((LANE - A_HEAD, LANE), F32)], 0))
        wkv_ref[i, 2 * j] = t[:A_HEAD, :A_HEAD]
        wkv_ref[i, 2 * j + 1] = t[A_HEAD:, :A_HEAD]

    if chained:
        @pl.when(step == 0)
        def _():
            gla_ref[...] = gla0_ref[...]
            for j in range(n_pairs):
                st_ref[j] = load_state(0, j)

    eseg = eseg_ref[...]

    def operand(name, i, j=None):
        col = OPERANDS.index(name) * d_a
        cols = slice(col, col + d_a) if j is None else slice(col + j * LANE, col + (j + 1) * LANE)
        return ops_ref[0, i * c_len:(i + 1) * c_len, cols]

    rr, cc = _iota2(c_len, c2)
    col_t = cc & (c_len - 1)
    strict = col_t < rr
    lower = col_t <= rr
    eye_tt = jnp.where(col_t == rr, 1.0, 0.0).astype(F32)
    rk, ck = _iota2(A_HEAD, LANE)
    eye_kk = jnp.where((ck & (A_HEAD - 1)) == rk, 1.0, 0.0).astype(F32)
    half_k = lax.broadcasted_iota(jnp.int32, (1, LANE), 1) < A_HEAD
    half_t = lax.broadcasted_iota(jnp.int32, (1, c2), 1) < c_len
    n_double = int(math.log2(c_len))
    assert 2 ** n_double == c_len and n_double >= 2

    def bd(x, half):
        xb = x.astype(BF16)
        return cat([xb * jnp.where(half, 1.0, 0.0).astype(BF16), xb * jnp.where(half, 0.0, 1.0).astype(BF16)], 0)

    bd_k = lambda x: bd(x, half_k)
    bd_t = lambda x: bd(x, half_t)
    rq, cq = _iota2(LANE, LANE)
    same_head = jnp.where((rq < A_HEAD) == (cq < A_HEAD), 1.0, 0.0).astype(BF16)

    def bd_kt(x):
        return jnp.transpose(cat([x, x], 0)) * same_head
    st = [st_ref[j] for j in range(n_pairs)] if chained else None
    mixed = {}

    def mix(chunks):
        chains = [(i, j) for i in chunks for j in range(n_pairs)]
        each = lambda f: [f(c) for c in range(len(chains))]
        at = lambda name: [operand(name, i, j) for i, j in chains]
        a_t, r_t, k_t, b_t, v_t = at("a_hat"), at("r_til"), at("k_hat"), at("b_hat"), at("v")
        kbar_t, bbar_t = at("kbar"), at("bbar")
        w_end = [wend_ref[0, i, :, j * LANE:(j + 1) * LANE] for i, j in chains]
        if c2 == LANE:
            bd_b = each(lambda c: bd_kt(b_t[c]))
            bd_kh = each(lambda c: bd_kt(k_t[c]))
            xa = each(lambda c: _mm(a_t[c], cat([bd_b[c], bd_kh[c]], 1)))
            yield
            xr = each(lambda c: _mm(r_t[c], cat([bd_kh[c], bd_b[c]], 1)))
        else:
            bd_b = each(lambda c: bd_k(b_t[c]))
            bd_kh = each(lambda c: bd_k(k_t[c]))
            xa = each(lambda c: _mm(a_t[c], cat([bd_b[c], bd_kh[c]], 0), NT))
            yield
            xr = each(lambda c: _mm(r_t[c], cat([bd_kh[c], bd_b[c]], 0), NT))
        l_mat = each(lambda c: jnp.where(strict, xa[c][:, :c2], 0.0))
        a_ak = each(lambda c: jnp.where(strict, xa[c][:, c2:], 0.0))
        yield
        t_inv = each(lambda c: eye_tt + l_mat[c])
        pw = each(lambda c: _mm(l_mat[c], bd_t(l_mat[c])))
        a_rk_rb = each(lambda c: cat([jnp.where(lower, xr[c][:, :c2], 0.0), jnp.where(lower, xr[c][:, c2:], 0.0)], 1))
        yield
        for _ in range(1, n_double - 1):
            y = each(lambda c: _mm(pw[c], cat([bd_t(pw[c]), bd_t(t_inv[c])], 1)))
            t_inv = each(lambda c: t_inv[c] + y[c][:, c2:])
            pw = each(lambda c: y[c][:, :c2])
            yield
        t_inv = each(lambda c: t_inv[c] + _mm(pw[c], bd_t(t_inv[c])))
        bd_v = each(lambda c: bd_k(v_t[c]))
        a_ak_v = each(lambda c: _mm(a_ak[c], bd_v[c]))
        yield
        pq = each(lambda c: _mm(t_inv[c], cat([bd_k(a_t[c]), bd_k(a_ak_v[c])], 1)))
        p_t = each(lambda c: pq[c][:, :LANE])
        q_t = each(lambda c: pq[c][:, LANE:])
        yield
        hg = each(lambda c: _mm(a_rk_rb[c], cat([cat([bd_v[c], jnp.zeros((c2, LANE), BF16)], 1),
                                                 cat([bd_k(q_t[c]), bd_k(p_t[c])], 1)], 0)))
        yield
        nm = each(lambda c: _mm(cat([kbar_t[c], bbar_t[c]], 0),
                                cat([cat([v_t[c], jnp.zeros((c_len, LANE), BF16)], 1),
                                     cat([q_t[c], p_t[c]], 1).astype(BF16)], 0), TN))
        n_sbs = each(lambda c: jnp.where(half_k, nm[c][:A_HEAD, :LANE], nm[c][A_HEAD:, :LANE]))
        m_sbs = each(lambda c: jnp.where(half_k, nm[c][:A_HEAD, LANE:], nm[c][A_HEAD:, LANE:]) + eye_kk * w_end[c])
        g_pair = each(lambda c: r_t[c] + hg[c][:, LANE:])
        h_pair = each(lambda c: hg[c][:, :LANE])
        yield
        for n, i in enumerate(chunks):
            out_tiles = []
            for j in range(n_pairs):
                c = n * n_pairs + j
                both = _mm(cat([m_sbs[c], g_pair[c]], 0), bd_k(st[j] if chained else load_state(i, j)))
                out_tiles.append(both[A_HEAD:] + h_pair[c])
                if chained:
                    st[j] = both[:A_HEAD] + n_sbs[c]
                else:
                    store_state(i, j, both[:A_HEAD] + n_sbs[c])
            mixed[i] = cat(out_tiles, 1)
            yield

    def post(i):
        o = mixed[i]
        inv_n = 1.0 / A_HEAD
        mean = _head_sum(o, eseg) * inv_n
        cen = o - mean
        yield
        var = _head_sum(cen * cen, eseg) * inv_n
        on = cen * lax.rsqrt(var + LNX_EPS) * lnw_ref[...] + lnb_ref[...]
        yield
        o_ref[0, i * c_len:(i + 1) * c_len, 0:d_a] = (on + operand("bonus", i)) * operand("g", i)

    groups = [list(range(s, min(s + RWKV_CHUNKS_PER_GROUP, n_sub))) for s in range(0, n_sub, RWKV_CHUNKS_PER_GROUP)]
    for n, grp in enumerate(groups):
        gla = _gla_part(c_len, grp, chained, pb_ref, gla0_ref, gkup_ref, gkb_ref, ng_ref, o_ref, d_a, gla_ref)
        _emit(mix(grp), gla, *([post(i) for i in groups[n - 1]] if n > 0 else []))
    _emit(*[post(i) for i in groups[-1]])
    if chained:
        for j in range(n_pairs):
            st_ref[j] = st[j]

        @pl.when(step == pl.num_programs(1) - 1)
        def _():
            for j in range(n_pairs):
                store_state(0, j, st_ref[j])


def _mixer_call(ops, wend, pb, wkv0, gla0, c_len, lnw, lnb, eseg, gkup, gkb, ng):
    b, t, ops_cols = ops.shape
    d_a = wkv0.shape[1] * A_HEAD
    n_heads_b, dk, dv = gla0.shape[1:]
    gla_shape = (n_heads_b // 2, 2 * dk, dv)
    n_chunks = t // c_len
    chained = n_chunks > 1
    if chained:
        n_sub = math.gcd(n_chunks, MIXER_CHUNKS_PER_STEP)
        grid = (b, n_chunks // n_sub)
        per_seq = lambda shp: pl.BlockSpec((1,) + shp, lambda i, j: (i,) + (0,) * len(shp))
    else:
        n_sub = math.gcd(b, MIXER_SEQUENCES_PER_STEP)
        grid = (1, b // n_sub)
        ops, wend, pb = ops.reshape(1, b * t, ops_cols), wend.reshape(1, b, 1, d_a), pb.reshape(1, b * t, -1)
        per_seq = lambda shp: pl.BlockSpec((n_sub,) + shp, lambda i, j: (j,) + (0,) * len(shp))
    seq = lambda w: pl.BlockSpec((1, c_len * n_sub, w), lambda i, j: (i, j, 0))
    consts = [lnw, lnb, eseg, gkup, gkb, ng]
    d_o = d_a + n_heads_b * dv
    o, wkv, gla = pl.pallas_call(
        functools.partial(_mixer_kernel, c_len, n_sub, chained),
        grid=grid,
        in_specs=[seq(ops_cols), pl.BlockSpec((1, n_sub, 1, d_a), lambda i, j: (i, j, 0, 0)), seq(pb.shape[2]),
                  per_seq(wkv0.shape[1:]), per_seq(gla_shape)] + [_resident(w.shape) for w in consts],
        out_specs=[seq(d_o), per_seq(wkv0.shape[1:]), per_seq(gla_shape)],
        out_shape=[jax.ShapeDtypeStruct(ops.shape[:2] + (d_o,), F32), jax.ShapeDtypeStruct(wkv0.shape, F32),
                   jax.ShapeDtypeStruct((b,) + gla_shape, F32)],
        scratch_shapes=[pltpu.VMEM((d_a // LANE, A_HEAD, LANE), F32)],
        compiler_params=pltpu.CompilerParams(dimension_semantics=("arbitrary", "arbitrary"),
                                             vmem_limit_bytes=VMEM_LIMIT),
        name="mixers",
    )(ops, wend, pb, wkv0, gla0.reshape((b,) + gla_shape), *consts)
    return o.reshape(b, t, d_o), wkv, gla.reshape(gla0.shape)


def _gla_part(c_len, chunks, chained, pb_ref, gla0_ref, gkup_ref, gkb_ref, ng_ref, o_ref, o_col, st_ref):
    n_pairs = st_ref.shape[1]
    dk = st_ref.shape[2] // 2
    dv = st_ref.shape[3]
    assert 2 * dk == LANE and dv == LANE
    n_heads = 2 * n_pairs
    nk = n_heads * dk
    nv = n_heads * dv
    n_chunks = len(chunks)
    tb = c_len * n_chunks
    row0 = chunks[0] * c_len
    cat = _cat

    p = pb_ref[0, row0:row0 + tb, :]
    q = p[:, 0:nk] * (dk ** -0.5)
    k = p[:, nk:2 * nk]
    v = p[:, 2 * nk:2 * nk + nv]
    og = p[:, 2 * nk + nv:2 * nk + 2 * nv]
    xgk = p[:, 2 * nk + 2 * nv:2 * nk + 2 * nv + LANE]
    z = _dot(xgk.astype(BF16), gkup_ref[...]) + gkb_ref[...]
    log_a = -_softplus(-z) / B_TAU

    sub = min(SUB, c_len)
    n_sub = c_len // sub
    row, col = _iota2(tb, tb)
    in_chunk = col >= (row & -c_len)
    tri = jnp.where((col <= row) & in_chunk, 1.0, 0.0).astype(BF16)
    b = _dot_ones_lhs(tri, log_a)
    if n_sub > 1:
        tri_start = jnp.where((col < (row & -sub)) & in_chunk, 1.0, 0.0).astype(BF16)
        b_start = _dot_ones_lhs(tri_start, log_a)
    else:
        b_start = jnp.zeros_like(b)
    yield
    q_inter = q * jnp.exp(b)
    q_local = q * jnp.exp(b - b_start)
    rows_of = lambda x, i: x[i * c_len:(i + 1) * c_len]
    b_last = [b[(i + 1) * c_len - 1:(i + 1) * c_len, :] for i in range(n_chunks)]
    k_state = cat([rows_of(k, i) * jnp.exp(b_last[i] - rows_of(b, i)) for i in range(n_chunks)], 0)
    row1 = lax.broadcasted_iota(jnp.int32, (c_len, 1), 0)
    k_local = []
    for i in range(n_chunks):
        per_sub = []
        for s in range(n_sub):
            ref = b_start[i * c_len + s * sub:i * c_len + s * sub + 1, :]
            expo = jnp.where(row1 < (s + 1) * sub, ref - rows_of(b, i), -1e30)
            per_sub.append(rows_of(k, i) * jnp.exp(expo))
        k_local.append(per_sub)

    half0 = lax.broadcasted_iota(jnp.int32, (1, LANE), 1) < dk
    rc, cc = _iota2(c_len, c_len)
    causal = rc >= cc
    top_rows = lax.broadcasted_iota(jnp.int32, (LANE, 1), 0) < dk
    kt = lambda x, i, j: x[i * c_len:(i + 1) * c_len, j * LANE:(j + 1) * LANE]
    vt = lambda x, i, h: x[i * c_len:(i + 1) * c_len, h * dv:(h + 1) * dv]
    pairs = [(i, j) for i in range(n_chunks) for j in range(n_pairs)]
    heads = [(i, h) for i in range(n_chunks) for h in range(n_heads)]

    yield

    def a_rows(i, j, s):
        ql = kt(q_local, i, j)[s * sub:(s + 1) * sub]
        lhs = cat([jnp.where(half0, ql, 0.0), jnp.where(half0, 0.0, ql)], 0)
        return _mm(lhs, kt(k_local[i][s], 0, j), NT)
    a_blk = {(i, j): [a_rows(i, j, s) for s in range(n_sub)] for i, j in pairs}
    yield
    a_mat = {}
    for i, h in heads:
        blocks = [a_blk[(i, h // 2)][s][(h % 2) * sub:(h % 2 + 1) * sub] for s in range(n_sub)]
        a_mat[(i, h)] = jnp.where(causal, cat(blocks, 0) if n_sub > 1 else blocks[0], 0.0)
    o_intra = {(i, h): _mm(a_mat[(i, h)], vt(v, i, h)) for i, h in heads}
    yield
    upd = {}
    dec = {}
    for i, j in pairs:
        kv = _mm(kt(k_state, i, j), v[i * c_len:(i + 1) * c_len, 2 * j * dv:(2 * j + 2) * dv], TN)
        upd[(i, j)] = jnp.where(top_rows, kv[:, :dv], kv[:, dv:])
        dec[(i, j)] = jnp.exp(jnp.transpose(jnp.broadcast_to(kt(b_last[i], 0, j)[0:1], (LANE, LANE))))
    yield
    states = {}
    for j in range(n_pairs):
        if chained:
            s_cur = st_ref[0, j]
            for i in range(n_chunks):
                states[(i, j)] = s_cur
                s_cur = dec[(i, j)] * s_cur + upd[(i, j)]
            st_ref[0, j] = s_cur
        else:
            for i in range(n_chunks):
                states[(i, j)] = gla0_ref[chunks[i], j]
                st_ref[chunks[i], j] = dec[(i, j)] * states[(i, j)] + upd[(i, j)]
    for i in range(n_chunks):
        outs = []
        for h in range(n_heads):
            qi = kt(q_inter, i, h // 2)
            qi = jnp.where(half0, qi, 0.0) if h % 2 == 0 else jnp.where(half0, 0.0, qi)
            o_h = o_intra[(i, h)] + _mm(qi, states[(i, h // 2)])
            o_h = o_h * lax.rsqrt(jnp.mean(o_h * o_h, axis=-1, keepdims=True) + EPS) * ng_ref[...]
            og_h = vt(og, i, h)
            outs.append(o_h * (og_h * jax.nn.sigmoid(og_h)))
        o_ref[0, row0 + i * c_len:row0 + (i + 1) * c_len, o_col:o_col + nv] = cat(outs, 1)
        yield


def _chunk_len(t):
    return GLA_CHUNK if t % GLA_CHUNK == 0 else t


def _pad_rows(w, start, total):
    return jnp.zeros((total, w.shape[1]), w.dtype).at[start:start + w.shape[0]].set(w)


def _prep_layer(l, ln1_g, ffn1_wg, ffn1_wu, ffn1_wd, ln_mix_g, w_in, mu_shift, w0, w_lora_up, a0, a_lora_up,
                g_lora_up, k_k, k_a, r_k, lnx_w, lnx_b, gk_up, gk_b, gla_norm_g, w_out, ln2_g, ffn2_wg, ffn2_wu,
                ffn2_wd):
    d_a = w0.shape[1]
    a_cols = mu_shift.shape[1]
    nk = gk_b.shape[1]
    d_b = w_out.shape[1] - d_a
    row = lambda x: x[l].reshape(1, -1)
    wi = w_in[l]
    n_qkv = a_cols + 2 * nk + d_b
    pad = jnp.zeros((wi.shape[0], LANE - B_GATE_RANK), wi.dtype)
    win = jnp.concatenate([wi[:, :n_qkv], wi[:, n_qkv + B_GATE_RANK:], wi[:, n_qkv:n_qkv + B_GATE_RANK], pad],
                          axis=1).astype(BF16)
    eseg = jnp.kron(jnp.eye(HEAD_SUM_TILE // A_HEAD, dtype=F32), jnp.ones((A_HEAD, A_HEAD), F32)).astype(BF16)
    return dict(
        pre=(row(ln1_g), ffn1_wg[l].astype(BF16), ffn1_wu[l].astype(BF16), ffn1_wd[l].astype(BF16), row(ln_mix_g),
             win, row(mu_shift), row(w0), _pad_rows(w_lora_up[l], 0, LANE).astype(BF16), row(a0),
             _pad_rows(a_lora_up[l], A_W_RANK, LANE).astype(BF16),
             _pad_rows(g_lora_up[l], A_W_RANK + A_A_RANK, LANE).astype(BF16), row(k_k), row(k_a), row(r_k), eseg),
        rwkv=(row(lnx_w), row(lnx_b), eseg),
        gla=(_pad_rows(gk_up[l], 0, LANE).astype(BF16), row(gk_b), row(gla_norm_g)),
        post=(w_out[l].astype(BF16), row(ln2_g), ffn2_wg[l].astype(BF16), ffn2_wu[l].astype(BF16),
              ffn2_wd[l].astype(BF16)),
    )


def _trunk(x, shift, wkv, gla, layers, ln_f):
    b, t, d = x.shape
    c_len = _chunk_len(t)
    xf = x.reshape(b * t, d)
    new_shift, new_wkv, new_gla = [], [], []
    for l, lw in enumerate(layers):
        x1, pb, ops, wend, last = _pre_call(xf, shift[l], c_len, *lw["pre"])
        o, s_a, s_b = _mixer_call(ops.reshape(b, t, -1), wend.reshape(b, t // c_len, 1, -1), pb.reshape(b, t, -1),
                                  wkv[l], gla[l], c_len, *lw["rwkv"], *lw["gla"])
        xf = _post_call(x1, o.reshape(b * t, -1), *lw["post"], ln_f, final_norm=(l == len(layers) - 1))
        new_shift.append(last.reshape(b, t // c_len, 1, -1)[:, -1])
        new_wkv.append(s_a)
        new_gla.append(s_b)
    return xf.reshape(b, t, d), jnp.stack(new_shift), jnp.stack(new_wkv), jnp.stack(new_gla)


def kernel(x_prompt, x_sample, state_shift, state_wkv, state_gla, ln1_g, ffn1_wg, ffn1_wu, ffn1_wd, ln_mix_g, w_in,
           mu_shift, w0, w_lora_up, a0, a_lora_up, g_lora_up, k_k, k_a, r_k, lnx_w, lnx_b, gk_up, gk_b, gla_norm_g,
           w_out, ln2_g, ffn2_wg, ffn2_wu, ffn2_wd, ln_f_g):
    depth = ln1_g.shape[0]
    per_layer = (ln1_g, ffn1_wg, ffn1_wu, ffn1_wd, ln_mix_g, w_in, mu_shift, w0, w_lora_up, a0, a_lora_up, g_lora_up,
                 k_k, k_a, r_k.reshape(depth, -1), lnx_w, lnx_b, gk_up, gk_b, gla_norm_g, w_out, ln2_g, ffn2_wg,
                 ffn2_wu, ffn2_wd)
    layers = [_prep_layer(l, *per_layer) for l in range(depth)]
    ln_f = ln_f_g.reshape(1, -1)
    bp = x_prompt.shape[0]
    shift0 = jnp.zeros((depth, bp) + state_shift.shape[2:], F32)
    wkv0 = jnp.zeros((depth, bp) + state_wkv.shape[2:], F32)
    gla0 = jnp.zeros((depth, bp) + state_gla.shape[2:], F32)
    y_p, shift_p, wkv_p, gla_p = _trunk(x_prompt, shift0, wkv0, gla0, layers, ln_f)
    y_s, shift_s, wkv_s, gla_s = _trunk(x_sample, state_shift, state_wkv, state_gla, layers, ln_f)
    return (y_p, y_s, shift_p, wkv_p, gla_p, shift_s, wkv_s, gla_s)
```

```python
import functools
import math

import jax
import jax.numpy as jnp
from jax import lax
from jax.experimental import pallas as pl
from jax.experimental.pallas import tpu as pltpu

F32 = jnp.float32
BF16 = jnp.bfloat16
HIGHEST = lax.Precision.HIGHEST

EPS = 1e-6
LNX_EPS = 64e-5
A_HEAD = 64
A_W_RANK = 32
A_A_RANK = 32
A_G_RANK = 64
B_HEADS = 4
B_GATE_RANK = 16
B_TAU = 16.0
GLA_CHUNK = 64
SUB = 16
LANE = 128
VMEM_LIMIT = 56 * 1024 * 1024
PRE_TILE = 256
POST_TILE = 512
MIXER_CHUNKS_PER_STEP = 16
MIXER_SEQUENCES_PER_STEP = 16
RWKV_CHUNKS_PER_GROUP = 8
HEAD_SUM_TILE = 256

NN = ((1,), (0,))
NT = ((1,), (1,))
TN = ((0,), (0,))


def _dot(a, b, precision=None):
    return jnp.dot(a, b, preferred_element_type=F32, precision=precision)


def _split2(x):
    hi = x.astype(BF16)
    return hi, (x - hi.astype(F32)).astype(BF16)


def _mm(a, b, dims=NN):
    return lax.dot_general(a.astype(BF16), b.astype(BF16), (dims, ((), ())), preferred_element_type=F32)


def _dot_ones_lhs(ones_bf16, x):
    hi, lo = _split2(x)
    return _dot(ones_bf16, hi) + _dot(ones_bf16, lo)


def _softplus(x):
    return jnp.maximum(x, 0.0) + jnp.log(1.0 + jnp.exp(-jnp.abs(x)))


def _rmsnorm(x, g):
    return x * lax.rsqrt(jnp.mean(x * x, axis=-1, keepdims=True) + EPS) * g


def _swiglu_half_step(x, ln_g, wg_ref, wu_ref, wd_ref):
    h = _rmsnorm(x, ln_g).astype(BF16)
    g = _dot(h, wg_ref[...])
    u = _dot(h, wu_ref[...])
    act = (g * jax.nn.sigmoid(g) * u).astype(BF16)
    return x + 0.5 * _dot(act, wd_ref[...])


def _iota2(n, m):
    return lax.broadcasted_iota(jnp.int32, (n, m), 0), lax.broadcasted_iota(jnp.int32, (n, m), 1)


def _cat(xs, axis):
    return jnp.concatenate(xs, axis=axis)


def _emit(*tasks):
    tasks = list(tasks)
    while tasks:
        for task in list(tasks):
            if next(task, True):
                tasks.remove(task)


def _head_sum(x, eseg):
    w = eseg.shape[0]
    return _cat([_dot(x[:, s:s + w].astype(BF16), eseg) for s in range(0, x.shape[1], w)], 1)


OPERANDS = ("a_hat", "k_hat", "b_hat", "kbar", "bbar", "v", "r_til", "bonus", "g")


N_PRE_INPUTS = 18
N_PRE_OUTPUTS = 5


def _pre_kernel(c_len, n_tiles, n_ride, *refs):
    (x_ref, shift_ref, ln1_ref, wg_ref, wu_ref, wd_ref, lnm_ref, win_ref, mu_ref, w0_ref, wup_ref, a0_ref, aup_ref,
     gup_ref, kk_ref, ka_ref, rk_ref, eseg_ref) = refs[:N_PRE_INPUTS]
    ride_in = refs[N_PRE_INPUTS:N_PRE_INPUTS + n_ride]
    outs = refs[N_PRE_INPUTS + n_ride:]
    x1_ref, pb_ref, ops_ref, wend_ref, last_ref = outs[:N_PRE_OUTPUTS]
    ride_out = outs[N_PRE_OUTPUTS:N_PRE_OUTPUTS + n_ride]
    pa_scr, carry_ref = outs[N_PRE_OUTPUTS + n_ride:]
    step = pl.program_id(0)
    tm = x_ref.shape[0]
    d_a = w0_ref.shape[1]
    n_chunks = tm // c_len
    eseg = eseg_ref[...]
    row1 = lax.broadcasted_iota(jnp.int32, (c_len, 1), 0)
    rt, ct = _iota2(c_len, c_len)
    tri = jnp.where(ct <= rt, 1.0, 0.0).astype(BF16)

    def front_end(load_p, load_prev, chunk):
        p = load_p()
        shifted = jnp.where(row1 == 0, load_prev(), pltpu.roll(p, 1, axis=0))
        last_ref[chunk] = p[c_len - 1:c_len, :]
        xs = p + (shifted - p) * mu_ref[...]
        r = xs[:, 0:d_a]
        k = xs[:, d_a:2 * d_a]
        v = xs[:, 2 * d_a:3 * d_a]
        lora_in = xs[:, 3 * d_a:3 * d_a + LANE]
        lw = _dot(jnp.tanh(lora_in).astype(BF16), wup_ref[...])
        la = _dot(lora_in.astype(BF16), aup_ref[...])
        g = _dot(jax.nn.sigmoid(lora_in).astype(BF16), gup_ref[...])
        yield
        kk = k * kk_ref[...]
        kk_sq = _head_sum(kk * kk, eseg)
        yield
        logw = (-math.exp(-0.5) * math.log2(math.e)) * jax.nn.sigmoid(w0_ref[...] + lw)
        gc = _dot_ones_lhs(tri, logw)
        yield
        a = jax.nn.sigmoid(a0_ref[...] + la)
        kk = kk * lax.rsqrt(jnp.maximum(kk_sq, 1e-24))
        kmod = k * (1.0 + (a - 1.0) * ka_ref[...])
        kka = kk * a
        bonus = _head_sum(r * kmod * rk_ref[...], eseg) * v
        yield
        gc_last = gc[c_len - 1:c_len, :]
        e_inv = jnp.exp2(-gc)
        e_end = jnp.exp2(gc_last - gc)
        ops = dict(a_hat=-kk * jnp.exp2(gc - logw), k_hat=kmod * e_inv, b_hat=kka * e_inv, kbar=kmod * e_end,
                   bbar=kka * e_end, v=v, r_til=r * jnp.exp2(gc), bonus=bonus, g=g)
        rows = slice(chunk * c_len, (chunk + 1) * c_len)
        for n, name in enumerate(OPERANDS):
            ops_ref[rows, n * d_a:(n + 1) * d_a] = ops[name].astype(BF16)
        wend_ref[chunk] = jnp.exp2(gc_last)

    def ffn(out):
        x = x_ref[...]
        h = _rmsnorm(x, ln1_ref[...]).astype(BF16)
        g = _dot(h, wg_ref[...])
        yield
        u = _dot(h, wu_ref[...])
        yield
        act = (g * jax.nn.sigmoid(g) * u).astype(BF16)
        x1 = x + 0.5 * _dot(act, wd_ref[...])
        x1_ref[...] = x1
        yield
        h = _rmsnorm(x1, lnm_ref[...]).astype(BF16)
        a_cols = mu_ref.shape[1]
        out["pa"] = _dot(h, win_ref[:, :a_cols])
        yield
        pb_ref[...] = _dot(h, win_ref[:, a_cols:])

    if n_tiles == 1:
        out = {}
        _emit(ffn(out))
        pa = out["pa"]
        _emit(*[front_end(lambda c=c: pa[c * c_len:(c + 1) * c_len], lambda c=c: shift_ref[c], c)
                for c in range(n_chunks)])
        return

    @pl.when(step == 0)
    def _():
        pa_scr[...] = jnp.zeros(pa_scr.shape, F32)
        carry_ref[...] = shift_ref[0]

    def previous_tile():
        return [front_end(lambda c=c: pa_scr[c * c_len:(c + 1) * c_len, :],
                          lambda c=c: carry_ref[...] if c == 0 else pa_scr[c * c_len - 1:c * c_len, :], c)
                for c in range(n_chunks)]

    @pl.when(step < n_tiles)
    def _():
        out = {}
        _emit(*previous_tile(), ffn(out))
        for src, dst in zip(ride_in, ride_out):
            dst[...] = src[...].astype(BF16)
        carry_ref[...] = jnp.where(step > 0, pa_scr[tm - 1:tm, :], carry_ref[...])
        pa_scr[...] = out["pa"]

    @pl.when(step == n_tiles)
    def _():
        _emit(*previous_tile())


def _resident(shape):
    return pl.BlockSpec(shape, lambda *_: (0,) * len(shape), pipeline_mode=pl.Buffered(1))


def _token_tile(n, tile):
    tm = min(tile, n)
    assert n % tm == 0
    return tm


def _pre_call(x, shift, c_len, ride, ln1, wg, wu, wd, lnm, win, *front_consts):
    n, d = x.shape
    a_cols = front_consts[0].shape[1]
    b_cols = win.shape[1] - a_cols
    d_a = front_consts[1].shape[1]
    tm = _token_tile(n, PRE_TILE)
    n_tiles = n // tm
    n_chunks = tm // c_len
    assert (shift.shape[0] == 1) if n_tiles > 1 else (shift.shape[0] == n_chunks)
    lag = 1 if n_tiles > 1 else 0
    cur = lambda w: pl.BlockSpec((tm, w), lambda i: (jnp.minimum(i, n_tiles - 1), 0))
    late = lambda shp: pl.BlockSpec(shp, lambda i: (jnp.maximum(i - lag, 0),) + (0,) * (len(shp) - 1))
    consts = [ln1, wg, wu, wd, lnm, win, *front_consts]
    assert 2 + len(consts) == N_PRE_INPUTS
    def ride_spec(w):
        nb = max(k for k in range(1, n_tiles + 1) if w.shape[0] % (16 * k) == 0)
        return pl.BlockSpec((w.shape[0] // nb, w.shape[1]), lambda i: (jnp.minimum(i, nb - 1), 0))

    assert not ride or n_tiles > 1
    outs = pl.pallas_call(
        functools.partial(_pre_kernel, c_len, n_tiles, len(ride)),
        grid=(n_tiles + lag,),
        in_specs=[cur(d), _resident(shift.shape)] + [_resident(w.shape) for w in consts] + [ride_spec(w) for w in ride],
        out_specs=[cur(d), cur(b_cols), late((tm, len(OPERANDS) * d_a)), late((n_chunks, 1, d_a)),
                   late((n_chunks, 1, a_cols))] + [ride_spec(w) for w in ride],
        out_shape=[jax.ShapeDtypeStruct((n, d), F32), jax.ShapeDtypeStruct((n, b_cols), F32),
                   jax.ShapeDtypeStruct((n, len(OPERANDS) * d_a), BF16),
                   jax.ShapeDtypeStruct((n // c_len, 1, d_a), F32),
                   jax.ShapeDtypeStruct((n // c_len, 1, a_cols), F32)]
        + [jax.ShapeDtypeStruct(w.shape, BF16) for w in ride],
        scratch_shapes=[pltpu.VMEM((tm, a_cols), F32), pltpu.VMEM((1, a_cols), F32)],
        compiler_params=pltpu.CompilerParams(dimension_semantics=("arbitrary",), vmem_limit_bytes=VMEM_LIMIT),
        name="pre_ffn_inproj",
    )(x, shift, *consts, *ride)
    return outs[:N_PRE_OUTPUTS], outs[N_PRE_OUTPUTS:]


def _post_kernel(final_norm, x1_ref, o_ref, wo_ref, ln2_ref, wg_ref, wu_ref, wd_ref, lnf_ref, y_ref):
    x2 = x1_ref[...] + _dot(o_ref[...].astype(BF16), wo_ref[...])
    x3 = _swiglu_half_step(x2, ln2_ref[...], wg_ref, wu_ref, wd_ref)
    y_ref[...] = _rmsnorm(x3, lnf_ref[...]) if final_norm else x3


def _post_call(x1, o, wo, ln2, wg, wu, wd, lnf, final_norm):
    n, d = x1.shape
    tm = _token_tile(n, POST_TILE)
    tok = lambda w: pl.BlockSpec((tm, w), lambda i: (i, 0))
    consts = [wo, ln2, wg, wu, wd, lnf]
    return pl.pallas_call(
        functools.partial(_post_kernel, final_norm),
        grid=(n // tm,),
        in_specs=[tok(d), tok(o.shape[1])] + [_resident(w.shape) for w in consts],
        out_specs=tok(d),
        out_shape=jax.ShapeDtypeStruct((n, d), F32),
        compiler_params=pltpu.CompilerParams(dimension_semantics=("arbitrary",), vmem_limit_bytes=VMEM_LIMIT),
        name="post_outproj_ffn",
    )(x1, o, *consts)


def _mixer_kernel(c_len, n_sub, chained, ops_ref, wend_ref, pb_ref, wkv0_ref, gla0_ref, lnw_ref, lnb_ref, eseg_ref,
                  gkup_ref, gkb_ref, ng_ref, o_ref, wkv_ref, gla_ref, st_ref):
    step = pl.program_id(1)
    n_pairs = wkv_ref.shape[1] // 2
    d_a = n_pairs * LANE
    c2 = 2 * c_len
    cat = _cat

    def load_state(i, j):
        both = cat([wkv0_ref[i, 2 * j], wkv0_ref[i, 2 * j + 1]], 0)
        return jnp.transpose(cat([both, jnp.zeros((LANE, LANE - A_HEAD), F32)], 1))[:A_HEAD]

    def store_state(i, j, s):
        t = jnp.transpose(cat([s, jnp.zeros((LANE - A_HEAD, LANE), F32)], 0))
        wkv_ref[i, 2 * j] = t[:A_HEAD, :A_HEAD]
        wkv_ref[i, 2 * j + 1] = t[A_HEAD:, :A_HEAD]

    if chained:
        @pl.when(step == 0)
        def _():
            gla_ref[...] = gla0_ref[...]
            for j in range(n_pairs):
                st_ref[j] = load_state(0, j)

    eseg = eseg_ref[...]

    def operand(name, i, j=None):
        col = OPERANDS.index(name) * d_a
        cols = slice(col, col + d_a) if j is None else slice(col + j * LANE, col + (j + 1) * LANE)
        return ops_ref[0, i * c_len:(i + 1) * c_len, cols]

    rr, cc = _iota2(c_len, c2)
    col_t = cc & (c_len - 1)
    strict = col_t < rr
    lower = col_t <= rr
    eye_tt = jnp.where(col_t == rr, 1.0, 0.0).astype(F32)
    rk, ck = _iota2(A_HEAD, LANE)
    eye_kk = jnp.where((ck & (A_HEAD - 1)) == rk, 1.0, 0.0).astype(F32)
    half_k = lax.broadcasted_iota(jnp.int32, (1, LANE), 1) < A_HEAD
    half_t = lax.broadcasted_iota(jnp.int32, (1, c2), 1) < c_len
    n_double = int(math.log2(c_len))
    assert 2 ** n_double == c_len and n_double >= 2

    def bd(x, half):
        xb = x.astype(BF16)
        return cat([xb * jnp.where(half, 1.0, 0.0).astype(BF16), xb * jnp.where(half, 0.0, 1.0).astype(BF16)], 0)

    bd_k = lambda x: bd(x, half_k)
    bd_t = lambda x: bd(x, half_t)
    rq, cq = _iota2(LANE, LANE)
    same_head = jnp.where((rq < A_HEAD) == (cq < A_HEAD), 1.0, 0.0).astype(BF16)

    def bd_kt(x):
        return jnp.transpose(cat([x, x], 0)) * same_head
    st = [st_ref[j] for j in range(n_pairs)] if chained else None
    mixed = {}

    def mix(chunks):
        chains = [(i, j) for i in chunks for j in range(n_pairs)]
        each = lambda f: [f(c) for c in range(len(chains))]
        at = lambda name: [operand(name, i, j) for i, j in chains]
        a_t, r_t, k_t, b_t, v_t = at("a_hat"), at("r_til"), at("k_hat"), at("b_hat"), at("v")
        kbar_t, bbar_t = at("kbar"), at("bbar")
        w_end = [wend_ref[0, i, :, j * LANE:(j + 1) * LANE] for i, j in chains]
        if c2 == LANE:
            bd_b = each(lambda c: bd_kt(b_t[c]))
            bd_kh = each(lambda c: bd_kt(k_t[c]))
            xa = each(lambda c: _mm(a_t[c], cat([bd_b[c], bd_kh[c]], 1)))
            yield
            xr = each(lambda c: _mm(r_t[c], cat([bd_kh[c], bd_b[c]], 1)))
        else:
            bd_b = each(lambda c: bd_k(b_t[c]))
            bd_kh = each(lambda c: bd_k(k_t[c]))
            xa = each(lambda c: _mm(a_t[c], cat([bd_b[c], bd_kh[c]], 0), NT))
            yield
            xr = each(lambda c: _mm(r_t[c], cat([bd_kh[c], bd_b[c]], 0), NT))
        l_mat = each(lambda c: jnp.where(strict, xa[c][:, :c2], 0.0))
        a_ak = each(lambda c: jnp.where(strict, xa[c][:, c2:], 0.0))
        yield
        t_inv = each(lambda c: eye_tt + l_mat[c])
        pw = each(lambda c: _mm(l_mat[c], bd_t(l_mat[c])))
        a_rk_rb = each(lambda c: cat([jnp.where(lower, xr[c][:, :c2], 0.0), jnp.where(lower, xr[c][:, c2:], 0.0)], 1))
        yield
        for _ in range(1, n_double - 1):
            y = each(lambda c: _mm(pw[c], cat([bd_t(pw[c]), bd_t(t_inv[c])], 1)))
            t_inv = each(lambda c: t_inv[c] + y[c][:, c2:])
            pw = each(lambda c: y[c][:, :c2])
            yield
        t_inv = each(lambda c: t_inv[c] + _mm(pw[c], bd_t(t_inv[c])))
        bd_v = each(lambda c: bd_k(v_t[c]))
        a_ak_v = each(lambda c: _mm(a_ak[c], bd_v[c]))
        yield
        pq = each(lambda c: _mm(t_inv[c], cat([bd_k(a_t[c]), bd_k(a_ak_v[c])], 1)))
        p_t = each(lambda c: pq[c][:, :LANE])
        q_t = each(lambda c: pq[c][:, LANE:])
        yield
        hg = each(lambda c: _mm(a_rk_rb[c], cat([cat([bd_v[c], jnp.zeros((c2, LANE), BF16)], 1),
                                                 cat([bd_k(q_t[c]), bd_k(p_t[c])], 1)], 0)))
        yield
        nm = each(lambda c: _mm(cat([kbar_t[c], bbar_t[c]], 0),
                                cat([cat([v_t[c], jnp.zeros((c_len, LANE), BF16)], 1),
                                     cat([q_t[c], p_t[c]], 1).astype(BF16)], 0), TN))
        n_sbs = each(lambda c: jnp.where(half_k, nm[c][:A_HEAD, :LANE], nm[c][A_HEAD:, :LANE]))
        m_sbs = each(lambda c: jnp.where(half_k, nm[c][:A_HEAD, LANE:], nm[c][A_HEAD:, LANE:]) + eye_kk * w_end[c])
        g_pair = each(lambda c: r_t[c] + hg[c][:, LANE:])
        h_pair = each(lambda c: hg[c][:, :LANE])
        yield
        for n, i in enumerate(chunks):
            out_tiles = []
            for j in range(n_pairs):
                c = n * n_pairs + j
                both = _mm(cat([m_sbs[c], g_pair[c]], 0), bd_k(st[j] if chained else load_state(i, j)))
                out_tiles.append(both[A_HEAD:] + h_pair[c])
                if chained:
                    st[j] = both[:A_HEAD] + n_sbs[c]
                else:
                    store_state(i, j, both[:A_HEAD] + n_sbs[c])
            mixed[i] = cat(out_tiles, 1)
            yield

    def post(i):
        o = mixed[i]
        inv_n = 1.0 / A_HEAD
        mean = _head_sum(o, eseg) * inv_n
        cen = o - mean
        yield
        var = _head_sum(cen * cen, eseg) * inv_n
        on = cen * lax.rsqrt(var + LNX_EPS) * lnw_ref[...] + lnb_ref[...]
        yield
        o_ref[0, i * c_len:(i + 1) * c_len, 0:d_a] = (on + operand("bonus", i)) * operand("g", i)

    groups = [list(range(s, min(s + RWKV_CHUNKS_PER_GROUP, n_sub))) for s in range(0, n_sub, RWKV_CHUNKS_PER_GROUP)]
    for n, grp in enumerate(groups):
        gla = _gla_part(c_len, grp, chained, pb_ref, gla0_ref, gkup_ref, gkb_ref, ng_ref, o_ref, d_a, gla_ref)
        _emit(mix(grp), gla, *([post(i) for i in groups[n - 1]] if n > 0 else []))
    _emit(*[post(i) for i in groups[-1]])
    if chained:
        for j in range(n_pairs):
            st_ref[j] = st[j]

        @pl.when(step == pl.num_programs(1) - 1)
        def _():
            for j in range(n_pairs):
                store_state(0, j, st_ref[j])


def _mixer_call(ops, wend, pb, wkv0, gla0, c_len, lnw, lnb, eseg, gkup, gkb, ng):
    b, t, ops_cols = ops.shape
    d_a = wkv0.shape[1] * A_HEAD
    n_heads_b, dk, dv = gla0.shape[1:]
    gla_shape = (n_heads_b // 2, 2 * dk, dv)
    n_chunks = t // c_len
    chained = n_chunks > 1
    if chained:
        n_sub = math.gcd(n_chunks, MIXER_CHUNKS_PER_STEP)
        grid = (b, n_chunks // n_sub)
        per_seq = lambda shp: pl.BlockSpec((1,) + shp, lambda i, j: (i,) + (0,) * len(shp))
    else:
        n_sub = math.gcd(b, MIXER_SEQUENCES_PER_STEP)
        grid = (1, b // n_sub)
        ops, wend, pb = ops.reshape(1, b * t, ops_cols), wend.reshape(1, b, 1, d_a), pb.reshape(1, b * t, -1)
        per_seq = lambda shp: pl.BlockSpec((n_sub,) + shp, lambda i, j: (j,) + (0,) * len(shp))
    seq = lambda w: pl.BlockSpec((1, c_len * n_sub, w), lambda i, j: (i, j, 0))
    consts = [lnw, lnb, eseg, gkup, gkb, ng]
    d_o = d_a + n_heads_b * dv
    o, wkv, gla = pl.pallas_call(
        functools.partial(_mixer_kernel, c_len, n_sub, chained),
        grid=grid,
        in_specs=[seq(ops_cols), pl.BlockSpec((1, n_sub, 1, d_a), lambda i, j: (i, j, 0, 0)), seq(pb.shape[2]),
                  per_seq(wkv0.shape[1:]), per_seq(gla_shape)] + [_resident(w.shape) for w in consts],
        out_specs=[seq(d_o), per_seq(wkv0.shape[1:]), per_seq(gla_shape)],
        out_shape=[jax.ShapeDtypeStruct(ops.shape[:2] + (d_o,), F32), jax.ShapeDtypeStruct(wkv0.shape, F32),
                   jax.ShapeDtypeStruct((b,) + gla_shape, F32)],
        scratch_shapes=[pltpu.VMEM((d_a // LANE, A_HEAD, LANE), F32)],
        compiler_params=pltpu.CompilerParams(dimension_semantics=("arbitrary", "arbitrary"),
                                             vmem_limit_bytes=VMEM_LIMIT),
        name="mixers",
    )(ops, wend, pb, wkv0, gla0.reshape((b,) + gla_shape), *consts)
    return o.reshape(b, t, d_o), wkv, gla.reshape(gla0.shape)


def _gla_part(c_len, chunks, chained, pb_ref, gla0_ref, gkup_ref, gkb_ref, ng_ref, o_ref, o_col, st_ref):
    n_pairs = st_ref.shape[1]
    dk = st_ref.shape[2] // 2
    dv = st_ref.shape[3]
    assert 2 * dk == LANE and dv == LANE
    n_heads = 2 * n_pairs
    nk = n_heads * dk
    nv = n_heads * dv
    n_chunks = len(chunks)
    tb = c_len * n_chunks
    row0 = chunks[0] * c_len
    cat = _cat

    p = pb_ref[0, row0:row0 + tb, :]
    q = p[:, 0:nk] * (dk ** -0.5)
    k = p[:, nk:2 * nk]
    v = p[:, 2 * nk:2 * nk + nv]
    og = p[:, 2 * nk + nv:2 * nk + 2 * nv]
    xgk = p[:, 2 * nk + 2 * nv:2 * nk + 2 * nv + LANE]
    z = _dot(xgk.astype(BF16), gkup_ref[...]) + gkb_ref[...]
    log_a = -_softplus(-z) / B_TAU

    sub = min(SUB, c_len)
    n_sub = c_len // sub
    row, col = _iota2(tb, tb)
    in_chunk = col >= (row & -c_len)
    tri = jnp.where((col <= row) & in_chunk, 1.0, 0.0).astype(BF16)
    b = _dot_ones_lhs(tri, log_a)
    if n_sub > 1:
        tri_start = jnp.where((col < (row & -sub)) & in_chunk, 1.0, 0.0).astype(BF16)
        b_start = _dot_ones_lhs(tri_start, log_a)
    else:
        b_start = jnp.zeros_like(b)
    yield
    q_inter = q * jnp.exp(b)
    q_local = q * jnp.exp(b - b_start)
    rows_of = lambda x, i: x[i * c_len:(i + 1) * c_len]
    b_last = [b[(i + 1) * c_len - 1:(i + 1) * c_len, :] for i in range(n_chunks)]
    k_state = cat([rows_of(k, i) * jnp.exp(b_last[i] - rows_of(b, i)) for i in range(n_chunks)], 0)
    row1 = lax.broadcasted_iota(jnp.int32, (c_len, 1), 0)
    k_local = []
    for i in range(n_chunks):
        per_sub = []
        for s in range(n_sub):
            ref = b_start[i * c_len + s * sub:i * c_len + s * sub + 1, :]
            expo = jnp.where(row1 < (s + 1) * sub, ref - rows_of(b, i), -1e30)
            per_sub.append(rows_of(k, i) * jnp.exp(expo))
        k_local.append(per_sub)

    half0 = lax.broadcasted_iota(jnp.int32, (1, LANE), 1) < dk
    rc, cc = _iota2(c_len, c_len)
    causal = rc >= cc
    top_rows = lax.broadcasted_iota(jnp.int32, (LANE, 1), 0) < dk
    kt = lambda x, i, j: x[i * c_len:(i + 1) * c_len, j * LANE:(j + 1) * LANE]
    vt = lambda x, i, h: x[i * c_len:(i + 1) * c_len, h * dv:(h + 1) * dv]
    pairs = [(i, j) for i in range(n_chunks) for j in range(n_pairs)]
    heads = [(i, h) for i in range(n_chunks) for h in range(n_heads)]

    yield

    def a_rows(i, j, s):
        ql = kt(q_local, i, j)[s * sub:(s + 1) * sub]
        lhs = cat([jnp.where(half0, ql, 0.0), jnp.where(half0, 0.0, ql)], 0)
        return _mm(lhs, kt(k_local[i][s], 0, j), NT)
    a_blk = {(i, j): [a_rows(i, j, s) for s in range(n_sub)] for i, j in pairs}
    yield
    a_mat = {}
    for i, h in heads:
        blocks = [a_blk[(i, h // 2)][s][(h % 2) * sub:(h % 2 + 1) * sub] for s in range(n_sub)]
        a_mat[(i, h)] = jnp.where(causal, cat(blocks, 0) if n_sub > 1 else blocks[0], 0.0)
    o_intra = {(i, h): _mm(a_mat[(i, h)], vt(v, i, h)) for i, h in heads}
    yield
    upd = {}
    dec = {}
    for i, j in pairs:
        kv = _mm(kt(k_state, i, j), v[i * c_len:(i + 1) * c_len, 2 * j * dv:(2 * j + 2) * dv], TN)
        upd[(i, j)] = jnp.where(top_rows, kv[:, :dv], kv[:, dv:])
        dec[(i, j)] = jnp.exp(jnp.transpose(jnp.broadcast_to(kt(b_last[i], 0, j)[0:1], (LANE, LANE))))
    yield
    states = {}
    for j in range(n_pairs):
        if chained:
            s_cur = st_ref[0, j]
            for i in range(n_chunks):
                states[(i, j)] = s_cur
                s_cur = dec[(i, j)] * s_cur + upd[(i, j)]
            st_ref[0, j] = s_cur
        else:
            for i in range(n_chunks):
                states[(i, j)] = gla0_ref[chunks[i], j]
                st_ref[chunks[i], j] = dec[(i, j)] * states[(i, j)] + upd[(i, j)]
    for i in range(n_chunks):
        outs = []
        for h in range(n_heads):
            qi = kt(q_inter, i, h // 2)
            qi = jnp.where(half0, qi, 0.0) if h % 2 == 0 else jnp.where(half0, 0.0, qi)
            o_h = o_intra[(i, h)] + _mm(qi, states[(i, h // 2)])
            o_h = o_h * lax.rsqrt(jnp.mean(o_h * o_h, axis=-1, keepdims=True) + EPS) * ng_ref[...]
            og_h = vt(og, i, h)
            outs.append(o_h * (og_h * jax.nn.sigmoid(og_h)))
        o_ref[0, row0 + i * c_len:row0 + (i + 1) * c_len, o_col:o_col + nv] = cat(outs, 1)
        yield


def _chunk_len(t):
    return GLA_CHUNK if t % GLA_CHUNK == 0 else t


def _pad_rows(w, start, total):
    return jnp.zeros((total, w.shape[1]), w.dtype).at[start:start + w.shape[0]].set(w)


def _prep_layer(l, ln1_g, ffn1_wg, ffn1_wu, ffn1_wd, ln_mix_g, w_in, mu_shift, w0, w_lora_up, a0, a_lora_up,
                g_lora_up, k_k, k_a, r_k, lnx_w, lnx_b, gk_up, gk_b, gla_norm_g, w_out, ln2_g, ffn2_wg, ffn2_wu,
                ffn2_wd):
    d_a = w0.shape[1]
    a_cols = mu_shift.shape[1]
    nk = gk_b.shape[1]
    d_b = w_out.shape[1] - d_a
    row = lambda x: x[l].reshape(1, -1)
    wi = w_in[l]
    n_qkv = a_cols + 2 * nk + d_b
    pad = jnp.zeros((wi.shape[0], LANE - B_GATE_RANK), wi.dtype)
    win = jnp.concatenate([wi[:, :n_qkv], wi[:, n_qkv + B_GATE_RANK:], wi[:, n_qkv:n_qkv + B_GATE_RANK], pad],
                          axis=1).astype(BF16)
    eseg = jnp.kron(jnp.eye(HEAD_SUM_TILE // A_HEAD, dtype=F32), jnp.ones((A_HEAD, A_HEAD), F32)).astype(BF16)
    return dict(
        pre=(row(ln1_g), ffn1_wg[l].astype(BF16), ffn1_wu[l].astype(BF16), ffn1_wd[l].astype(BF16), row(ln_mix_g),
             win, row(mu_shift), row(w0), _pad_rows(w_lora_up[l], 0, LANE).astype(BF16), row(a0),
             _pad_rows(a_lora_up[l], A_W_RANK, LANE).astype(BF16),
             _pad_rows(g_lora_up[l], A_W_RANK + A_A_RANK, LANE).astype(BF16), row(k_k), row(k_a), row(r_k), eseg),
        rwkv=(row(lnx_w), row(lnx_b), eseg),
        gla=(_pad_rows(gk_up[l], 0, LANE).astype(BF16), row(gk_b), row(gla_norm_g)),
        post_weights=(w_out[l], ffn2_wg[l], ffn2_wu[l], ffn2_wd[l]),
        ln2=row(ln2_g),
    )


def _trunk(x, shift, wkv, gla, layers, ln_f, post_bf16):
    b, t, d = x.shape
    c_len = _chunk_len(t)
    xf = x.reshape(b * t, d)
    new_shift, new_wkv, new_gla = [], [], []
    for l, lw in enumerate(layers):
        ride = lw["post_weights"] if post_bf16[l] is None else ()
        (x1, pb, ops, wend, last), cast = _pre_call(xf, shift[l], c_len, ride, *lw["pre"])
        if post_bf16[l] is None:
            post_bf16[l] = cast
        wo, wg, wu, wd = post_bf16[l]
        o, s_a, s_b = _mixer_call(ops.reshape(b, t, -1), wend.reshape(b, t // c_len, 1, -1), pb.reshape(b, t, -1),
                                  wkv[l], gla[l], c_len, *lw["rwkv"], *lw["gla"])
        xf = _post_call(x1, o.reshape(b * t, -1), wo, lw["ln2"], wg, wu, wd, ln_f, final_norm=(l == len(layers) - 1))
        new_shift.append(last.reshape(b, t // c_len, 1, -1)[:, -1])
        new_wkv.append(s_a)
        new_gla.append(s_b)
    return xf.reshape(b, t, d), jnp.stack(new_shift), jnp.stack(new_wkv), jnp.stack(new_gla)


def kernel(x_prompt, x_sample, state_shift, state_wkv, state_gla, ln1_g, ffn1_wg, ffn1_wu, ffn1_wd, ln_mix_g, w_in,
           mu_shift, w0, w_lora_up, a0, a_lora_up, g_lora_up, k_k, k_a, r_k, lnx_w, lnx_b, gk_up, gk_b, gla_norm_g,
           w_out, ln2_g, ffn2_wg, ffn2_wu, ffn2_wd, ln_f_g):
    depth = ln1_g.shape[0]
    per_layer = (ln1_g, ffn1_wg, ffn1_wu, ffn1_wd, ln_mix_g, w_in, mu_shift, w0, w_lora_up, a0, a_lora_up, g_lora_up,
                 k_k, k_a, r_k.reshape(depth, -1), lnx_w, lnx_b, gk_up, gk_b, gla_norm_g, w_out, ln2_g, ffn2_wg,
                 ffn2_wu, ffn2_wd)
    layers = [_prep_layer(l, *per_layer) for l in range(depth)]
    ln_f = ln_f_g.reshape(1, -1)
    bp = x_prompt.shape[0]
    shift0 = jnp.zeros((depth, bp) + state_shift.shape[2:], F32)
    wkv0 = jnp.zeros((depth, bp) + state_wkv.shape[2:], F32)
    gla0 = jnp.zeros((depth, bp) + state_gla.shape[2:], F32)
    post_bf16 = [None] * depth
    y_p, shift_p, wkv_p, gla_p = _trunk(x_prompt, shift0, wkv0, gla0, layers, ln_f, post_bf16)
    y_s, shift_s, wkv_s, gla_s = _trunk(x_sample, state_shift, state_wkv, state_gla, layers, ln_f, post_bf16)
    return (y_p, y_s, shift_p, wkv_p, gla_p, shift_s, wkv_s, gla_s)
```

```python
import functools
import math

import jax
import jax.numpy as jnp
from jax import lax
from jax.experimental import pallas as pl
from jax.experimental.pallas import tpu as pltpu

F32 = jnp.float32
BF16 = jnp.bfloat16

EPS = 1e-6
LNX_EPS = 64e-5
A_HEAD = 64
A_W_RANK = 32
A_A_RANK = 32
B_GATE_RANK = 16
B_TAU = 16.0
GLA_CHUNK = 64
SUB = 16
LANE = 128
VMEM_LIMIT = 56 * 1024 * 1024
PRE_TILE = 256
POST_TILE = 512
MIXER_CHUNKS_PER_STEP = 16
MIXER_SEQUENCES_PER_STEP = 16
RWKV_CHUNKS_PER_GROUP = 8
W_IN_ROW_BLOCKS = 8
HEAD_SUM_TILE = 256

NN = ((1,), (0,))
NT = ((1,), (1,))
TN = ((0,), (0,))


def _dot(a, b, precision=None):
    return jnp.dot(a, b, preferred_element_type=F32, precision=precision)


def _split2(x):
    hi = x.astype(BF16)
    return hi, (x - hi.astype(F32)).astype(BF16)


def _mm(a, b, dims=NN):
    return lax.dot_general(a.astype(BF16), b.astype(BF16), (dims, ((), ())), preferred_element_type=F32)


def _dot_ones_lhs(ones_bf16, x):
    hi, lo = _split2(x)
    return _dot(ones_bf16, hi) + _dot(ones_bf16, lo)


def _softplus(x):
    return jnp.maximum(x, 0.0) + jnp.log(1.0 + jnp.exp(-jnp.abs(x)))


def _rmsnorm(x, g):
    return x * lax.rsqrt(jnp.mean(x * x, axis=-1, keepdims=True) + EPS) * g


def _swiglu_half_step(x, ln_g, wg_ref, wu_ref, wd_ref):
    h = _rmsnorm(x, ln_g).astype(BF16)
    g = _dot(h, wg_ref[...])
    u = _dot(h, wu_ref[...])
    act = (g * jax.nn.sigmoid(g) * u).astype(BF16)
    return x + 0.5 * _dot(act, wd_ref[...])


def _iota2(n, m):
    return lax.broadcasted_iota(jnp.int32, (n, m), 0), lax.broadcasted_iota(jnp.int32, (n, m), 1)


def _cat(xs, axis):
    return jnp.concatenate(xs, axis=axis)


def _emit(*tasks):
    tasks = list(tasks)
    while tasks:
        for task in list(tasks):
            if next(task, True):
                tasks.remove(task)


def _head_sum(x, eseg):
    w = eseg.shape[0]
    return _cat([_dot(x[:, s:s + w].astype(BF16), eseg) for s in range(0, x.shape[1], w)], 1)


OPERANDS = ("a_hat", "k_hat", "b_hat", "kbar", "bbar", "v", "r_til", "bonus", "g")


N_PRE_INPUTS = 18
N_PRE_OUTPUTS = 5


def _pre_kernel(c_len, n_tiles, n_ride, *refs):
    (x_ref, shift_ref, ln1_ref, wg_ref, wu_ref, wd_ref, lnm_ref, win_ref, mu_ref, w0_ref, wup_ref, a0_ref, aup_ref,
     gup_ref, kk_ref, ka_ref, rk_ref, eseg_ref) = refs[:N_PRE_INPUTS]
    ride_in = refs[N_PRE_INPUTS:N_PRE_INPUTS + n_ride]
    outs = refs[N_PRE_INPUTS + n_ride:]
    x1_ref, pb_ref, ops_ref, wend_ref, last_ref = outs[:N_PRE_OUTPUTS]
    ride_out = outs[N_PRE_OUTPUTS:N_PRE_OUTPUTS + n_ride]
    pa_scr, carry_ref = outs[N_PRE_OUTPUTS + n_ride:]
    step = pl.program_id(0)
    tm = x_ref.shape[0]
    d_a = w0_ref.shape[1]
    n_chunks = tm // c_len
    eseg = eseg_ref[...]
    row1 = lax.broadcasted_iota(jnp.int32, (c_len, 1), 0)
    rt, ct = _iota2(c_len, c_len)
    tri = jnp.where(ct <= rt, 1.0, 0.0).astype(BF16)

    def front_end(load_p, load_prev, chunk):
        p = load_p()
        shifted = jnp.where(row1 == 0, load_prev(), pltpu.roll(p, 1, axis=0))
        last_ref[chunk] = p[c_len - 1:c_len, :]
        xs = p + (shifted - p) * mu_ref[...]
        r = xs[:, 0:d_a]
        k = xs[:, d_a:2 * d_a]
        v = xs[:, 2 * d_a:3 * d_a]
        lora_in = xs[:, 3 * d_a:3 * d_a + LANE]
        lw = _dot(jnp.tanh(lora_in).astype(BF16), wup_ref[...])
        la = _dot(lora_in.astype(BF16), aup_ref[...])
        g = _dot(jax.nn.sigmoid(lora_in).astype(BF16), gup_ref[...])
        yield
        kk = k * kk_ref[...]
        kk_sq = _head_sum(kk * kk, eseg)
        yield
        logw = (-math.exp(-0.5) * math.log2(math.e)) * jax.nn.sigmoid(w0_ref[...] + lw)
        gc = _dot_ones_lhs(tri, logw)
        yield
        a = jax.nn.sigmoid(a0_ref[...] + la)
        kk = kk * lax.rsqrt(jnp.maximum(kk_sq, 1e-24))
        kmod = k * (1.0 + (a - 1.0) * ka_ref[...])
        kka = kk * a
        bonus = _head_sum(r * kmod * rk_ref[...], eseg) * v
        yield
        gc_last = gc[c_len - 1:c_len, :]
        e_inv = jnp.exp2(-gc)
        e_end = jnp.exp2(gc_last - gc)
        ops = dict(a_hat=-kk * jnp.exp2(gc - logw), k_hat=kmod * e_inv, b_hat=kka * e_inv, kbar=kmod * e_end,
                   bbar=kka * e_end, v=v, r_til=r * jnp.exp2(gc), bonus=bonus, g=g)
        rows = slice(chunk * c_len, (chunk + 1) * c_len)
        for n, name in enumerate(OPERANDS):
            ops_ref[rows, n * d_a:(n + 1) * d_a] = ops[name].astype(BF16)
        wend_ref[chunk] = jnp.exp2(gc_last)

    def ffn(out):
        x = x_ref[...]
        h = _rmsnorm(x, ln1_ref[...]).astype(BF16)
        g = _dot(h, wg_ref[...])
        yield
        u = _dot(h, wu_ref[...])
        yield
        act = (g * jax.nn.sigmoid(g) * u).astype(BF16)
        x1 = x + 0.5 * _dot(act, wd_ref[...])
        x1_ref[...] = x1
        yield
        h = _rmsnorm(x1, lnm_ref[...]).astype(BF16)
        a_cols = mu_ref.shape[1]
        out["pa"] = _dot(h, win_ref[:, :a_cols])
        yield
        pb_ref[...] = _dot(h, win_ref[:, a_cols:])

    if n_tiles == 1:
        out = {}
        _emit(ffn(out))
        pa = out["pa"]
        _emit(*[front_end(lambda c=c: pa[c * c_len:(c + 1) * c_len], lambda c=c: shift_ref[c], c)
                for c in range(n_chunks)])
        return

    @pl.when(step == 0)
    def _():
        pa_scr[...] = jnp.zeros(pa_scr.shape, F32)
        carry_ref[...] = shift_ref[0]

    def previous_tile():
        return [front_end(lambda c=c: pa_scr[c * c_len:(c + 1) * c_len, :],
                          lambda c=c: carry_ref[...] if c == 0 else pa_scr[c * c_len - 1:c * c_len, :], c)
                for c in range(n_chunks)]

    @pl.when(step < n_tiles)
    def _():
        out = {}
        _emit(*previous_tile(), ffn(out))
        for src, dst in zip(ride_in, ride_out):
            dst[...] = src[...].astype(BF16)
        carry_ref[...] = jnp.where(step > 0, pa_scr[tm - 1:tm, :], carry_ref[...])
        pa_scr[...] = out["pa"]

    @pl.when(step == n_tiles)
    def _():
        _emit(*previous_tile())


def _resident(shape):
    return pl.BlockSpec(shape, lambda *_: (0,) * len(shape), pipeline_mode=pl.Buffered(1))


def _token_tile(n, tile):
    tm = min(tile, n)
    assert n % tm == 0
    return tm


def _pre_call(x, shift, c_len, ride, ln1, wg, wu, wd, lnm, win, *front_consts):
    n, d = x.shape
    a_cols = front_consts[0].shape[1]
    b_cols = win.shape[1] - a_cols
    d_a = front_consts[1].shape[1]
    tm = _token_tile(n, PRE_TILE)
    n_tiles = n // tm
    n_chunks = tm // c_len
    assert (shift.shape[0] == 1) if n_tiles > 1 else (shift.shape[0] == n_chunks)
    lag = 1 if n_tiles > 1 else 0
    cur = lambda w: pl.BlockSpec((tm, w), lambda i: (jnp.minimum(i, n_tiles - 1), 0))
    late = lambda shp: pl.BlockSpec(shp, lambda i: (jnp.maximum(i - lag, 0),) + (0,) * (len(shp) - 1))
    consts = [ln1, wg, wu, wd, lnm, win, *front_consts]
    assert 2 + len(consts) == N_PRE_INPUTS
    def ride_spec(w):
        nb = max(k for k in range(1, n_tiles + 1) if w.shape[0] % (16 * k) == 0)
        return pl.BlockSpec((w.shape[0] // nb, w.shape[1]), lambda i: (jnp.minimum(i, nb - 1), 0))

    assert not ride or n_tiles > 1
    outs = pl.pallas_call(
        functools.partial(_pre_kernel, c_len, n_tiles, len(ride)),
        grid=(n_tiles + lag,),
        in_specs=[cur(d), _resident(shift.shape)] + [_resident(w.shape) for w in consts] + [ride_spec(w) for w in ride],
        out_specs=[cur(d), cur(b_cols), late((tm, len(OPERANDS) * d_a)), late((n_chunks, 1, d_a)),
                   late((n_chunks, 1, a_cols))] + [ride_spec(w) for w in ride],
        out_shape=[jax.ShapeDtypeStruct((n, d), F32), jax.ShapeDtypeStruct((n, b_cols), F32),
                   jax.ShapeDtypeStruct((n, len(OPERANDS) * d_a), BF16),
                   jax.ShapeDtypeStruct((n // c_len, 1, d_a), F32),
                   jax.ShapeDtypeStruct((n // c_len, 1, a_cols), F32)]
        + [jax.ShapeDtypeStruct(w.shape, BF16) for w in ride],
        scratch_shapes=[pltpu.VMEM((tm, a_cols), F32), pltpu.VMEM((1, a_cols), F32)],
        compiler_params=pltpu.CompilerParams(dimension_semantics=("arbitrary",), vmem_limit_bytes=VMEM_LIMIT),
        name="pre_ffn_inproj",
    )(x, shift, *consts, *ride)
    return outs[:N_PRE_OUTPUTS], outs[N_PRE_OUTPUTS:]


def _post_kernel(final_norm, x1_ref, o_ref, wo_ref, ln2_ref, wg_ref, wu_ref, wd_ref, lnf_ref, y_ref):
    x2 = x1_ref[...] + _dot(o_ref[...].astype(BF16), wo_ref[...])
    x3 = _swiglu_half_step(x2, ln2_ref[...], wg_ref, wu_ref, wd_ref)
    y_ref[...] = _rmsnorm(x3, lnf_ref[...]) if final_norm else x3


def _post_call(x1, o, wo, ln2, wg, wu, wd, lnf, final_norm):
    n, d = x1.shape
    tm = _token_tile(n, POST_TILE)
    tok = lambda w: pl.BlockSpec((tm, w), lambda i: (i, 0))
    consts = [wo, ln2, wg, wu, wd, lnf]
    return pl.pallas_call(
        functools.partial(_post_kernel, final_norm),
        grid=(n // tm,),
        in_specs=[tok(d), tok(o.shape[1])] + [_resident(w.shape) for w in consts],
        out_specs=tok(d),
        out_shape=jax.ShapeDtypeStruct((n, d), F32),
        compiler_params=pltpu.CompilerParams(dimension_semantics=("arbitrary",), vmem_limit_bytes=VMEM_LIMIT),
        name="post_outproj_ffn",
    )(x1, o, *consts)


def _mixer_kernel(c_len, n_sub, chained, ops_ref, wend_ref, pb_ref, wkv0_ref, gla0_ref, lnw_ref, lnb_ref, eseg_ref,
                  gkup_ref, gkb_ref, ng_ref, o_ref, wkv_ref, gla_ref, st_ref):
    step = pl.program_id(1)
    n_pairs = wkv_ref.shape[1] // 2
    d_a = n_pairs * LANE
    c2 = 2 * c_len
    cat = _cat

    def load_state(i, j):
        both = cat([wkv0_ref[i, 2 * j], wkv0_ref[i, 2 * j + 1]], 0)
        return jnp.transpose(cat([both, jnp.zeros((LANE, LANE - A_HEAD), F32)], 1))[:A_HEAD]

    def store_state(i, j, s):
        t = jnp.transpose(cat([s, jnp.zeros((LANE - A_HEAD, LANE), F32)], 0))
        wkv_ref[i, 2 * j] = t[:A_HEAD, :A_HEAD]
        wkv_ref[i, 2 * j + 1] = t[A_HEAD:, :A_HEAD]

    if chained:
        @pl.when(step == 0)
        def _():
            gla_ref[...] = gla0_ref[...]
            for j in range(n_pairs):
                st_ref[j] = load_state(0, j)

    eseg = eseg_ref[...]

    def operand(name, i, j=None):
        col = OPERANDS.index(name) * d_a
        cols = slice(col, col + d_a) if j is None else slice(col + j * LANE, col + (j + 1) * LANE)
        return ops_ref[0, i * c_len:(i + 1) * c_len, cols]

    rr, cc = _iota2(c_len, c2)
    col_t = cc & (c_len - 1)
    strict = col_t < rr
    lower = col_t <= rr
    eye_tt = jnp.where(col_t == rr, 1.0, 0.0).astype(F32)
    rk, ck = _iota2(A_HEAD, LANE)
    eye_kk = jnp.where((ck & (A_HEAD - 1)) == rk, 1.0, 0.0).astype(F32)
    half_k = lax.broadcasted_iota(jnp.int32, (1, LANE), 1) < A_HEAD
    half_t = lax.broadcasted_iota(jnp.int32, (1, c2), 1) < c_len
    n_double = int(math.log2(c_len))
    assert 2 ** n_double == c_len and n_double >= 2

    def bd(x, half):
        xb = x.astype(BF16)
        return cat([xb * jnp.where(half, 1.0, 0.0).astype(BF16), xb * jnp.where(half, 0.0, 1.0).astype(BF16)], 0)

    bd_k = lambda x: bd(x, half_k)
    bd_t = lambda x: bd(x, half_t)
    rq, cq = _iota2(LANE, LANE)
    same_head = jnp.where((rq < A_HEAD) == (cq < A_HEAD), 1.0, 0.0).astype(BF16)

    def bd_kt(x):
        return jnp.transpose(cat([x, x], 0)) * same_head
    st = [st_ref[j] for j in range(n_pairs)] if chained else None
    mixed = {}

    def mix(chunks):
        chains = [(i, j) for i in chunks for j in range(n_pairs)]
        each = lambda f: [f(c) for c in range(len(chains))]
        at = lambda name: [operand(name, i, j) for i, j in chains]
        a_t, r_t, k_t, b_t, v_t = at("a_hat"), at("r_til"), at("k_hat"), at("b_hat"), at("v")
        kbar_t, bbar_t = at("kbar"), at("bbar")
        w_end = [wend_ref[0, i, :, j * LANE:(j + 1) * LANE] for i, j in chains]
        if c2 == LANE:
            bd_b = each(lambda c: bd_kt(b_t[c]))
            bd_kh = each(lambda c: bd_kt(k_t[c]))
            xa = each(lambda c: _mm(a_t[c], cat([bd_b[c], bd_kh[c]], 1)))
            yield
            xr = each(lambda c: _mm(r_t[c], cat([bd_kh[c], bd_b[c]], 1)))
        else:
            bd_b = each(lambda c: bd_k(b_t[c]))
            bd_kh = each(lambda c: bd_k(k_t[c]))
            xa = each(lambda c: _mm(a_t[c], cat([bd_b[c], bd_kh[c]], 0), NT))
            yield
            xr = each(lambda c: _mm(r_t[c], cat([bd_kh[c], bd_b[c]], 0), NT))
        l_mat = each(lambda c: jnp.where(strict, xa[c][:, :c2], 0.0))
        a_ak = each(lambda c: jnp.where(strict, xa[c][:, c2:], 0.0))
        yield
        t_inv = each(lambda c: eye_tt + l_mat[c])
        pw = each(lambda c: _mm(l_mat[c], bd_t(l_mat[c])))
        a_rk_rb = each(lambda c: cat([jnp.where(lower, xr[c][:, :c2], 0.0), jnp.where(lower, xr[c][:, c2:], 0.0)], 1))
        yield
        for _ in range(1, n_double - 1):
            y = each(lambda c: _mm(pw[c], cat([bd_t(pw[c]), bd_t(t_inv[c])], 1)))
            t_inv = each(lambda c: t_inv[c] + y[c][:, c2:])
            pw = each(lambda c: y[c][:, :c2])
            yield
        t_inv = each(lambda c: t_inv[c] + _mm(pw[c], bd_t(t_inv[c])))
        bd_v = each(lambda c: bd_k(v_t[c]))
        a_ak_v = each(lambda c: _mm(a_ak[c], bd_v[c]))
        yield
        pq = each(lambda c: _mm(t_inv[c], cat([bd_k(a_t[c]), bd_k(a_ak_v[c])], 1)))
        p_t = each(lambda c: pq[c][:, :LANE])
        q_t = each(lambda c: pq[c][:, LANE:])
        yield
        hg = each(lambda c: _mm(a_rk_rb[c], cat([cat([bd_v[c], jnp.zeros((c2, LANE), BF16)], 1),
                                                 cat([bd_k(q_t[c]), bd_k(p_t[c])], 1)], 0)))
        yield
        nm = each(lambda c: _mm(cat([kbar_t[c], bbar_t[c]], 0),
                                cat([cat([v_t[c], jnp.zeros((c_len, LANE), BF16)], 1),
                                     cat([q_t[c], p_t[c]], 1).astype(BF16)], 0), TN))
        n_sbs = each(lambda c: jnp.where(half_k, nm[c][:A_HEAD, :LANE], nm[c][A_HEAD:, :LANE]))
        m_sbs = each(lambda c: jnp.where(half_k, nm[c][:A_HEAD, LANE:], nm[c][A_HEAD:, LANE:]) + eye_kk * w_end[c])
        g_pair = each(lambda c: r_t[c] + hg[c][:, LANE:])
        h_pair = each(lambda c: hg[c][:, :LANE])
        yield
        for n, i in enumerate(chunks):
            out_tiles = []
            for j in range(n_pairs):
                c = n * n_pairs + j
                both = _mm(cat([m_sbs[c], g_pair[c]], 0), bd_k(st[j] if chained else load_state(i, j)))
                out_tiles.append(both[A_HEAD:] + h_pair[c])
                if chained:
                    st[j] = both[:A_HEAD] + n_sbs[c]
                else:
                    store_state(i, j, both[:A_HEAD] + n_sbs[c])
            mixed[i] = cat(out_tiles, 1)
            yield

    def post(i):
        o = mixed[i]
        inv_n = 1.0 / A_HEAD
        mean = _head_sum(o, eseg) * inv_n
        cen = o - mean
        yield
        var = _head_sum(cen * cen, eseg) * inv_n
        on = cen * lax.rsqrt(var + LNX_EPS) * lnw_ref[...] + lnb_ref[...]
        yield
        o_ref[0, i * c_len:(i + 1) * c_len, 0:d_a] = (on + operand("bonus", i)) * operand("g", i)

    groups = [list(range(s, min(s + RWKV_CHUNKS_PER_GROUP, n_sub))) for s in range(0, n_sub, RWKV_CHUNKS_PER_GROUP)]
    for n, grp in enumerate(groups):
        gla = _gla_part(c_len, grp, chained, pb_ref, gla0_ref, gkup_ref, gkb_ref, ng_ref, o_ref, d_a, gla_ref)
        _emit(mix(grp), gla, *([post(i) for i in groups[n - 1]] if n > 0 else []))
    _emit(*[post(i) for i in groups[-1]])
    if chained:
        for j in range(n_pairs):
            st_ref[j] = st[j]

        @pl.when(step == pl.num_programs(1) - 1)
        def _():
            for j in range(n_pairs):
                store_state(0, j, st_ref[j])


def _mixer_call(ops, wend, pb, wkv0, gla0, c_len, lnw, lnb, eseg, gkup, gkb, ng):
    b, t, ops_cols = ops.shape
    d_a = wkv0.shape[1] * A_HEAD
    n_heads_b, dk, dv = gla0.shape[1:]
    gla_shape = (n_heads_b // 2, 2 * dk, dv)
    n_chunks = t // c_len
    chained = n_chunks > 1
    if chained:
        n_sub = math.gcd(n_chunks, MIXER_CHUNKS_PER_STEP)
        grid = (b, n_chunks // n_sub)
        per_seq = lambda shp: pl.BlockSpec((1,) + shp, lambda i, j: (i,) + (0,) * len(shp))
    else:
        n_sub = math.gcd(b, MIXER_SEQUENCES_PER_STEP)
        grid = (1, b // n_sub)
        ops, wend, pb = ops.reshape(1, b * t, ops_cols), wend.reshape(1, b, 1, d_a), pb.reshape(1, b * t, -1)
        per_seq = lambda shp: pl.BlockSpec((n_sub,) + shp, lambda i, j: (j,) + (0,) * len(shp))
    seq = lambda w: pl.BlockSpec((1, c_len * n_sub, w), lambda i, j: (i, j, 0))
    consts = [lnw, lnb, eseg, gkup, gkb, ng]
    d_o = d_a + n_heads_b * dv
    o, wkv, gla = pl.pallas_call(
        functools.partial(_mixer_kernel, c_len, n_sub, chained),
        grid=grid,
        in_specs=[seq(ops_cols), pl.BlockSpec((1, n_sub, 1, d_a), lambda i, j: (i, j, 0, 0)), seq(pb.shape[2]),
                  per_seq(wkv0.shape[1:]), per_seq(gla_shape)] + [_resident(w.shape) for w in consts],
        out_specs=[seq(d_o), per_seq(wkv0.shape[1:]), per_seq(gla_shape)],
        out_shape=[jax.ShapeDtypeStruct(ops.shape[:2] + (d_o,), F32), jax.ShapeDtypeStruct(wkv0.shape, F32),
                   jax.ShapeDtypeStruct((b,) + gla_shape, F32)],
        scratch_shapes=[pltpu.VMEM((d_a // LANE, A_HEAD, LANE), F32)],
        compiler_params=pltpu.CompilerParams(dimension_semantics=("arbitrary", "arbitrary"),
                                             vmem_limit_bytes=VMEM_LIMIT),
        name="mixers",
    )(ops, wend, pb, wkv0, gla0.reshape((b,) + gla_shape), *consts)
    return o.reshape(b, t, d_o), wkv, gla.reshape(gla0.shape)


def _gla_part(c_len, chunks, chained, pb_ref, gla0_ref, gkup_ref, gkb_ref, ng_ref, o_ref, o_col, st_ref):
    n_pairs = st_ref.shape[1]
    dk = st_ref.shape[2] // 2
    dv = st_ref.shape[3]
    assert 2 * dk == LANE and dv == LANE
    n_heads = 2 * n_pairs
    nk = n_heads * dk
    nv = n_heads * dv
    n_chunks = len(chunks)
    tb = c_len * n_chunks
    row0 = chunks[0] * c_len
    cat = _cat

    p = pb_ref[0, row0:row0 + tb, :]
    q = p[:, 0:nk] * (dk ** -0.5)
    k = p[:, nk:2 * nk]
    v = p[:, 2 * nk:2 * nk + nv]
    og = p[:, 2 * nk + nv:2 * nk + 2 * nv]
    xgk = p[:, 2 * nk + 2 * nv:2 * nk + 2 * nv + LANE]
    z = _dot(xgk.astype(BF16), gkup_ref[...]) + gkb_ref[...]
    log_a = -_softplus(-z) / B_TAU

    sub = min(SUB, c_len)
    n_sub = c_len // sub
    row, col = _iota2(tb, tb)
    in_chunk = col >= (row & -c_len)
    tri = jnp.where((col <= row) & in_chunk, 1.0, 0.0).astype(BF16)
    b = _dot_ones_lhs(tri, log_a)
    if n_sub > 1:
        tri_start = jnp.where((col < (row & -sub)) & in_chunk, 1.0, 0.0).astype(BF16)
        b_start = _dot_ones_lhs(tri_start, log_a)
    else:
        b_start = jnp.zeros_like(b)
    yield
    q_inter = q * jnp.exp(b)
    q_local = q * jnp.exp(b - b_start)
    rows_of = lambda x, i: x[i * c_len:(i + 1) * c_len]
    b_last = [b[(i + 1) * c_len - 1:(i + 1) * c_len, :] for i in range(n_chunks)]
    k_state = cat([rows_of(k, i) * jnp.exp(b_last[i] - rows_of(b, i)) for i in range(n_chunks)], 0)
    row1 = lax.broadcasted_iota(jnp.int32, (c_len, 1), 0)
    k_local = []
    for i in range(n_chunks):
        per_sub = []
        for s in range(n_sub):
            ref = b_start[i * c_len + s * sub:i * c_len + s * sub + 1, :]
            expo = jnp.where(row1 < (s + 1) * sub, ref - rows_of(b, i), -1e30)
            per_sub.append(rows_of(k, i) * jnp.exp(expo))
        k_local.append(per_sub)

    half0 = lax.broadcasted_iota(jnp.int32, (1, LANE), 1) < dk
    rc, cc = _iota2(c_len, c_len)
    causal = rc >= cc
    top_rows = lax.broadcasted_iota(jnp.int32, (LANE, 1), 0) < dk
    kt = lambda x, i, j: x[i * c_len:(i + 1) * c_len, j * LANE:(j + 1) * LANE]
    vt = lambda x, i, h: x[i * c_len:(i + 1) * c_len, h * dv:(h + 1) * dv]
    pairs = [(i, j) for i in range(n_chunks) for j in range(n_pairs)]
    heads = [(i, h) for i in range(n_chunks) for h in range(n_heads)]

    yield

    def a_rows(i, j, s):
        ql = kt(q_local, i, j)[s * sub:(s + 1) * sub]
        lhs = cat([jnp.where(half0, ql, 0.0), jnp.where(half0, 0.0, ql)], 0)
        return _mm(lhs, kt(k_local[i][s], 0, j), NT)
    a_blk = {(i, j): [a_rows(i, j, s) for s in range(n_sub)] for i, j in pairs}
    yield
    a_mat = {}
    for i, h in heads:
        blocks = [a_blk[(i, h // 2)][s][(h % 2) * sub:(h % 2 + 1) * sub] for s in range(n_sub)]
        a_mat[(i, h)] = jnp.where(causal, cat(blocks, 0) if n_sub > 1 else blocks[0], 0.0)
    o_intra = {(i, h): _mm(a_mat[(i, h)], vt(v, i, h)) for i, h in heads}
    yield
    upd = {}
    dec = {}
    for i, j in pairs:
        kv = _mm(kt(k_state, i, j), v[i * c_len:(i + 1) * c_len, 2 * j * dv:(2 * j + 2) * dv], TN)
        upd[(i, j)] = jnp.where(top_rows, kv[:, :dv], kv[:, dv:])
        dec[(i, j)] = jnp.exp(jnp.transpose(jnp.broadcast_to(kt(b_last[i], 0, j)[0:1], (LANE, LANE))))
    yield
    states = {}
    for j in range(n_pairs):
        if chained:
            s_cur = st_ref[0, j]
            for i in range(n_chunks):
                states[(i, j)] = s_cur
                s_cur = dec[(i, j)] * s_cur + upd[(i, j)]
            st_ref[0, j] = s_cur
        else:
            for i in range(n_chunks):
                states[(i, j)] = gla0_ref[chunks[i], j]
                st_ref[chunks[i], j] = dec[(i, j)] * states[(i, j)] + upd[(i, j)]
    for i in range(n_chunks):
        outs = []
        for h in range(n_heads):
            qi = kt(q_inter, i, h // 2)
            qi = jnp.where(half0, qi, 0.0) if h % 2 == 0 else jnp.where(half0, 0.0, qi)
            o_h = o_intra[(i, h)] + _mm(qi, states[(i, h // 2)])
            o_h = o_h * lax.rsqrt(jnp.mean(o_h * o_h, axis=-1, keepdims=True) + EPS) * ng_ref[...]
            og_h = vt(og, i, h)
            outs.append(o_h * (og_h * jax.nn.sigmoid(og_h)))
        o_ref[0, row0 + i * c_len:row0 + (i + 1) * c_len, o_col:o_col + nv] = cat(outs, 1)
        yield


def _chunk_len(t):
    return GLA_CHUNK if t % GLA_CHUNK == 0 else t


def _regroup_w_in_kernel(n_keep, w_ref, o_ref):
    x = w_ref[...]
    n_in = x.shape[1]
    n_og = n_in - n_keep - B_GATE_RANK
    o_ref[:, 0:n_keep] = x[:, 0:n_keep].astype(BF16)
    o_ref[:, n_keep:n_keep + n_og] = x[:, n_keep + B_GATE_RANK:].astype(BF16)
    o_ref[:, n_keep + n_og:n_in] = x[:, n_keep:n_keep + B_GATE_RANK].astype(BF16)
    o_ref[:, n_in:] = jnp.zeros((x.shape[0], o_ref.shape[1] - n_in), BF16)


def _regroup_w_in_call(wi, n_keep):
    d, n_in = wi.shape
    assert n_keep % LANE == 0
    n_out = -(-n_in // LANE) * LANE
    rows = d // W_IN_ROW_BLOCKS
    return pl.pallas_call(
        functools.partial(_regroup_w_in_kernel, n_keep),
        grid=(W_IN_ROW_BLOCKS,),
        in_specs=[pl.BlockSpec((rows, n_in), lambda i: (i, 0))],
        out_specs=pl.BlockSpec((rows, n_out), lambda i: (i, 0)),
        out_shape=jax.ShapeDtypeStruct((d, n_out), BF16),
        compiler_params=pltpu.CompilerParams(dimension_semantics=("arbitrary",)),
        name="regroup_w_in",
    )(wi)


def _pad_rows(w, start, total):
    return jnp.zeros((total, w.shape[1]), w.dtype).at[start:start + w.shape[0]].set(w)


def _prep_layer(l, ln1_g, ffn1_wg, ffn1_wu, ffn1_wd, ln_mix_g, w_in, mu_shift, w0, w_lora_up, a0, a_lora_up,
                g_lora_up, k_k, k_a, r_k, lnx_w, lnx_b, gk_up, gk_b, gla_norm_g, w_out, ln2_g, ffn2_wg, ffn2_wu,
                ffn2_wd):
    d_a = w0.shape[1]
    a_cols = mu_shift.shape[1]
    nk = gk_b.shape[1]
    d_b = w_out.shape[1] - d_a
    row = lambda x: x[l].reshape(1, -1)
    wi = w_in[l]
    win = _regroup_w_in_call(wi, a_cols + 2 * nk + d_b)
    eseg = jnp.kron(jnp.eye(HEAD_SUM_TILE // A_HEAD, dtype=F32), jnp.ones((A_HEAD, A_HEAD), F32)).astype(BF16)
    return dict(
        pre=(row(ln1_g), ffn1_wg[l].astype(BF16), ffn1_wu[l].astype(BF16), ffn1_wd[l].astype(BF16), row(ln_mix_g),
             win, row(mu_shift), row(w0), _pad_rows(w_lora_up[l], 0, LANE).astype(BF16), row(a0),
             _pad_rows(a_lora_up[l], A_W_RANK, LANE).astype(BF16),
             _pad_rows(g_lora_up[l], A_W_RANK + A_A_RANK, LANE).astype(BF16), row(k_k), row(k_a), row(r_k), eseg),
        rwkv=(row(lnx_w), row(lnx_b), eseg),
        gla=(_pad_rows(gk_up[l], 0, LANE).astype(BF16), row(gk_b), row(gla_norm_g)),
        post_weights=(w_out[l], ffn2_wg[l], ffn2_wu[l], ffn2_wd[l]),
        ln2=row(ln2_g),
    )


def _trunk(x, shift, wkv, gla, layers, ln_f, post_bf16):
    b, t, d = x.shape
    c_len = _chunk_len(t)
    xf = x.reshape(b * t, d)
    new_shift, new_wkv, new_gla = [], [], []
    for l, lw in enumerate(layers):
        ride = lw["post_weights"] if post_bf16[l] is None else ()
        (x1, pb, ops, wend, last), cast = _pre_call(xf, shift[l], c_len, ride, *lw["pre"])
        if post_bf16[l] is None:
            post_bf16[l] = cast
        wo, wg, wu, wd = post_bf16[l]
        o, s_a, s_b = _mixer_call(ops.reshape(b, t, -1), wend.reshape(b, t // c_len, 1, -1), pb.reshape(b, t, -1),
                                  wkv[l], gla[l], c_len, *lw["rwkv"], *lw["gla"])
        xf = _post_call(x1, o.reshape(b * t, -1), wo, lw["ln2"], wg, wu, wd, ln_f, final_norm=(l == len(layers) - 1))
        new_shift.append(last.reshape(b, t // c_len, 1, -1)[:, -1])
        new_wkv.append(s_a)
        new_gla.append(s_b)
    return xf.reshape(b, t, d), jnp.stack(new_shift), jnp.stack(new_wkv), jnp.stack(new_gla)


def kernel(x_prompt, x_sample, state_shift, state_wkv, state_gla, ln1_g, ffn1_wg, ffn1_wu, ffn1_wd, ln_mix_g, w_in,
           mu_shift, w0, w_lora_up, a0, a_lora_up, g_lora_up, k_k, k_a, r_k, lnx_w, lnx_b, gk_up, gk_b, gla_norm_g,
           w_out, ln2_g, ffn2_wg, ffn2_wu, ffn2_wd, ln_f_g):
    depth = ln1_g.shape[0]
    per_layer = (ln1_g, ffn1_wg, ffn1_wu, ffn1_wd, ln_mix_g, w_in, mu_shift, w0, w_lora_up, a0, a_lora_up, g_lora_up,
                 k_k, k_a, r_k.reshape(depth, -1), lnx_w, lnx_b, gk_up, gk_b, gla_norm_g, w_out, ln2_g, ffn2_wg,
                 ffn2_wu, ffn2_wd)
    layers = [_prep_layer(l, *per_layer) for l in range(depth)]
    ln_f = ln_f_g.reshape(1, -1)
    bp = x_prompt.shape[0]
    shift0 = jnp.zeros((depth, bp) + state_shift.shape[2:], F32)
    wkv0 = jnp.zeros((depth, bp) + state_wkv.shape[2:], F32)
    gla0 = jnp.zeros((depth, bp) + state_gla.shape[2:], F32)
    post_bf16 = [None] * depth
    y_p, shift_p, wkv_p, gla_p = _trunk(x_prompt, shift0, wkv0, gla0, layers, ln_f, post_bf16)
    y_s, shift_s, wkv_s, gla_s = _trunk(x_sample, state_shift, state_wkv, state_gla, layers, ln_f, post_bf16)
    return (y_p, y_s, shift_p, wkv_p, gla_p, shift_s, wkv_s, gla_s)
```

```python
import functools
import math

import jax
import jax.numpy as jnp
from jax import lax
from jax.experimental import pallas as pl
from jax.experimental.pallas import tpu as pltpu

F32 = jnp.float32
BF16 = jnp.bfloat16

EPS = 1e-6
LNX_EPS = 64e-5
A_HEAD = 64
A_W_RANK = 32
A_A_RANK = 32
B_GATE_RANK = 16
B_TAU = 16.0
GLA_CHUNK = 64
SUB = 16
LANE = 128
VMEM_LIMIT = 56 * 1024 * 1024
PRE_TILE = 256
POST_TILE = 512
MIXER_CHUNKS_PER_STEP = 16
MIXER_SEQUENCES_PER_STEP = 16
RWKV_CHUNKS_PER_GROUP = 8
HEAD_SUM_TILE = 256

NN = ((1,), (0,))
NT = ((1,), (1,))
TN = ((0,), (0,))


def _dot(a, b, precision=None):
    return jnp.dot(a, b, preferred_element_type=F32, precision=precision)


def _split2(x):
    hi = x.astype(BF16)
    return hi, (x - hi.astype(F32)).astype(BF16)


def _mm(a, b, dims=NN):
    return lax.dot_general(a.astype(BF16), b.astype(BF16), (dims, ((), ())), preferred_element_type=F32)


def _dot_ones_lhs(ones_bf16, x):
    hi, lo = _split2(x)
    return _dot(ones_bf16, hi) + _dot(ones_bf16, lo)


def _softplus(x):
    return jnp.maximum(x, 0.0) + jnp.log(1.0 + jnp.exp(-jnp.abs(x)))


def _rmsnorm(x, g):
    return x * lax.rsqrt(jnp.mean(x * x, axis=-1, keepdims=True) + EPS) * g


def _swiglu_half_step(x, ln_g, wg_ref, wu_ref, wd_ref):
    h = _rmsnorm(x, ln_g).astype(BF16)
    g = _dot(h, wg_ref[...])
    u = _dot(h, wu_ref[...])
    act = (g * jax.nn.sigmoid(g) * u).astype(BF16)
    return x + 0.5 * _dot(act, wd_ref[...])


def _iota2(n, m):
    return lax.broadcasted_iota(jnp.int32, (n, m), 0), lax.broadcasted_iota(jnp.int32, (n, m), 1)


def _cat(xs, axis):
    return jnp.concatenate(xs, axis=axis)


def _emit(*tasks):
    tasks = list(tasks)
    while tasks:
        for task in list(tasks):
            if next(task, True):
                tasks.remove(task)


def _head_sum(x, eseg):
    w = eseg.shape[0]
    return _cat([_dot(x[:, s:s + w].astype(BF16), eseg) for s in range(0, x.shape[1], w)], 1)


OPERANDS = ("a_hat", "k_hat", "b_hat", "kbar", "bbar", "v", "r_til", "bonus", "g")


N_PRE_INPUTS = 18
N_PRE_OUTPUTS = 5


def _pre_kernel(c_len, n_tiles, n_ride, *refs):
    (x_ref, shift_ref, ln1_ref, wg_ref, wu_ref, wd_ref, lnm_ref, win_ref, mu_ref, w0_ref, wup_ref, a0_ref, aup_ref,
     gup_ref, kk_ref, ka_ref, rk_ref, eseg_ref) = refs[:N_PRE_INPUTS]
    ride_in = refs[N_PRE_INPUTS:N_PRE_INPUTS + n_ride]
    outs = refs[N_PRE_INPUTS + n_ride:]
    x1_ref, pb_ref, ops_ref, wend_ref, last_ref = outs[:N_PRE_OUTPUTS]
    ride_out = outs[N_PRE_OUTPUTS:N_PRE_OUTPUTS + n_ride]
    pa_scr, carry_ref = outs[N_PRE_OUTPUTS + n_ride:]
    step = pl.program_id(0)
    tm = x_ref.shape[0]
    d_a = w0_ref.shape[1]
    n_chunks = tm // c_len
    eseg = eseg_ref[...]
    row1 = lax.broadcasted_iota(jnp.int32, (c_len, 1), 0)
    rt, ct = _iota2(c_len, c_len)
    tri = jnp.where(ct <= rt, 1.0, 0.0).astype(BF16)

    def front_end(load_p, load_prev, chunk):
        p = load_p()
        shifted = jnp.where(row1 == 0, load_prev(), pltpu.roll(p, 1, axis=0))
        last_ref[chunk] = p[c_len - 1:c_len, :]
        xs = p + (shifted - p) * mu_ref[...]
        r = xs[:, 0:d_a]
        k = xs[:, d_a:2 * d_a]
        v = xs[:, 2 * d_a:3 * d_a]
        lora_in = xs[:, 3 * d_a:3 * d_a + LANE]
        lw = _dot(jnp.tanh(lora_in).astype(BF16), wup_ref[...])
        la = _dot(lora_in.astype(BF16), aup_ref[...])
        g = _dot(jax.nn.sigmoid(lora_in).astype(BF16), gup_ref[...])
        yield
        kk = k * kk_ref[...]
        kk_sq = _head_sum(kk * kk, eseg)
        yield
        logw = (-math.exp(-0.5) * math.log2(math.e)) * jax.nn.sigmoid(w0_ref[...] + lw)
        gc = _dot_ones_lhs(tri, logw)
        yield
        a = jax.nn.sigmoid(a0_ref[...] + la)
        kk = kk * lax.rsqrt(jnp.maximum(kk_sq, 1e-24))
        kmod = k * (1.0 + (a - 1.0) * ka_ref[...])
        kka = kk * a
        bonus = _head_sum(r * kmod * rk_ref[...], eseg) * v
        yield
        gc_last = gc[c_len - 1:c_len, :]
        e_inv = jnp.exp2(-gc)
        e_end = jnp.exp2(gc_last - gc)
        ops = dict(a_hat=-kk * jnp.exp2(gc - logw), k_hat=kmod * e_inv, b_hat=kka * e_inv, kbar=kmod * e_end,
                   bbar=kka * e_end, v=v, r_til=r * jnp.exp2(gc), bonus=bonus, g=g)
        rows = slice(chunk * c_len, (chunk + 1) * c_len)
        for n, name in enumerate(OPERANDS):
            ops_ref[rows, n * d_a:(n + 1) * d_a] = ops[name].astype(BF16)
        wend_ref[chunk] = jnp.exp2(gc_last)

    def ffn(out):
        x = x_ref[...]
        h = _rmsnorm(x, ln1_ref[...]).astype(BF16)
        g = _dot(h, wg_ref[...])
        yield
        u = _dot(h, wu_ref[...])
        yield
        act = (g * jax.nn.sigmoid(g) * u).astype(BF16)
        x1 = x + 0.5 * _dot(act, wd_ref[...])
        x1_ref[...] = x1
        yield
        h = _rmsnorm(x1, lnm_ref[...]).astype(BF16)
        a_cols = mu_ref.shape[1]
        out["pa"] = _dot(h, win_ref[:, :a_cols])
        yield
        pb_ref[...] = _dot(h, win_ref[:, a_cols:])

    if n_tiles == 1:
        out = {}
        _emit(ffn(out))
        pa = out["pa"]
        _emit(*[front_end(lambda c=c: pa[c * c_len:(c + 1) * c_len], lambda c=c: shift_ref[c], c)
                for c in range(n_chunks)])
        return

    @pl.when(step == 0)
    def _():
        pa_scr[...] = jnp.zeros(pa_scr.shape, F32)
        carry_ref[...] = shift_ref[0]

    def previous_tile():
        return [front_end(lambda c=c: pa_scr[c * c_len:(c + 1) * c_len, :],
                          lambda c=c: carry_ref[...] if c == 0 else pa_scr[c * c_len - 1:c * c_len, :], c)
                for c in range(n_chunks)]

    @pl.when(step < n_tiles)
    def _():
        out = {}
        _emit(*previous_tile(), ffn(out))
        for src, dst in zip(ride_in, ride_out):
            dst[...] = src[...].astype(BF16)
        carry_ref[...] = jnp.where(step > 0, pa_scr[tm - 1:tm, :], carry_ref[...])
        pa_scr[...] = out["pa"]

    @pl.when(step == n_tiles)
    def _():
        _emit(*previous_tile())


def _resident(shape):
    return pl.BlockSpec(shape, lambda *_: (0,) * len(shape), pipeline_mode=pl.Buffered(1))


def _token_tile(n, tile):
    tm = min(tile, n)
    assert n % tm == 0
    return tm


def _pre_call(x, shift, c_len, ride, ln1, wg, wu, wd, lnm, win, *front_consts):
    n, d = x.shape
    a_cols = front_consts[0].shape[1]
    b_cols = win.shape[1] - a_cols
    d_a = front_consts[1].shape[1]
    tm = _token_tile(n, PRE_TILE)
    n_tiles = n // tm
    n_chunks = tm // c_len
    assert (shift.shape[0] == 1) if n_tiles > 1 else (shift.shape[0] == n_chunks)
    lag = 1 if n_tiles > 1 else 0
    cur = lambda w: pl.BlockSpec((tm, w), lambda i: (jnp.minimum(i, n_tiles - 1), 0))
    late = lambda shp: pl.BlockSpec(shp, lambda i: (jnp.maximum(i - lag, 0),) + (0,) * (len(shp) - 1))
    consts = [ln1, wg, wu, wd, lnm, win, *front_consts]
    assert 2 + len(consts) == N_PRE_INPUTS
    def ride_spec(w):
        nb = max(k for k in range(1, n_tiles + 1) if w.shape[0] % (16 * k) == 0)
        return pl.BlockSpec((w.shape[0] // nb, w.shape[1]), lambda i: (jnp.minimum(i, nb - 1), 0))

    assert not ride or n_tiles > 1
    outs = pl.pallas_call(
        functools.partial(_pre_kernel, c_len, n_tiles, len(ride)),
        grid=(n_tiles + lag,),
        in_specs=[cur(d), _resident(shift.shape)] + [_resident(w.shape) for w in consts] + [ride_spec(w) for w in ride],
        out_specs=[cur(d), cur(b_cols), late((tm, len(OPERANDS) * d_a)), late((n_chunks, 1, d_a)),
                   late((n_chunks, 1, a_cols))] + [ride_spec(w) for w in ride],
        out_shape=[jax.ShapeDtypeStruct((n, d), F32), jax.ShapeDtypeStruct((n, b_cols), F32),
                   jax.ShapeDtypeStruct((n, len(OPERANDS) * d_a), BF16),
                   jax.ShapeDtypeStruct((n // c_len, 1, d_a), F32),
                   jax.ShapeDtypeStruct((n // c_len, 1, a_cols), F32)]
        + [jax.ShapeDtypeStruct(w.shape, BF16) for w in ride],
        scratch_shapes=[pltpu.VMEM((tm, a_cols), F32), pltpu.VMEM((1, a_cols), F32)],
        compiler_params=pltpu.CompilerParams(dimension_semantics=("arbitrary",), vmem_limit_bytes=VMEM_LIMIT),
        name="pre_ffn_inproj",
    )(x, shift, *consts, *ride)
    return outs[:N_PRE_OUTPUTS], outs[N_PRE_OUTPUTS:]


def _post_kernel(final_norm, x1_ref, o_ref, wo_ref, ln2_ref, wg_ref, wu_ref, wd_ref, lnf_ref, y_ref):
    x2 = x1_ref[...] + _dot(o_ref[...].astype(BF16), wo_ref[...])
    x3 = _swiglu_half_step(x2, ln2_ref[...], wg_ref, wu_ref, wd_ref)
    y_ref[...] = _rmsnorm(x3, lnf_ref[...]) if final_norm else x3


def _post_call(x1, o, wo, ln2, wg, wu, wd, lnf, final_norm):
    n, d = x1.shape
    tm = _token_tile(n, POST_TILE)
    tok = lambda w: pl.BlockSpec((tm, w), lambda i: (i, 0))
    consts = [wo, ln2, wg, wu, wd, lnf]
    return pl.pallas_call(
        functools.partial(_post_kernel, final_norm),
        grid=(n // tm,),
        in_specs=[tok(d), tok(o.shape[1])] + [_resident(w.shape) for w in consts],
        out_specs=tok(d),
        out_shape=jax.ShapeDtypeStruct((n, d), F32),
        compiler_params=pltpu.CompilerParams(dimension_semantics=("arbitrary",), vmem_limit_bytes=VMEM_LIMIT),
        name="post_outproj_ffn",
    )(x1, o, *consts)


def _mixer_kernel(c_len, n_sub, chained, ops_ref, wend_ref, pb_ref, wkv0_ref, gla0_ref, lnw_ref, lnb_ref, eseg_ref,
                  gkup_ref, gkb_ref, ng_ref, o_ref, wkv_ref, gla_ref, st_ref):
    step = pl.program_id(1)
    n_pairs = wkv_ref.shape[1] // 2
    d_a = n_pairs * LANE
    c2 = 2 * c_len
    cat = _cat

    def load_state(i, j):
        both = cat([wkv0_ref[i, 2 * j], wkv0_ref[i, 2 * j + 1]], 0)
        return jnp.transpose(cat([both, jnp.zeros((LANE, LANE - A_HEAD), F32)], 1))[:A_HEAD]

    def store_state(i, j, s):
        t = jnp.transpose(cat([s, jnp.zeros((LANE - A_HEAD, LANE), F32)], 0))
        wkv_ref[i, 2 * j] = t[:A_HEAD, :A_HEAD]
        wkv_ref[i, 2 * j + 1] = t[A_HEAD:, :A_HEAD]

    if chained:
        @pl.when(step == 0)
        def _():
            gla_ref[...] = gla0_ref[...]
            for j in range(n_pairs):
                st_ref[j] = load_state(0, j)

    eseg = eseg_ref[...]

    def operand(name, i, j=None):
        col = OPERANDS.index(name) * d_a
        cols = slice(col, col + d_a) if j is None else slice(col + j * LANE, col + (j + 1) * LANE)
        return ops_ref[0, i * c_len:(i + 1) * c_len, cols]

    rr, cc = _iota2(c_len, c2)
    col_t = cc & (c_len - 1)
    strict = col_t < rr
    lower = col_t <= rr
    eye_tt = jnp.where(col_t == rr, 1.0, 0.0).astype(F32)
    rk, ck = _iota2(A_HEAD, LANE)
    eye_kk = jnp.where((ck & (A_HEAD - 1)) == rk, 1.0, 0.0).astype(F32)
    half_k = lax.broadcasted_iota(jnp.int32, (1, LANE), 1) < A_HEAD
    half_t = lax.broadcasted_iota(jnp.int32, (1, c2), 1) < c_len
    n_double = int(math.log2(c_len))
    assert 2 ** n_double == c_len and n_double >= 2

    def bd(x, half):
        xb = x.astype(BF16)
        return cat([xb * jnp.where(half, 1.0, 0.0).astype(BF16), xb * jnp.where(half, 0.0, 1.0).astype(BF16)], 0)

    bd_k = lambda x: bd(x, half_k)
    bd_t = lambda x: bd(x, half_t)
    rq, cq = _iota2(LANE, LANE)
    same_head = jnp.where((rq < A_HEAD) == (cq < A_HEAD), 1.0, 0.0).astype(BF16)

    def bd_kt(x):
        return jnp.transpose(cat([x, x], 0)) * same_head
    st = [st_ref[j] for j in range(n_pairs)] if chained else None
    mixed = {}

    def mix(chunks):
        chains = [(i, j) for i in chunks for j in range(n_pairs)]
        each = lambda f: [f(c) for c in range(len(chains))]
        at = lambda name: [operand(name, i, j) for i, j in chains]
        a_t, r_t, k_t, b_t, v_t = at("a_hat"), at("r_til"), at("k_hat"), at("b_hat"), at("v")
        kbar_t, bbar_t = at("kbar"), at("bbar")
        w_end = [wend_ref[0, i, :, j * LANE:(j + 1) * LANE] for i, j in chains]
        if c2 == LANE:
            bd_b = each(lambda c: bd_kt(b_t[c]))
            bd_kh = each(lambda c: bd_kt(k_t[c]))
            xa = each(lambda c: _mm(a_t[c], cat([bd_b[c], bd_kh[c]], 1)))
            yield
            xr = each(lambda c: _mm(r_t[c], cat([bd_kh[c], bd_b[c]], 1)))
        else:
            bd_b = each(lambda c: bd_k(b_t[c]))
            bd_kh = each(lambda c: bd_k(k_t[c]))
            xa = each(lambda c: _mm(a_t[c], cat([bd_b[c], bd_kh[c]], 0), NT))
            yield
            xr = each(lambda c: _mm(r_t[c], cat([bd_kh[c], bd_b[c]], 0), NT))
        l_mat = each(lambda c: jnp.where(strict, xa[c][:, :c2], 0.0))
        a_ak = each(lambda c: jnp.where(strict, xa[c][:, c2:], 0.0))
        yield
        t_inv = each(lambda c: eye_tt + l_mat[c])
        pw = each(lambda c: _mm(l_mat[c], bd_t(l_mat[c])))
        a_rk_rb = each(lambda c: cat([jnp.where(lower, xr[c][:, :c2], 0.0), jnp.where(lower, xr[c][:, c2:], 0.0)], 1))
        yield
        for _ in range(1, n_double - 1):
            y = each(lambda c: _mm(pw[c], cat([bd_t(pw[c]), bd_t(t_inv[c])], 1)))
            t_inv = each(lambda c: t_inv[c] + y[c][:, c2:])
            pw = each(lambda c: y[c][:, :c2])
            yield
        t_inv = each(lambda c: t_inv[c] + _mm(pw[c], bd_t(t_inv[c])))
        bd_v = each(lambda c: bd_k(v_t[c]))
        a_ak_v = each(lambda c: _mm(a_ak[c], bd_v[c]))
        yield
        pq = each(lambda c: _mm(t_inv[c], cat([bd_k(a_t[c]), bd_k(a_ak_v[c])], 1)))
        p_t = each(lambda c: pq[c][:, :LANE])
        q_t = each(lambda c: pq[c][:, LANE:])
        yield
        hg = each(lambda c: _mm(a_rk_rb[c], cat([cat([bd_v[c], jnp.zeros((c2, LANE), BF16)], 1),
                                                 cat([bd_k(q_t[c]), bd_k(p_t[c])], 1)], 0)))
        yield
        nm = each(lambda c: _mm(cat([kbar_t[c], bbar_t[c]], 0),
                                cat([cat([v_t[c], jnp.zeros((c_len, LANE), BF16)], 1),
                                     cat([q_t[c], p_t[c]], 1).astype(BF16)], 0), TN))
        n_sbs = each(lambda c: jnp.where(half_k, nm[c][:A_HEAD, :LANE], nm[c][A_HEAD:, :LANE]))
        m_sbs = each(lambda c: jnp.where(half_k, nm[c][:A_HEAD, LANE:], nm[c][A_HEAD:, LANE:]) + eye_kk * w_end[c])
        g_pair = each(lambda c: r_t[c] + hg[c][:, LANE:])
        h_pair = each(lambda c: hg[c][:, :LANE])
        yield
        for n, i in enumerate(chunks):
            out_tiles = []
            for j in range(n_pairs):
                c = n * n_pairs + j
                both = _mm(cat([m_sbs[c], g_pair[c]], 0), bd_k(st[j] if chained else load_state(i, j)))
                out_tiles.append(both[A_HEAD:] + h_pair[c])
                if chained:
                    st[j] = both[:A_HEAD] + n_sbs[c]
                else:
                    store_state(i, j, both[:A_HEAD] + n_sbs[c])
            mixed[i] = cat(out_tiles, 1)
            yield

    def post(i):
        o = mixed[i]
        inv_n = 1.0 / A_HEAD
        mean = _head_sum(o, eseg) * inv_n
        cen = o - mean
        yield
        var = _head_sum(cen * cen, eseg) * inv_n
        on = cen * lax.rsqrt(var + LNX_EPS) * lnw_ref[...] + lnb_ref[...]
        yield
        o_ref[0, i * c_len:(i + 1) * c_len, 0:d_a] = (on + operand("bonus", i)) * operand("g", i)

    groups = [list(range(s, min(s + RWKV_CHUNKS_PER_GROUP, n_sub))) for s in range(0, n_sub, RWKV_CHUNKS_PER_GROUP)]
    for n, grp in enumerate(groups):
        gla = _gla_part(c_len, grp, chained, pb_ref, gla0_ref, gkup_ref, gkb_ref, ng_ref, o_ref, d_a, gla_ref)
        _emit(mix(grp), gla, *([post(i) for i in groups[n - 1]] if n > 0 else []))
    _emit(*[post(i) for i in groups[-1]])
    if chained:
        for j in range(n_pairs):
            st_ref[j] = st[j]

        @pl.when(step == pl.num_programs(1) - 1)
        def _():
            for j in range(n_pairs):
                store_state(0, j, st_ref[j])


def _mixer_call(ops, wend, pb, wkv0, gla0, c_len, lnw, lnb, eseg, gkup, gkb, ng):
    b, t, ops_cols = ops.shape
    d_a = wkv0.shape[1] * A_HEAD
    n_heads_b, dk, dv = gla0.shape[1:]
    gla_shape = (n_heads_b // 2, 2 * dk, dv)
    n_chunks = t // c_len
    chained = n_chunks > 1
    if chained:
        n_sub = math.gcd(n_chunks, MIXER_CHUNKS_PER_STEP)
        grid = (b, n_chunks // n_sub)
        per_seq = lambda shp: pl.BlockSpec((1,) + shp, lambda i, j: (i,) + (0,) * len(shp))
    else:
        n_sub = math.gcd(b, MIXER_SEQUENCES_PER_STEP)
        grid = (1, b // n_sub)
        ops, wend, pb = ops.reshape(1, b * t, ops_cols), wend.reshape(1, b, 1, d_a), pb.reshape(1, b * t, -1)
        per_seq = lambda shp: pl.BlockSpec((n_sub,) + shp, lambda i, j: (j,) + (0,) * len(shp))
    seq = lambda w: pl.BlockSpec((1, c_len * n_sub, w), lambda i, j: (i, j, 0))
    consts = [lnw, lnb, eseg, gkup, gkb, ng]
    d_o = d_a + n_heads_b * dv
    o, wkv, gla = pl.pallas_call(
        functools.partial(_mixer_kernel, c_len, n_sub, chained),
        grid=grid,
        in_specs=[seq(ops_cols), pl.BlockSpec((1, n_sub, 1, d_a), lambda i, j: (i, j, 0, 0)), seq(pb.shape[2]),
                  per_seq(wkv0.shape[1:]), per_seq(gla_shape)] + [_resident(w.shape) for w in consts],
        out_specs=[seq(d_o), per_seq(wkv0.shape[1:]), per_seq(gla_shape)],
        out_shape=[jax.ShapeDtypeStruct(ops.shape[:2] + (d_o,), F32), jax.ShapeDtypeStruct(wkv0.shape, F32),
                   jax.ShapeDtypeStruct((b,) + gla_shape, F32)],
        scratch_shapes=[pltpu.VMEM((d_a // LANE, A_HEAD, LANE), F32)],
        compiler_params=pltpu.CompilerParams(dimension_semantics=("arbitrary", "arbitrary"),
                                             vmem_limit_bytes=VMEM_LIMIT),
        name="mixers",
    )(ops, wend, pb, wkv0, gla0.reshape((b,) + gla_shape), *consts)
    return o.reshape(b, t, d_o), wkv, gla.reshape(gla0.shape)


def _gla_part(c_len, chunks, chained, pb_ref, gla0_ref, gkup_ref, gkb_ref, ng_ref, o_ref, o_col, st_ref):
    n_pairs = st_ref.shape[1]
    dk = st_ref.shape[2] // 2
    dv = st_ref.shape[3]
    assert 2 * dk == LANE and dv == LANE
    n_heads = 2 * n_pairs
    nk = n_heads * dk
    nv = n_heads * dv
    n_chunks = len(chunks)
    tb = c_len * n_chunks
    row0 = chunks[0] * c_len
    cat = _cat

    p = pb_ref[0, row0:row0 + tb, :]
    q = p[:, 0:nk] * (dk ** -0.5)
    k = p[:, nk:2 * nk]
    v = p[:, 2 * nk:2 * nk + nv]
    og = p[:, 2 * nk + nv:2 * nk + 2 * nv]
    xgk = p[:, 2 * nk + 2 * nv:2 * nk + 2 * nv + LANE]
    z = _dot(xgk.astype(BF16), gkup_ref[...]) + gkb_ref[...]
    log_a = -_softplus(-z) / B_TAU

    sub = min(SUB, c_len)
    n_sub = c_len // sub
    row, col = _iota2(tb, tb)
    in_chunk = col >= (row & -c_len)
    tri = jnp.where((col <= row) & in_chunk, 1.0, 0.0).astype(BF16)
    b = _dot_ones_lhs(tri, log_a)
    if n_sub > 1:
        tri_start = jnp.where((col < (row & -sub)) & in_chunk, 1.0, 0.0).astype(BF16)
        b_start = _dot_ones_lhs(tri_start, log_a)
    else:
        b_start = jnp.zeros_like(b)
    yield
    q_inter = q * jnp.exp(b)
    q_local = q * jnp.exp(b - b_start)
    rows_of = lambda x, i: x[i * c_len:(i + 1) * c_len]
    b_last = [b[(i + 1) * c_len - 1:(i + 1) * c_len, :] for i in range(n_chunks)]
    k_state = cat([rows_of(k, i) * jnp.exp(b_last[i] - rows_of(b, i)) for i in range(n_chunks)], 0)
    row1 = lax.broadcasted_iota(jnp.int32, (c_len, 1), 0)
    k_local = []
    for i in range(n_chunks):
        per_sub = []
        for s in range(n_sub):
            ref = b_start[i * c_len + s * sub:i * c_len + s * sub + 1, :]
            expo = jnp.where(row1 < (s + 1) * sub, ref - rows_of(b, i), -1e30)
            per_sub.append(rows_of(k, i) * jnp.exp(expo))
        k_local.append(per_sub)

    half0 = lax.broadcasted_iota(jnp.int32, (1, LANE), 1) < dk
    rc, cc = _iota2(c_len, c_len)
    causal = rc >= cc
    top_rows = lax.broadcasted_iota(jnp.int32, (LANE, 1), 0) < dk
    kt = lambda x, i, j: x[i * c_len:(i + 1) * c_len, j * LANE:(j + 1) * LANE]
    vt = lambda x, i, h: x[i * c_len:(i + 1) * c_len, h * dv:(h + 1) * dv]
    pairs = [(i, j) for i in range(n_chunks) for j in range(n_pairs)]
    heads = [(i, h) for i in range(n_chunks) for h in range(n_heads)]

    yield

    def a_rows(i, j, s):
        ql = kt(q_local, i, j)[s * sub:(s + 1) * sub]
        lhs = cat([jnp.where(half0, ql, 0.0), jnp.where(half0, 0.0, ql)], 0)
        return _mm(lhs, kt(k_local[i][s], 0, j), NT)
    a_blk = {(i, j): [a_rows(i, j, s) for s in range(n_sub)] for i, j in pairs}
    yield
    a_mat = {}
    for i, h in heads:
        blocks = [a_blk[(i, h // 2)][s][(h % 2) * sub:(h % 2 + 1) * sub] for s in range(n_sub)]
        a_mat[(i, h)] = jnp.where(causal, cat(blocks, 0) if n_sub > 1 else blocks[0], 0.0)
    o_intra = {(i, h): _mm(a_mat[(i, h)], vt(v, i, h)) for i, h in heads}
    yield
    upd = {}
    dec = {}
    for i, j in pairs:
        kv = _mm(kt(k_state, i, j), v[i * c_len:(i + 1) * c_len, 2 * j * dv:(2 * j + 2) * dv], TN)
        upd[(i, j)] = jnp.where(top_rows, kv[:, :dv], kv[:, dv:])
        dec[(i, j)] = jnp.exp(jnp.transpose(jnp.broadcast_to(kt(b_last[i], 0, j)[0:1], (LANE, LANE))))
    yield
    states = {}
    for j in range(n_pairs):
        if chained:
            s_cur = st_ref[0, j]
            for i in range(n_chunks):
                states[(i, j)] = s_cur
                s_cur = dec[(i, j)] * s_cur + upd[(i, j)]
            st_ref[0, j] = s_cur
        else:
            for i in range(n_chunks):
                states[(i, j)] = gla0_ref[chunks[i], j]
                st_ref[chunks[i], j] = dec[(i, j)] * states[(i, j)] + upd[(i, j)]
    for i in range(n_chunks):
        outs = []
        for h in range(n_heads):
            qi = kt(q_inter, i, h // 2)
            qi = jnp.where(half0, qi, 0.0) if h % 2 == 0 else jnp.where(half0, 0.0, qi)
            o_h = o_intra[(i, h)] + _mm(qi, states[(i, h // 2)])
            o_h = o_h * lax.rsqrt(jnp.mean(o_h * o_h, axis=-1, keepdims=True) + EPS) * ng_ref[...]
            og_h = vt(og, i, h)
            outs.append(o_h * (og_h * jax.nn.sigmoid(og_h)))
        o_ref[0, row0 + i * c_len:row0 + (i + 1) * c_len, o_col:o_col + nv] = cat(outs, 1)
        yield


def _chunk_len(t):
    return GLA_CHUNK if t % GLA_CHUNK == 0 else t


def _pad_rows(w, start, total):
    return jnp.zeros((total, w.shape[1]), w.dtype).at[start:start + w.shape[0]].set(w)


def _prep_layer(l, ln1_g, ffn1_wg, ffn1_wu, ffn1_wd, ln_mix_g, w_in, mu_shift, w0, w_lora_up, a0, a_lora_up,
                g_lora_up, k_k, k_a, r_k, lnx_w, lnx_b, gk_up, gk_b, gla_norm_g, w_out, ln2_g, ffn2_wg, ffn2_wu,
                ffn2_wd):
    d_a = w0.shape[1]
    a_cols = mu_shift.shape[1]
    nk = gk_b.shape[1]
    d_b = w_out.shape[1] - d_a
    row = lambda x: x[l].reshape(1, -1)
    wi = w_in[l]
    n_qkv = a_cols + 2 * nk + d_b
    pad = jnp.zeros((wi.shape[0], LANE - B_GATE_RANK), wi.dtype)
    win = jnp.concatenate([wi[:, :n_qkv], wi[:, n_qkv + B_GATE_RANK:], wi[:, n_qkv:n_qkv + B_GATE_RANK], pad],
                          axis=1).astype(BF16)
    eseg = jnp.kron(jnp.eye(HEAD_SUM_TILE // A_HEAD, dtype=F32), jnp.ones((A_HEAD, A_HEAD), F32)).astype(BF16)
    return dict(
        pre=(row(ln1_g), ffn1_wg[l].astype(BF16), ffn1_wu[l].astype(BF16), ffn1_wd[l].astype(BF16), row(ln_mix_g),
             win, row(mu_shift), row(w0), _pad_rows(w_lora_up[l], 0, LANE).astype(BF16), row(a0),
             _pad_rows(a_lora_up[l], A_W_RANK, LANE).astype(BF16),
             _pad_rows(g_lora_up[l], A_W_RANK + A_A_RANK, LANE).astype(BF16), row(k_k), row(k_a), row(r_k), eseg),
        rwkv=(row(lnx_w), row(lnx_b), eseg),
        gla=(_pad_rows(gk_up[l], 0, LANE).astype(BF16), row(gk_b), row(gla_norm_g)),
        post_weights=(w_out[l], ffn2_wg[l], ffn2_wu[l], ffn2_wd[l]),
        ln2=row(ln2_g),
    )


def _trunk(x, shift, wkv, gla, layers, ln_f, post_bf16):
    b, t, d = x.shape
    c_len = _chunk_len(t)
    xf = x.reshape(b * t, d)
    new_shift, new_wkv, new_gla = [], [], []
    for l, lw in enumerate(layers):
        ride = lw["post_weights"] if post_bf16[l] is None else ()
        (x1, pb, ops, wend, last), cast = _pre_call(xf, shift[l], c_len, ride, *lw["pre"])
        if post_bf16[l] is None:
            post_bf16[l] = cast
        wo, wg, wu, wd = post_bf16[l]
        o, s_a, s_b = _mixer_call(ops.reshape(b, t, -1), wend.reshape(b, t // c_len, 1, -1), pb.reshape(b, t, -1),
                                  wkv[l], gla[l], c_len, *lw["rwkv"], *lw["gla"])
        xf = _post_call(x1, o.reshape(b * t, -1), wo, lw["ln2"], wg, wu, wd, ln_f, final_norm=(l == len(layers) - 1))
        new_shift.append(last.reshape(b, t // c_len, 1, -1)[:, -1])
        new_wkv.append(s_a)
        new_gla.append(s_b)
    return xf.reshape(b, t, d), jnp.stack(new_shift), jnp.stack(new_wkv), jnp.stack(new_gla)


def kernel(x_prompt, x_sample, state_shift, state_wkv, state_gla, ln1_g, ffn1_wg, ffn1_wu, ffn1_wd, ln_mix_g, w_in,
           mu_shift, w0, w_lora_up, a0, a_lora_up, g_lora_up, k_k, k_a, r_k, lnx_w, lnx_b, gk_up, gk_b, gla_norm_g,
           w_out, ln2_g, ffn2_wg, ffn2_wu, ffn2_wd, ln_f_g):
    depth = ln1_g.shape[0]
    per_layer = (ln1_g, ffn1_wg, ffn1_wu, ffn1_wd, ln_mix_g, w_in, mu_shift, w0, w_lora_up, a0, a_lora_up, g_lora_up,
                 k_k, k_a, r_k.reshape(depth, -1), lnx_w, lnx_b, gk_up, gk_b, gla_norm_g, w_out, ln2_g, ffn2_wg,
                 ffn2_wu, ffn2_wd)
    layers = [_prep_layer(l, *per_layer) for l in range(depth)]
    ln_f = ln_f_g.reshape(1, -1)
    bp = x_prompt.shape[0]
    shift0 = jnp.zeros((depth, bp) + state_shift.shape[2:], F32)
    wkv0 = jnp.zeros((depth, bp) + state_wkv.shape[2:], F32)
    gla0 = jnp.zeros((depth, bp) + state_gla.shape[2:], F32)
    post_bf16 = [None] * depth
    y_p, shift_p, wkv_p, gla_p = _trunk(x_prompt, shift0, wkv0, gla0, layers, ln_f, post_bf16)
    y_s, shift_s, wkv_s, gla_s = _trunk(x_sample, state_shift, state_wkv, state_gla, layers, ln_f, post_bf16)
    return (y_p, y_s, shift_p, wkv_p, gla_p, shift_s, wkv_s, gla_s)
```

```python
import functools
import math

import jax
import jax.numpy as jnp
from jax import lax
from jax.experimental import pallas as pl
from jax.experimental.pallas import tpu as pltpu

F32 = jnp.float32
BF16 = jnp.bfloat16

EPS = 1e-6
LNX_EPS = 64e-5
A_HEAD = 64
A_W_RANK = 32
A_A_RANK = 32
B_GATE_RANK = 16
B_TAU = 16.0
GLA_CHUNK = 64
SUB = 16
LANE = 128
VMEM_LIMIT = 56 * 1024 * 1024
PRE_TILE = 256
POST_TILE = 512
MIXER_CHUNKS_PER_STEP = 16
MIXER_SEQUENCES_PER_STEP = 16
RWKV_CHUNKS_PER_GROUP = 8

NN = ((1,), (0,))
NT = ((1,), (1,))
TN = ((0,), (0,))


def _dot(a, b, precision=None):
    return jnp.dot(a, b, preferred_element_type=F32, precision=precision)


def _split2(x):
    hi = x.astype(BF16)
    return hi, (x - hi.astype(F32)).astype(BF16)


def _mm(a, b, dims=NN):
    return lax.dot_general(a.astype(BF16), b.astype(BF16), (dims, ((), ())), preferred_element_type=F32)


def _dot_ones_lhs(ones_bf16, x):
    hi, lo = _split2(x)
    return _dot(ones_bf16, hi) + _dot(ones_bf16, lo)


def _softplus(x):
    return jnp.maximum(x, 0.0) + jnp.log(1.0 + jnp.exp(-jnp.abs(x)))


def _rmsnorm(x, g):
    return x * lax.rsqrt(jnp.mean(x * x, axis=-1, keepdims=True) + EPS) * g


def _swiglu_half_step(x, ln_g, wg_ref, wu_ref, wd_ref):
    h = _rmsnorm(x, ln_g).astype(BF16)
    g = _dot(h, wg_ref[...])
    u = _dot(h, wu_ref[...])
    act = (g * jax.nn.sigmoid(g) * u).astype(BF16)
    return x + 0.5 * _dot(act, wd_ref[...])


def _iota2(n, m):
    return lax.broadcasted_iota(jnp.int32, (n, m), 0), lax.broadcasted_iota(jnp.int32, (n, m), 1)


def _cat(xs, axis):
    return jnp.concatenate(xs, axis=axis)


def _emit(*tasks):
    tasks = list(tasks)
    while tasks:
        for task in list(tasks):
            if next(task, True):
                tasks.remove(task)


def _head_sum_lanes(x):
    first = lax.broadcasted_iota(jnp.int32, (1, LANE), 1) < A_HEAD
    tiles = []
    for s in range(0, x.shape[1], LANE):
        xt = x[:, s:s + LANE]
        even = jnp.sum(jnp.where(first, xt, 0.0), axis=-1, keepdims=True)
        odd = jnp.sum(jnp.where(first, 0.0, xt), axis=-1, keepdims=True)
        tiles.append(jnp.where(first, even, odd))
    return _cat(tiles, 1)


OPERANDS = ("a_hat", "k_hat", "b_hat", "kbar", "bbar", "v", "r_til", "bonus", "g")


N_PRE_INPUTS = 17
N_PRE_OUTPUTS = 5


def _pre_kernel(c_len, n_tiles, n_ride, *refs):
    (x_ref, shift_ref, ln1_ref, wg_ref, wu_ref, wd_ref, lnm_ref, win_ref, mu_ref, w0_ref, wup_ref, a0_ref, aup_ref,
     gup_ref, kk_ref, ka_ref, rk_ref) = refs[:N_PRE_INPUTS]
    ride_in = refs[N_PRE_INPUTS:N_PRE_INPUTS + n_ride]
    outs = refs[N_PRE_INPUTS + n_ride:]
    x1_ref, pb_ref, ops_ref, wend_ref, last_ref = outs[:N_PRE_OUTPUTS]
    ride_out = outs[N_PRE_OUTPUTS:N_PRE_OUTPUTS + n_ride]
    pa_scr, carry_ref = outs[N_PRE_OUTPUTS + n_ride:]
    step = pl.program_id(0)
    tm = x_ref.shape[0]
    d_a = w0_ref.shape[1]
    n_chunks = tm // c_len
    row1 = lax.broadcasted_iota(jnp.int32, (c_len, 1), 0)
    rt, ct = _iota2(c_len, c_len)
    tri = jnp.where(ct <= rt, 1.0, 0.0).astype(BF16)

    def front_end(load_p, load_prev, chunk):
        p = load_p()
        shifted = jnp.where(row1 == 0, load_prev(), pltpu.roll(p, 1, axis=0))
        last_ref[chunk] = p[c_len - 1:c_len, :]
        xs = p + (shifted - p) * mu_ref[...]
        r = xs[:, 0:d_a]
        k = xs[:, d_a:2 * d_a]
        v = xs[:, 2 * d_a:3 * d_a]
        lora_in = xs[:, 3 * d_a:3 * d_a + LANE]
        lw = _dot(jnp.tanh(lora_in).astype(BF16), wup_ref[...])
        la = _dot(lora_in.astype(BF16), aup_ref[...])
        g = _dot(jax.nn.sigmoid(lora_in).astype(BF16), gup_ref[...])
        yield
        kk = k * kk_ref[...]
        kk_sq = _head_sum_lanes(kk * kk)
        yield
        logw = (-math.exp(-0.5) * math.log2(math.e)) * jax.nn.sigmoid(w0_ref[...] + lw)
        gc = _dot_ones_lhs(tri, logw)
        yield
        a = jax.nn.sigmoid(a0_ref[...] + la)
        kk = kk * lax.rsqrt(jnp.maximum(kk_sq, 1e-24))
        kmod = k * (1.0 + (a - 1.0) * ka_ref[...])
        kka = kk * a
        bonus = _head_sum_lanes(r * kmod * rk_ref[...]) * v
        yield
        gc_last = gc[c_len - 1:c_len, :]
        e_inv = jnp.exp2(-gc)
        e_end = jnp.exp2(gc_last - gc)
        ops = dict(a_hat=-kk * jnp.exp2(gc - logw), k_hat=kmod * e_inv, b_hat=kka * e_inv, kbar=kmod * e_end,
                   bbar=kka * e_end, v=v, r_til=r * jnp.exp2(gc), bonus=bonus, g=g)
        rows = slice(chunk * c_len, (chunk + 1) * c_len)
        for n, name in enumerate(OPERANDS):
            ops_ref[rows, n * d_a:(n + 1) * d_a] = ops[name].astype(BF16)
        wend_ref[chunk] = jnp.exp2(gc_last)

    def ffn(out):
        x = x_ref[...]
        h = _rmsnorm(x, ln1_ref[...]).astype(BF16)
        g = _dot(h, wg_ref[...])
        yield
        u = _dot(h, wu_ref[...])
        yield
        act = (g * jax.nn.sigmoid(g) * u).astype(BF16)
        x1 = x + 0.5 * _dot(act, wd_ref[...])
        x1_ref[...] = x1
        yield
        h = _rmsnorm(x1, lnm_ref[...]).astype(BF16)
        a_cols = mu_ref.shape[1]
        out["pa"] = _dot(h, win_ref[:, :a_cols])
        yield
        pb_ref[...] = _dot(h, win_ref[:, a_cols:])

    if n_tiles == 1:
        out = {}
        _emit(ffn(out))
        pa = out["pa"]
        _emit(*[front_end(lambda c=c: pa[c * c_len:(c + 1) * c_len], lambda c=c: shift_ref[c], c)
                for c in range(n_chunks)])
        return

    @pl.when(step == 0)
    def _():
        pa_scr[...] = jnp.zeros(pa_scr.shape, F32)
        carry_ref[...] = shift_ref[0]

    def previous_tile():
        return [front_end(lambda c=c: pa_scr[c * c_len:(c + 1) * c_len, :],
                          lambda c=c: carry_ref[...] if c == 0 else pa_scr[c * c_len - 1:c * c_len, :], c)
                for c in range(n_chunks)]

    @pl.when(step < n_tiles)
    def _():
        out = {}
        _emit(*previous_tile(), ffn(out))
        for src, dst in zip(ride_in, ride_out):
            dst[...] = src[...].astype(BF16)
        carry_ref[...] = jnp.where(step > 0, pa_scr[tm - 1:tm, :], carry_ref[...])
        pa_scr[...] = out["pa"]

    @pl.when(step == n_tiles)
    def _():
        _emit(*previous_tile())


def _resident(shape):
    return pl.BlockSpec(shape, lambda *_: (0,) * len(shape), pipeline_mode=pl.Buffered(1))


def _token_tile(n, tile):
    tm = min(tile, n)
    assert n % tm == 0
    return tm


def _pre_call(x, shift, c_len, ride, ln1, wg, wu, wd, lnm, win, *front_consts):
    n, d = x.shape
    a_cols = front_consts[0].shape[1]
    b_cols = win.shape[1] - a_cols
    d_a = front_consts[1].shape[1]
    tm = _token_tile(n, PRE_TILE)
    n_tiles = n // tm
    n_chunks = tm // c_len
    assert (shift.shape[0] == 1) if n_tiles > 1 else (shift.shape[0] == n_chunks)
    lag = 1 if n_tiles > 1 else 0
    cur = lambda w: pl.BlockSpec((tm, w), lambda i: (jnp.minimum(i, n_tiles - 1), 0))
    late = lambda shp: pl.BlockSpec(shp, lambda i: (jnp.maximum(i - lag, 0),) + (0,) * (len(shp) - 1))
    consts = [ln1, wg, wu, wd, lnm, win, *front_consts]
    assert 2 + len(consts) == N_PRE_INPUTS
    def ride_spec(w):
        nb = max(k for k in range(1, n_tiles + 1) if w.shape[0] % (16 * k) == 0)
        return pl.BlockSpec((w.shape[0] // nb, w.shape[1]), lambda i: (jnp.minimum(i, nb - 1), 0))

    assert not ride or n_tiles > 1
    outs = pl.pallas_call(
        functools.partial(_pre_kernel, c_len, n_tiles, len(ride)),
        grid=(n_tiles + lag,),
        in_specs=[cur(d), _resident(shift.shape)] + [_resident(w.shape) for w in consts] + [ride_spec(w) for w in ride],
        out_specs=[cur(d), cur(b_cols), late((tm, len(OPERANDS) * d_a)), late((n_chunks, 1, d_a)),
                   late((n_chunks, 1, a_cols))] + [ride_spec(w) for w in ride],
        out_shape=[jax.ShapeDtypeStruct((n, d), F32), jax.ShapeDtypeStruct((n, b_cols), F32),
                   jax.ShapeDtypeStruct((n, len(OPERANDS) * d_a), BF16),
                   jax.ShapeDtypeStruct((n // c_len, 1, d_a), F32),
                   jax.ShapeDtypeStruct((n // c_len, 1, a_cols), F32)]
        + [jax.ShapeDtypeStruct(w.shape, BF16) for w in ride],
        scratch_shapes=[pltpu.VMEM((tm, a_cols), F32), pltpu.VMEM((1, a_cols), F32)],
        compiler_params=pltpu.CompilerParams(dimension_semantics=("arbitrary",), vmem_limit_bytes=VMEM_LIMIT),
        name="pre_ffn_inproj",
    )(x, shift, *consts, *ride)
    return outs[:N_PRE_OUTPUTS], outs[N_PRE_OUTPUTS:]


def _post_kernel(final_norm, x1_ref, o_ref, wo_ref, ln2_ref, wg_ref, wu_ref, wd_ref, lnf_ref, y_ref):
    x2 = x1_ref[...] + _dot(o_ref[...].astype(BF16), wo_ref[...])
    x3 = _swiglu_half_step(x2, ln2_ref[...], wg_ref, wu_ref, wd_ref)
    y_ref[...] = _rmsnorm(x3, lnf_ref[...]) if final_norm else x3


def _post_call(x1, o, wo, ln2, wg, wu, wd, lnf, final_norm):
    n, d = x1.shape
    tm = _token_tile(n, POST_TILE)
    tok = lambda w: pl.BlockSpec((tm, w), lambda i: (i, 0))
    consts = [wo, ln2, wg, wu, wd, lnf]
    return pl.pallas_call(
        functools.partial(_post_kernel, final_norm),
        grid=(n // tm,),
        in_specs=[tok(d), tok(o.shape[1])] + [_resident(w.shape) for w in consts],
        out_specs=tok(d),
        out_shape=jax.ShapeDtypeStruct((n, d), F32),
        compiler_params=pltpu.CompilerParams(dimension_semantics=("arbitrary",), vmem_limit_bytes=VMEM_LIMIT),
        name="post_outproj_ffn",
    )(x1, o, *consts)


def _mixer_kernel(c_len, n_sub, chained, ops_ref, wend_ref, pb_ref, wkv0_ref, gla0_ref, lnw_ref, lnb_ref,
                  gkup_ref, gkb_ref, ng_ref, o_ref, wkv_ref, gla_ref, st_ref):
    step = pl.program_id(1)
    n_pairs = wkv_ref.shape[1] // 2
    d_a = n_pairs * LANE
    c2 = 2 * c_len
    cat = _cat

    def load_state(i, j):
        both = cat([wkv0_ref[i, 2 * j], wkv0_ref[i, 2 * j + 1]], 0)
        return jnp.transpose(cat([both, jnp.zeros((LANE, LANE - A_HEAD), F32)], 1))[:A_HEAD]

    def store_state(i, j, s):
        t = jnp.transpose(cat([s, jnp.zeros((LANE - A_HEAD, LANE), F32)], 0))
        wkv_ref[i, 2 * j] = t[:A_HEAD, :A_HEAD]
        wkv_ref[i, 2 * j + 1] = t[A_HEAD:, :A_HEAD]

    if chained:
        @pl.when(step == 0)
        def _():
            gla_ref[...] = gla0_ref[...]
            for j in range(n_pairs):
                st_ref[j] = load_state(0, j)

    def operand(name, i, j=None):
        col = OPERANDS.index(name) * d_a
        cols = slice(col, col + d_a) if j is None else slice(col + j * LANE, col + (j + 1) * LANE)
        return ops_ref[0, i * c_len:(i + 1) * c_len, cols]

    rr, cc = _iota2(c_len, c2)
    col_t = cc & (c_len - 1)
    strict = col_t < rr
    lower = col_t <= rr
    eye_tt = jnp.where(col_t == rr, 1.0, 0.0).astype(F32)
    rk, ck = _iota2(A_HEAD, LANE)
    eye_kk = jnp.where((ck & (A_HEAD - 1)) == rk, 1.0, 0.0).astype(F32)
    half_k = lax.broadcasted_iota(jnp.int32, (1, LANE), 1) < A_HEAD
    half_t = lax.broadcasted_iota(jnp.int32, (1, c2), 1) < c_len
    n_double = int(math.log2(c_len))
    assert 2 ** n_double == c_len and n_double >= 2

    def bd(x, half):
        xb = x.astype(BF16)
        return cat([xb * jnp.where(half, 1.0, 0.0).astype(BF16), xb * jnp.where(half, 0.0, 1.0).astype(BF16)], 0)

    bd_k = lambda x: bd(x, half_k)
    bd_t = lambda x: bd(x, half_t)
    rq, cq = _iota2(LANE, LANE)
    same_head = jnp.where((rq < A_HEAD) == (cq < A_HEAD), 1.0, 0.0).astype(BF16)

    def bd_kt(x):
        return jnp.transpose(cat([x, x], 0)) * same_head
    st = [st_ref[j] for j in range(n_pairs)] if chained else None
    mixed = {}

    def mix(chunks):
        chains = [(i, j) for i in chunks for j in range(n_pairs)]
        each = lambda f: [f(c) for c in range(len(chains))]
        at = lambda name: [operand(name, i, j) for i, j in chains]
        a_t, r_t, k_t, b_t, v_t = at("a_hat"), at("r_til"), at("k_hat"), at("b_hat"), at("v")
        kbar_t, bbar_t = at("kbar"), at("bbar")
        w_end = [wend_ref[0, i, :, j * LANE:(j + 1) * LANE] for i, j in chains]
        if c2 == LANE:
            bd_b = each(lambda c: bd_kt(b_t[c]))
            bd_kh = each(lambda c: bd_kt(k_t[c]))
            xa = each(lambda c: _mm(a_t[c], cat([bd_b[c], bd_kh[c]], 1)))
            yield
            xr = each(lambda c: _mm(r_t[c], cat([bd_kh[c], bd_b[c]], 1)))
        else:
            bd_b = each(lambda c: bd_k(b_t[c]))
            bd_kh = each(lambda c: bd_k(k_t[c]))
            xa = each(lambda c: _mm(a_t[c], cat([bd_b[c], bd_kh[c]], 0), NT))
            yield
            xr = each(lambda c: _mm(r_t[c], cat([bd_kh[c], bd_b[c]], 0), NT))
        l_mat = each(lambda c: jnp.where(strict, xa[c][:, :c2], 0.0))
        a_ak = each(lambda c: jnp.where(strict, xa[c][:, c2:], 0.0))
        yield
        t_inv = each(lambda c: eye_tt + l_mat[c])
        pw = each(lambda c: _mm(l_mat[c], bd_t(l_mat[c])))
        a_rk_rb = each(lambda c: cat([jnp.where(lower, xr[c][:, :c2], 0.0), jnp.where(lower, xr[c][:, c2:], 0.0)], 1))
        yield
        for _ in range(1, n_double - 1):
            y = each(lambda c: _mm(pw[c], cat([bd_t(pw[c]), bd_t(t_inv[c])], 1)))
            t_inv = each(lambda c: t_inv[c] + y[c][:, c2:])
            pw = each(lambda c: y[c][:, :c2])
            yield
        t_inv = each(lambda c: t_inv[c] + _mm(pw[c], bd_t(t_inv[c])))
        bd_v = each(lambda c: bd_k(v_t[c]))
        a_ak_v = each(lambda c: _mm(a_ak[c], bd_v[c]))
        yield
        pq = each(lambda c: _mm(t_inv[c], cat([bd_k(a_t[c]), bd_k(a_ak_v[c])], 1)))
        p_t = each(lambda c: pq[c][:, :LANE])
        q_t = each(lambda c: pq[c][:, LANE:])
        yield
        hg = each(lambda c: _mm(a_rk_rb[c], cat([cat([bd_v[c], jnp.zeros((c2, LANE), BF16)], 1),
                                                 cat([bd_k(q_t[c]), bd_k(p_t[c])], 1)], 0)))
        yield
        nm = each(lambda c: _mm(cat([kbar_t[c], bbar_t[c]], 0),
                                cat([cat([v_t[c], jnp.zeros((c_len, LANE), BF16)], 1),
                                     cat([q_t[c], p_t[c]], 1).astype(BF16)], 0), TN))
        n_sbs = each(lambda c: jnp.where(half_k, nm[c][:A_HEAD, :LANE], nm[c][A_HEAD:, :LANE]))
        m_sbs = each(lambda c: jnp.where(half_k, nm[c][:A_HEAD, LANE:], nm[c][A_HEAD:, LANE:]) + eye_kk * w_end[c])
        g_pair = each(lambda c: r_t[c] + hg[c][:, LANE:])
        h_pair = each(lambda c: hg[c][:, :LANE])
        yield
        for n, i in enumerate(chunks):
            out_tiles = []
            for j in range(n_pairs):
                c = n * n_pairs + j
                both = _mm(cat([m_sbs[c], g_pair[c]], 0), bd_k(st[j] if chained else load_state(i, j)))
                out_tiles.append(both[A_HEAD:] + h_pair[c])
                if chained:
                    st[j] = both[:A_HEAD] + n_sbs[c]
                else:
                    store_state(i, j, both[:A_HEAD] + n_sbs[c])
            mixed[i] = cat(out_tiles, 1)
            yield

    def post(i):
        o = mixed[i]
        inv_n = 1.0 / A_HEAD
        mean = _head_sum_lanes(o) * inv_n
        cen = o - mean
        yield
        var = _head_sum_lanes(cen * cen) * inv_n
        on = cen * lax.rsqrt(var + LNX_EPS) * lnw_ref[...] + lnb_ref[...]
        yield
        o_ref[0, i * c_len:(i + 1) * c_len, 0:d_a] = (on + operand("bonus", i)) * operand("g", i)

    groups = [list(range(s, min(s + RWKV_CHUNKS_PER_GROUP, n_sub))) for s in range(0, n_sub, RWKV_CHUNKS_PER_GROUP)]
    for n, grp in enumerate(groups):
        gla = _gla_part(c_len, grp, chained, pb_ref, gla0_ref, gkup_ref, gkb_ref, ng_ref, o_ref, d_a, gla_ref)
        _emit(mix(grp), gla, *([post(i) for i in groups[n - 1]] if n > 0 else []))
    _emit(*[post(i) for i in groups[-1]])
    if chained:
        for j in range(n_pairs):
            st_ref[j] = st[j]

        @pl.when(step == pl.num_programs(1) - 1)
        def _():
            for j in range(n_pairs):
                store_state(0, j, st_ref[j])


def _mixer_call(ops, wend, pb, wkv0, gla0, c_len, lnw, lnb, gkup, gkb, ng):
    b, t, ops_cols = ops.shape
    d_a = wkv0.shape[1] * A_HEAD
    n_heads_b, dk, dv = gla0.shape[1:]
    gla_shape = (n_heads_b // 2, 2 * dk, dv)
    n_chunks = t // c_len
    chained = n_chunks > 1
    if chained:
        n_sub = math.gcd(n_chunks, MIXER_CHUNKS_PER_STEP)
        grid = (b, n_chunks // n_sub)
        per_seq = lambda shp: pl.BlockSpec((1,) + shp, lambda i, j: (i,) + (0,) * len(shp))
    else:
        n_sub = math.gcd(b, MIXER_SEQUENCES_PER_STEP)
        grid = (1, b // n_sub)
        ops, wend, pb = ops.reshape(1, b * t, ops_cols), wend.reshape(1, b, 1, d_a), pb.reshape(1, b * t, -1)
        per_seq = lambda shp: pl.BlockSpec((n_sub,) + shp, lambda i, j: (j,) + (0,) * len(shp))
    seq = lambda w: pl.BlockSpec((1, c_len * n_sub, w), lambda i, j: (i, j, 0))
    consts = [lnw, lnb, gkup, gkb, ng]
    d_o = d_a + n_heads_b * dv
    o, wkv, gla = pl.pallas_call(
        functools.partial(_mixer_kernel, c_len, n_sub, chained),
        grid=grid,
        in_specs=[seq(ops_cols), pl.BlockSpec((1, n_sub, 1, d_a), lambda i, j: (i, j, 0, 0)), seq(pb.shape[2]),
                  per_seq(wkv0.shape[1:]), per_seq(gla_shape)] + [_resident(w.shape) for w in consts],
        out_specs=[seq(d_o), per_seq(wkv0.shape[1:]), per_seq(gla_shape)],
        out_shape=[jax.ShapeDtypeStruct(ops.shape[:2] + (d_o,), F32), jax.ShapeDtypeStruct(wkv0.shape, F32),
                   jax.ShapeDtypeStruct((b,) + gla_shape, F32)],
        scratch_shapes=[pltpu.VMEM((d_a // LANE, A_HEAD, LANE), F32)],
        compiler_params=pltpu.CompilerParams(dimension_semantics=("arbitrary", "arbitrary"),
                                             vmem_limit_bytes=VMEM_LIMIT),
        name="mixers",
    )(ops, wend, pb, wkv0, gla0.reshape((b,) + gla_shape), *consts)
    return o.reshape(b, t, d_o), wkv, gla.reshape(gla0.shape)


def _gla_part(c_len, chunks, chained, pb_ref, gla0_ref, gkup_ref, gkb_ref, ng_ref, o_ref, o_col, st_ref):
    n_pairs = st_ref.shape[1]
    dk = st_ref.shape[2] // 2
    dv = st_ref.shape[3]
    assert 2 * dk == LANE and dv == LANE
    n_heads = 2 * n_pairs
    nk = n_heads * dk
    nv = n_heads * dv
    n_chunks = len(chunks)
    tb = c_len * n_chunks
    row0 = chunks[0] * c_len
    cat = _cat

    p = pb_ref[0, row0:row0 + tb, :]
    q = p[:, 0:nk] * (dk ** -0.5)
    k = p[:, nk:2 * nk]
    v = p[:, 2 * nk:2 * nk + nv]
    og = p[:, 2 * nk + nv:2 * nk + 2 * nv]
    xgk = p[:, 2 * nk + 2 * nv:2 * nk + 2 * nv + LANE]
    z = _dot(xgk.astype(BF16), gkup_ref[...]) + gkb_ref[...]
    log_a = -_softplus(-z) / B_TAU

    sub = min(SUB, c_len)
    n_sub = c_len // sub
    row, col = _iota2(tb, tb)
    in_chunk = col >= (row & -c_len)
    tri = jnp.where((col <= row) & in_chunk, 1.0, 0.0).astype(BF16)
    b = _dot_ones_lhs(tri, log_a)
    if n_sub > 1:
        tri_start = jnp.where((col < (row & -sub)) & in_chunk, 1.0, 0.0).astype(BF16)
        b_start = _dot_ones_lhs(tri_start, log_a)
    else:
        b_start = jnp.zeros_like(b)
    yield
    q_inter = q * jnp.exp(b)
    q_local = q * jnp.exp(b - b_start)
    rows_of = lambda x, i: x[i * c_len:(i + 1) * c_len]
    b_last = [b[(i + 1) * c_len - 1:(i + 1) * c_len, :] for i in range(n_chunks)]
    k_state = cat([rows_of(k, i) * jnp.exp(b_last[i] - rows_of(b, i)) for i in range(n_chunks)], 0)
    row1 = lax.broadcasted_iota(jnp.int32, (c_len, 1), 0)
    k_local = []
    for i in range(n_chunks):
        per_sub = []
        for s in range(n_sub):
            ref = b_start[i * c_len + s * sub:i * c_len + s * sub + 1, :]
            expo = jnp.where(row1 < (s + 1) * sub, ref - rows_of(b, i), -1e30)
            per_sub.append(rows_of(k, i) * jnp.exp(expo))
        k_local.append(per_sub)

    half0 = lax.broadcasted_iota(jnp.int32, (1, LANE), 1) < dk
    rc, cc = _iota2(c_len, c_len)
    causal = rc >= cc
    top_rows = lax.broadcasted_iota(jnp.int32, (LANE, 1), 0) < dk
    kt = lambda x, i, j: x[i * c_len:(i + 1) * c_len, j * LANE:(j + 1) * LANE]
    vt = lambda x, i, h: x[i * c_len:(i + 1) * c_len, h * dv:(h + 1) * dv]
    pairs = [(i, j) for i in range(n_chunks) for j in range(n_pairs)]
    heads = [(i, h) for i in range(n_chunks) for h in range(n_heads)]

    yield

    def a_rows(i, j, s):
        ql = kt(q_local, i, j)[s * sub:(s + 1) * sub]
        lhs = cat([jnp.where(half0, ql, 0.0), jnp.where(half0, 0.0, ql)], 0)
        return _mm(lhs, kt(k_local[i][s], 0, j), NT)
    a_blk = {(i, j): [a_rows(i, j, s) for s in range(n_sub)] for i, j in pairs}
    yield
    a_mat = {}
    for i, h in heads:
        blocks = [a_blk[(i, h // 2)][s][(h % 2) * sub:(h % 2 + 1) * sub] for s in range(n_sub)]
        a_mat[(i, h)] = jnp.where(causal, cat(blocks, 0) if n_sub > 1 else blocks[0], 0.0)
    o_intra = {(i, h): _mm(a_mat[(i, h)], vt(v, i, h)) for i, h in heads}
    yield
    upd = {}
    dec = {}
    for i, j in pairs:
        kv = _mm(kt(k_state, i, j), v[i * c_len:(i + 1) * c_len, 2 * j * dv:(2 * j + 2) * dv], TN)
        upd[(i, j)] = jnp.where(top_rows, kv[:, :dv], kv[:, dv:])
        dec[(i, j)] = jnp.exp(jnp.transpose(jnp.broadcast_to(kt(b_last[i], 0, j)[0:1], (LANE, LANE))))
    yield
    states = {}
    for j in range(n_pairs):
        if chained:
            s_cur = st_ref[0, j]
            for i in range(n_chunks):
                states[(i, j)] = s_cur
                s_cur = dec[(i, j)] * s_cur + upd[(i, j)]
            st_ref[0, j] = s_cur
        else:
            for i in range(n_chunks):
                states[(i, j)] = gla0_ref[chunks[i], j]
                st_ref[chunks[i], j] = dec[(i, j)] * states[(i, j)] + upd[(i, j)]
    for i in range(n_chunks):
        outs = []
        for h in range(n_heads):
            qi = kt(q_inter, i, h // 2)
            qi = jnp.where(half0, qi, 0.0) if h % 2 == 0 else jnp.where(half0, 0.0, qi)
            o_h = o_intra[(i, h)] + _mm(qi, states[(i, h // 2)])
            o_h = o_h * lax.rsqrt(jnp.mean(o_h * o_h, axis=-1, keepdims=True) + EPS) * ng_ref[...]
            og_h = vt(og, i, h)
            outs.append(o_h * (og_h * jax.nn.sigmoid(og_h)))
        o_ref[0, row0 + i * c_len:row0 + (i + 1) * c_len, o_col:o_col + nv] = cat(outs, 1)
        yield


def _chunk_len(t):
    return GLA_CHUNK if t % GLA_CHUNK == 0 else t


def _pad_rows(w, start, total):
    return jnp.zeros((total, w.shape[1]), w.dtype).at[start:start + w.shape[0]].set(w)


def _prep_layer(l, ln1_g, ffn1_wg, ffn1_wu, ffn1_wd, ln_mix_g, w_in, mu_shift, w0, w_lora_up, a0, a_lora_up,
                g_lora_up, k_k, k_a, r_k, lnx_w, lnx_b, gk_up, gk_b, gla_norm_g, w_out, ln2_g, ffn2_wg, ffn2_wu,
                ffn2_wd):
    d_a = w0.shape[1]
    a_cols = mu_shift.shape[1]
    nk = gk_b.shape[1]
    d_b = w_out.shape[1] - d_a
    row = lambda x: x[l].reshape(1, -1)
    wi = w_in[l]
    n_qkv = a_cols + 2 * nk + d_b
    pad = jnp.zeros((wi.shape[0], LANE - B_GATE_RANK), wi.dtype)
    win = jnp.concatenate([wi[:, :n_qkv], wi[:, n_qkv + B_GATE_RANK:], wi[:, n_qkv:n_qkv + B_GATE_RANK], pad],
                          axis=1).astype(BF16)
    return dict(
        pre=(row(ln1_g), ffn1_wg[l].astype(BF16), ffn1_wu[l].astype(BF16), ffn1_wd[l].astype(BF16), row(ln_mix_g),
             win, row(mu_shift), row(w0), _pad_rows(w_lora_up[l], 0, LANE).astype(BF16), row(a0),
             _pad_rows(a_lora_up[l], A_W_RANK, LANE).astype(BF16),
             _pad_rows(g_lora_up[l], A_W_RANK + A_A_RANK, LANE).astype(BF16), row(k_k), row(k_a), row(r_k)),
        rwkv=(row(lnx_w), row(lnx_b)),
        gla=(_pad_rows(gk_up[l], 0, LANE).astype(BF16), row(gk_b), row(gla_norm_g)),
        post_weights=(w_out[l], ffn2_wg[l], ffn2_wu[l], ffn2_wd[l]),
        ln2=row(ln2_g),
    )


def _trunk(x, shift, wkv, gla, layers, ln_f, post_bf16):
    b, t, d = x.shape
    c_len = _chunk_len(t)
    xf = x.reshape(b * t, d)
    new_shift, new_wkv, new_gla = [], [], []
    for l, lw in enumerate(layers):
        ride = lw["post_weights"] if post_bf16[l] is None else ()
        (x1, pb, ops, wend, last), cast = _pre_call(xf, shift[l], c_len, ride, *lw["pre"])
        if post_bf16[l] is None:
            post_bf16[l] = cast
        wo, wg, wu, wd = post_bf16[l]
        o, s_a, s_b = _mixer_call(ops.reshape(b, t, -1), wend.reshape(b, t // c_len, 1, -1), pb.reshape(b, t, -1),
                                  wkv[l], gla[l], c_len, *lw["rwkv"], *lw["gla"])
        xf = _post_call(x1, o.reshape(b * t, -1), wo, lw["ln2"], wg, wu, wd, ln_f, final_norm=(l == len(layers) - 1))
        new_shift.append(last.reshape(b, t // c_len, 1, -1)[:, -1])
        new_wkv.append(s_a)
        new_gla.append(s_b)
    return xf.reshape(b, t, d), jnp.stack(new_shift), jnp.stack(new_wkv), jnp.stack(new_gla)


def kernel(x_prompt, x_sample, state_shift, state_wkv, state_gla, ln1_g, ffn1_wg, ffn1_wu, ffn1_wd, ln_mix_g, w_in,
           mu_shift, w0, w_lora_up, a0, a_lora_up, g_lora_up, k_k, k_a, r_k, lnx_w, lnx_b, gk_up, gk_b, gla_norm_g,
           w_out, ln2_g, ffn2_wg, ffn2_wu, ffn2_wd, ln_f_g):
    depth = ln1_g.shape[0]
    per_layer = (ln1_g, ffn1_wg, ffn1_wu, ffn1_wd, ln_mix_g, w_in, mu_shift, w0, w_lora_up, a0, a_lora_up, g_lora_up,
                 k_k, k_a, r_k.reshape(depth, -1), lnx_w, lnx_b, gk_up, gk_b, gla_norm_g, w_out, ln2_g, ffn2_wg,
                 ffn2_wu, ffn2_wd)
    layers = [_prep_layer(l, *per_layer) for l in range(depth)]
    ln_f = ln_f_g.reshape(1, -1)
    bp = x_prompt.shape[0]
    shift0 = jnp.zeros((depth, bp) + state_shift.shape[2:], F32)
    wkv0 = jnp.zeros((depth, bp) + state_wkv.shape[2:], F32)
    gla0 = jnp.zeros((depth, bp) + state_gla.shape[2:], F32)
    post_bf16 = [None] * depth
    y_p, shift_p, wkv_p, gla_p = _trunk(x_prompt, shift0, wkv0, gla0, layers, ln_f, post_bf16)
    y_s, shift_s, wkv_s, gla_s = _trunk(x_sample, state_shift, state_wkv, state_gla, layers, ln_f, post_bf16)
    return (y_p, y_s, shift_p, wkv_p, gla_p, shift_s, wkv_s, gla_s)
```

```python
import functools
import math

import jax
import jax.numpy as jnp
from jax import lax
from jax.experimental import pallas as pl
from jax.experimental.pallas import tpu as pltpu

F32 = jnp.float32
BF16 = jnp.bfloat16

EPS = 1e-6
LNX_EPS = 64e-5
A_HEAD = 64
A_W_RANK = 32
A_A_RANK = 32
B_GATE_RANK = 16
B_TAU = 16.0
GLA_CHUNK = 64
SUB = 16
LANE = 128
VMEM_LIMIT = 56 * 1024 * 1024
PRE_TILE = 256
POST_TILE = 512
MIXER_CHUNKS_PER_STEP = 16
MIXER_SEQUENCES_PER_STEP = 16
RWKV_CHUNKS_PER_GROUP = 8

NN = ((1,), (0,))
NT = ((1,), (1,))
TN = ((0,), (0,))


def _dot(a, b, precision=None):
    return jnp.dot(a, b, preferred_element_type=F32, precision=precision)


def _mm(a, b, dims=NN):
    return lax.dot_general(a.astype(BF16), b.astype(BF16), (dims, ((), ())), preferred_element_type=F32)


def _cumsum_rows(x, c_len):
    row = lax.broadcasted_iota(jnp.int32, (x.shape[0], 1), 0) & (c_len - 1)
    shift = 1
    while shift < c_len:
        x = x + jnp.where(row >= shift, pltpu.roll(x, shift, axis=0), 0.0)
        shift *= 2
    return x


def _softplus(x):
    return jnp.maximum(x, 0.0) + jnp.log(1.0 + jnp.exp(-jnp.abs(x)))


def _rmsnorm(x, g):
    return x * lax.rsqrt(jnp.mean(x * x, axis=-1, keepdims=True) + EPS) * g


def _swiglu_half_step(x, ln_g, wg_ref, wu_ref, wd_ref):
    h = _rmsnorm(x, ln_g).astype(BF16)
    g = _dot(h, wg_ref[...])
    u = _dot(h, wu_ref[...])
    act = (g * jax.nn.sigmoid(g) * u).astype(BF16)
    return x + 0.5 * _dot(act, wd_ref[...])


def _iota2(n, m):
    return lax.broadcasted_iota(jnp.int32, (n, m), 0), lax.broadcasted_iota(jnp.int32, (n, m), 1)


def _cat(xs, axis):
    return jnp.concatenate(xs, axis=axis)


def _emit(*tasks):
    tasks = list(tasks)
    while tasks:
        for task in list(tasks):
            if next(task, True):
                tasks.remove(task)


def _head_sum_lanes(x):
    first = lax.broadcasted_iota(jnp.int32, (1, LANE), 1) < A_HEAD
    tiles = []
    for s in range(0, x.shape[1], LANE):
        xt = x[:, s:s + LANE]
        even = jnp.sum(jnp.where(first, xt, 0.0), axis=-1, keepdims=True)
        odd = jnp.sum(jnp.where(first, 0.0, xt), axis=-1, keepdims=True)
        tiles.append(jnp.where(first, even, odd))
    return _cat(tiles, 1)


OPERANDS = ("a_hat", "k_hat", "b_hat", "kbar", "bbar", "v", "r_til", "bonus", "g")


N_PRE_INPUTS = 17
N_PRE_OUTPUTS = 5


def _pre_kernel(c_len, n_tiles, n_ride, *refs):
    (x_ref, shift_ref, ln1_ref, wg_ref, wu_ref, wd_ref, lnm_ref, win_ref, mu_ref, w0_ref, wup_ref, a0_ref, aup_ref,
     gup_ref, kk_ref, ka_ref, rk_ref) = refs[:N_PRE_INPUTS]
    ride_in = refs[N_PRE_INPUTS:N_PRE_INPUTS + n_ride]
    outs = refs[N_PRE_INPUTS + n_ride:]
    x1_ref, pb_ref, ops_ref, wend_ref, last_ref = outs[:N_PRE_OUTPUTS]
    ride_out = outs[N_PRE_OUTPUTS:N_PRE_OUTPUTS + n_ride]
    pa_scr, carry_ref = outs[N_PRE_OUTPUTS + n_ride:]
    step = pl.program_id(0)
    tm = x_ref.shape[0]
    d_a = w0_ref.shape[1]
    n_chunks = tm // c_len
    row1 = lax.broadcasted_iota(jnp.int32, (c_len, 1), 0)

    def front_end(load_p, load_prev, chunk):
        p = load_p()
        shifted = jnp.where(row1 == 0, load_prev(), pltpu.roll(p, 1, axis=0))
        last_ref[chunk] = p[c_len - 1:c_len, :]
        xs = p + (shifted - p) * mu_ref[...]
        r = xs[:, 0:d_a]
        k = xs[:, d_a:2 * d_a]
        v = xs[:, 2 * d_a:3 * d_a]
        lora_in = xs[:, 3 * d_a:3 * d_a + LANE]
        lw = _dot(jnp.tanh(lora_in).astype(BF16), wup_ref[...])
        la = _dot(lora_in.astype(BF16), aup_ref[...])
        g = _dot(jax.nn.sigmoid(lora_in).astype(BF16), gup_ref[...])
        yield
        kk = k * kk_ref[...]
        kk_sq = _head_sum_lanes(kk * kk)
        yield
        logw = (-math.exp(-0.5) * math.log2(math.e)) * jax.nn.sigmoid(w0_ref[...] + lw)
        gc = _cumsum_rows(logw, c_len)
        yield
        a = jax.nn.sigmoid(a0_ref[...] + la)
        kk = kk * lax.rsqrt(jnp.maximum(kk_sq, 1e-24))
        kmod = k * (1.0 + (a - 1.0) * ka_ref[...])
        kka = kk * a
        bonus = _head_sum_lanes(r * kmod * rk_ref[...]) * v
        yield
        gc_last = gc[c_len - 1:c_len, :]
        e_inv = jnp.exp2(-gc)
        e_end = jnp.exp2(gc_last - gc)
        ops = dict(a_hat=-kk * jnp.exp2(gc - logw), k_hat=kmod * e_inv, b_hat=kka * e_inv, kbar=kmod * e_end,
                   bbar=kka * e_end, v=v, r_til=r * jnp.exp2(gc), bonus=bonus, g=g)
        rows = slice(chunk * c_len, (chunk + 1) * c_len)
        for n, name in enumerate(OPERANDS):
            ops_ref[rows, n * d_a:(n + 1) * d_a] = ops[name].astype(BF16)
        wend_ref[chunk] = jnp.exp2(gc_last)

    def ffn(out):
        x = x_ref[...]
        h = _rmsnorm(x, ln1_ref[...]).astype(BF16)
        g = _dot(h, wg_ref[...])
        yield
        u = _dot(h, wu_ref[...])
        yield
        act = (g * jax.nn.sigmoid(g) * u).astype(BF16)
        x1 = x + 0.5 * _dot(act, wd_ref[...])
        x1_ref[...] = x1
        yield
        h = _rmsnorm(x1, lnm_ref[...]).astype(BF16)
        a_cols = mu_ref.shape[1]
        out["pa"] = _dot(h, win_ref[:, :a_cols])
        yield
        pb_ref[...] = _dot(h, win_ref[:, a_cols:])

    if n_tiles == 1:
        out = {}
        _emit(ffn(out))
        pa = out["pa"]
        _emit(*[front_end(lambda c=c: pa[c * c_len:(c + 1) * c_len], lambda c=c: shift_ref[c], c)
                for c in range(n_chunks)])
        return

    @pl.when(step == 0)
    def _():
        pa_scr[...] = jnp.zeros(pa_scr.shape, F32)
        carry_ref[...] = shift_ref[0]

    def previous_tile():
        return [front_end(lambda c=c: pa_scr[c * c_len:(c + 1) * c_len, :],
                          lambda c=c: carry_ref[...] if c == 0 else pa_scr[c * c_len - 1:c * c_len, :], c)
                for c in range(n_chunks)]

    @pl.when(step < n_tiles)
    def _():
        out = {}
        _emit(*previous_tile(), ffn(out))
        for src, dst in zip(ride_in, ride_out):
            dst[...] = src[...].astype(BF16)
        carry_ref[...] = jnp.where(step > 0, pa_scr[tm - 1:tm, :], carry_ref[...])
        pa_scr[...] = out["pa"]

    @pl.when(step == n_tiles)
    def _():
        _emit(*previous_tile())


def _resident(shape):
    return pl.BlockSpec(shape, lambda *_: (0,) * len(shape), pipeline_mode=pl.Buffered(1))


def _token_tile(n, tile):
    tm = min(tile, n)
    assert n % tm == 0
    return tm


def _pre_call(x, shift, c_len, ride, ln1, wg, wu, wd, lnm, win, *front_consts):
    n, d = x.shape
    a_cols = front_consts[0].shape[1]
    b_cols = win.shape[1] - a_cols
    d_a = front_consts[1].shape[1]
    tm = _token_tile(n, PRE_TILE)
    n_tiles = n // tm
    n_chunks = tm // c_len
    assert (shift.shape[0] == 1) if n_tiles > 1 else (shift.shape[0] == n_chunks)
    lag = 1 if n_tiles > 1 else 0
    cur = lambda w: pl.BlockSpec((tm, w), lambda i: (jnp.minimum(i, n_tiles - 1), 0))
    late = lambda shp: pl.BlockSpec(shp, lambda i: (jnp.maximum(i - lag, 0),) + (0,) * (len(shp) - 1))
    consts = [ln1, wg, wu, wd, lnm, win, *front_consts]
    assert 2 + len(consts) == N_PRE_INPUTS
    def ride_spec(w):
        nb = max(k for k in range(1, n_tiles + 1) if w.shape[0] % (16 * k) == 0)
        return pl.BlockSpec((w.shape[0] // nb, w.shape[1]), lambda i: (jnp.minimum(i, nb - 1), 0))

    assert not ride or n_tiles > 1
    outs = pl.pallas_call(
        functools.partial(_pre_kernel, c_len, n_tiles, len(ride)),
        grid=(n_tiles + lag,),
        in_specs=[cur(d), _resident(shift.shape)] + [_resident(w.shape) for w in consts] + [ride_spec(w) for w in ride],
        out_specs=[cur(d), cur(b_cols), late((tm, len(OPERANDS) * d_a)), late((n_chunks, 1, d_a)),
                   late((n_chunks, 1, a_cols))] + [ride_spec(w) for w in ride],
        out_shape=[jax.ShapeDtypeStruct((n, d), F32), jax.ShapeDtypeStruct((n, b_cols), F32),
                   jax.ShapeDtypeStruct((n, len(OPERANDS) * d_a), BF16),
                   jax.ShapeDtypeStruct((n // c_len, 1, d_a), F32),
                   jax.ShapeDtypeStruct((n // c_len, 1, a_cols), F32)]
        + [jax.ShapeDtypeStruct(w.shape, BF16) for w in ride],
        scratch_shapes=[pltpu.VMEM((tm, a_cols), F32), pltpu.VMEM((1, a_cols), F32)],
        compiler_params=pltpu.CompilerParams(dimension_semantics=("arbitrary",), vmem_limit_bytes=VMEM_LIMIT),
        name="pre_ffn_inproj",
    )(x, shift, *consts, *ride)
    return outs[:N_PRE_OUTPUTS], outs[N_PRE_OUTPUTS:]


def _post_kernel(final_norm, x1_ref, o_ref, wo_ref, ln2_ref, wg_ref, wu_ref, wd_ref, lnf_ref, y_ref):
    x2 = x1_ref[...] + _dot(o_ref[...].astype(BF16), wo_ref[...])
    x3 = _swiglu_half_step(x2, ln2_ref[...], wg_ref, wu_ref, wd_ref)
    y_ref[...] = _rmsnorm(x3, lnf_ref[...]) if final_norm else x3


def _post_call(x1, o, wo, ln2, wg, wu, wd, lnf, final_norm):
    n, d = x1.shape
    tm = _token_tile(n, POST_TILE)
    tok = lambda w: pl.BlockSpec((tm, w), lambda i: (i, 0))
    consts = [wo, ln2, wg, wu, wd, lnf]
    return pl.pallas_call(
        functools.partial(_post_kernel, final_norm),
        grid=(n // tm,),
        in_specs=[tok(d), tok(o.shape[1])] + [_resident(w.shape) for w in consts],
        out_specs=tok(d),
        out_shape=jax.ShapeDtypeStruct((n, d), F32),
        compiler_params=pltpu.CompilerParams(dimension_semantics=("arbitrary",), vmem_limit_bytes=VMEM_LIMIT),
        name="post_outproj_ffn",
    )(x1, o, *consts)


def _mixer_kernel(c_len, n_sub, chained, ops_ref, wend_ref, pb_ref, wkv0_ref, gla0_ref, lnw_ref, lnb_ref,
                  gkup_ref, gkb_ref, ng_ref, o_ref, wkv_ref, gla_ref, st_ref):
    step = pl.program_id(1)
    n_pairs = wkv_ref.shape[1] // 2
    d_a = n_pairs * LANE
    c2 = 2 * c_len
    cat = _cat

    def load_state(i, j):
        both = cat([wkv0_ref[i, 2 * j], wkv0_ref[i, 2 * j + 1]], 0)
        return jnp.transpose(cat([both, jnp.zeros((LANE, LANE - A_HEAD), F32)], 1))[:A_HEAD]

    def store_state(i, j, s):
        t = jnp.transpose(cat([s, jnp.zeros((LANE - A_HEAD, LANE), F32)], 0))
        wkv_ref[i, 2 * j] = t[:A_HEAD, :A_HEAD]
        wkv_ref[i, 2 * j + 1] = t[A_HEAD:, :A_HEAD]

    if chained:
        @pl.when(step == 0)
        def _():
            gla_ref[...] = gla0_ref[...]
            for j in range(n_pairs):
                st_ref[j] = load_state(0, j)

    def operand(name, i, j=None):
        col = OPERANDS.index(name) * d_a
        cols = slice(col, col + d_a) if j is None else slice(col + j * LANE, col + (j + 1) * LANE)
        return ops_ref[0, i * c_len:(i + 1) * c_len, cols]

    rr, cc = _iota2(c_len, c2)
    col_t = cc & (c_len - 1)
    strict = col_t < rr
    lower = col_t <= rr
    eye_tt = jnp.where(col_t == rr, 1.0, 0.0).astype(F32)
    rk, ck = _iota2(A_HEAD, LANE)
    eye_kk = jnp.where((ck & (A_HEAD - 1)) == rk, 1.0, 0.0).astype(F32)
    half_k = lax.broadcasted_iota(jnp.int32, (1, LANE), 1) < A_HEAD
    half_t = lax.broadcasted_iota(jnp.int32, (1, c2), 1) < c_len
    n_double = int(math.log2(c_len))
    assert 2 ** n_double == c_len and n_double >= 2

    def bd(x, half):
        xb = x.astype(BF16)
        return cat([xb * jnp.where(half, 1.0, 0.0).astype(BF16), xb * jnp.where(half, 0.0, 1.0).astype(BF16)], 0)

    bd_k = lambda x: bd(x, half_k)
    bd_t = lambda x: bd(x, half_t)
    rq, cq = _iota2(LANE, LANE)
    same_head = jnp.where((rq < A_HEAD) == (cq < A_HEAD), 1.0, 0.0).astype(BF16)

    def bd_kt(x):
        return jnp.transpose(cat([x, x], 0)) * same_head
    st = [st_ref[j] for j in range(n_pairs)] if chained else None
    mixed = {}

    def mix(chunks):
        chains = [(i, j) for i in chunks for j in range(n_pairs)]
        each = lambda f: [f(c) for c in range(len(chains))]
        at = lambda name: [operand(name, i, j) for i, j in chains]
        a_t, r_t, k_t, b_t, v_t = at("a_hat"), at("r_til"), at("k_hat"), at("b_hat"), at("v")
        kbar_t, bbar_t = at("kbar"), at("bbar")
        w_end = [wend_ref[0, i, :, j * LANE:(j + 1) * LANE] for i, j in chains]
        if c2 == LANE:
            bd_b = each(lambda c: bd_kt(b_t[c]))
            bd_kh = each(lambda c: bd_kt(k_t[c]))
            xa = each(lambda c: _mm(a_t[c], cat([bd_b[c], bd_kh[c]], 1)))
            yield
            xr = each(lambda c: _mm(r_t[c], cat([bd_kh[c], bd_b[c]], 1)))
        else:
            bd_b = each(lambda c: bd_k(b_t[c]))
            bd_kh = each(lambda c: bd_k(k_t[c]))
            xa = each(lambda c: _mm(a_t[c], cat([bd_b[c], bd_kh[c]], 0), NT))
            yield
            xr = each(lambda c: _mm(r_t[c], cat([bd_kh[c], bd_b[c]], 0), NT))
        l_mat = each(lambda c: jnp.where(strict, xa[c][:, :c2], 0.0))
        a_ak = each(lambda c: jnp.where(strict, xa[c][:, c2:], 0.0))
        yield
        t_inv = each(lambda c: eye_tt + l_mat[c])
        pw = each(lambda c: _mm(l_mat[c], bd_t(l_mat[c])))
        a_rk_rb = each(lambda c: cat([jnp.where(lower, xr[c][:, :c2], 0.0), jnp.where(lower, xr[c][:, c2:], 0.0)], 1))
        yield
        for _ in range(1, n_double - 1):
            y = each(lambda c: _mm(pw[c], cat([bd_t(pw[c]), bd_t(t_inv[c])], 1)))
            t_inv = each(lambda c: t_inv[c] + y[c][:, c2:])
            pw = each(lambda c: y[c][:, :c2])
            yield
        t_inv = each(lambda c: t_inv[c] + _mm(pw[c], bd_t(t_inv[c])))
        bd_v = each(lambda c: bd_k(v_t[c]))
        a_ak_v = each(lambda c: _mm(a_ak[c], bd_v[c]))
        yield
        pq = each(lambda c: _mm(t_inv[c], cat([bd_k(a_t[c]), bd_k(a_ak_v[c])], 1)))
        p_t = each(lambda c: pq[c][:, :LANE])
        q_t = each(lambda c: pq[c][:, LANE:])
        yield
        hg = each(lambda c: _mm(a_rk_rb[c], cat([cat([bd_v[c], jnp.zeros((c2, LANE), BF16)], 1),
                                                 cat([bd_k(q_t[c]), bd_k(p_t[c])], 1)], 0)))
        yield
        nm = each(lambda c: _mm(cat([kbar_t[c], bbar_t[c]], 0),
                                cat([cat([v_t[c], jnp.zeros((c_len, LANE), BF16)], 1),
                                     cat([q_t[c], p_t[c]], 1).astype(BF16)], 0), TN))
        n_sbs = each(lambda c: jnp.where(half_k, nm[c][:A_HEAD, :LANE], nm[c][A_HEAD:, :LANE]))
        m_sbs = each(lambda c: jnp.where(half_k, nm[c][:A_HEAD, LANE:], nm[c][A_HEAD:, LANE:]) + eye_kk * w_end[c])
        g_pair = each(lambda c: r_t[c] + hg[c][:, LANE:])
        h_pair = each(lambda c: hg[c][:, :LANE])
        yield
        for n, i in enumerate(chunks):
            out_tiles = []
            for j in range(n_pairs):
                c = n * n_pairs + j
                both = _mm(cat([m_sbs[c], g_pair[c]], 0), bd_k(st[j] if chained else load_state(i, j)))
                out_tiles.append(both[A_HEAD:] + h_pair[c])
                if chained:
                    st[j] = both[:A_HEAD] + n_sbs[c]
                else:
                    store_state(i, j, both[:A_HEAD] + n_sbs[c])
            mixed[i] = cat(out_tiles, 1)
            yield

    def post(i):
        o = mixed[i]
        inv_n = 1.0 / A_HEAD
        mean = _head_sum_lanes(o) * inv_n
        cen = o - mean
        yield
        var = _head_sum_lanes(cen * cen) * inv_n
        on = cen * lax.rsqrt(var + LNX_EPS) * lnw_ref[...] + lnb_ref[...]
        yield
        o_ref[0, i * c_len:(i + 1) * c_len, 0:d_a] = (on + operand("bonus", i)) * operand("g", i)

    groups = [list(range(s, min(s + RWKV_CHUNKS_PER_GROUP, n_sub))) for s in range(0, n_sub, RWKV_CHUNKS_PER_GROUP)]
    for n, grp in enumerate(groups):
        gla = _gla_part(c_len, grp, chained, pb_ref, gla0_ref, gkup_ref, gkb_ref, ng_ref, o_ref, d_a, gla_ref)
        _emit(mix(grp), gla, *([post(i) for i in groups[n - 1]] if n > 0 else []))
    _emit(*[post(i) for i in groups[-1]])
    if chained:
        for j in range(n_pairs):
            st_ref[j] = st[j]

        @pl.when(step == pl.num_programs(1) - 1)
        def _():
            for j in range(n_pairs):
                store_state(0, j, st_ref[j])


def _mixer_call(ops, wend, pb, wkv0, gla0, c_len, lnw, lnb, gkup, gkb, ng):
    b, t, ops_cols = ops.shape
    d_a = wkv0.shape[1] * A_HEAD
    n_heads_b, dk, dv = gla0.shape[1:]
    gla_shape = (n_heads_b // 2, 2 * dk, dv)
    n_chunks = t // c_len
    chained = n_chunks > 1
    if chained:
        n_sub = math.gcd(n_chunks, MIXER_CHUNKS_PER_STEP)
        grid = (b, n_chunks // n_sub)
        per_seq = lambda shp: pl.BlockSpec((1,) + shp, lambda i, j: (i,) + (0,) * len(shp))
    else:
        n_sub = math.gcd(b, MIXER_SEQUENCES_PER_STEP)
        grid = (1, b // n_sub)
        ops, wend, pb = ops.reshape(1, b * t, ops_cols), wend.reshape(1, b, 1, d_a), pb.reshape(1, b * t, -1)
        per_seq = lambda shp: pl.BlockSpec((n_sub,) + shp, lambda i, j: (j,) + (0,) * len(shp))
    seq = lambda w: pl.BlockSpec((1, c_len * n_sub, w), lambda i, j: (i, j, 0))
    consts = [lnw, lnb, gkup, gkb, ng]
    d_o = d_a + n_heads_b * dv
    o, wkv, gla = pl.pallas_call(
        functools.partial(_mixer_kernel, c_len, n_sub, chained),
        grid=grid,
        in_specs=[seq(ops_cols), pl.BlockSpec((1, n_sub, 1, d_a), lambda i, j: (i, j, 0, 0)), seq(pb.shape[2]),
                  per_seq(wkv0.shape[1:]), per_seq(gla_shape)] + [_resident(w.shape) for w in consts],
        out_specs=[seq(d_o), per_seq(wkv0.shape[1:]), per_seq(gla_shape)],
        out_shape=[jax.ShapeDtypeStruct(ops.shape[:2] + (d_o,), F32), jax.ShapeDtypeStruct(wkv0.shape, F32),
                   jax.ShapeDtypeStruct((b,) + gla_shape, F32)],
        scratch_shapes=[pltpu.VMEM((d_a // LANE, A_HEAD, LANE), F32)],
        compiler_params=pltpu.CompilerParams(dimension_semantics=("arbitrary", "arbitrary"),
                                             vmem_limit_bytes=VMEM_LIMIT),
        name="mixers",
    )(ops, wend, pb, wkv0, gla0.reshape((b,) + gla_shape), *consts)
    return o.reshape(b, t, d_o), wkv, gla.reshape(gla0.shape)


def _gla_part(c_len, chunks, chained, pb_ref, gla0_ref, gkup_ref, gkb_ref, ng_ref, o_ref, o_col, st_ref):
    n_pairs = st_ref.shape[1]
    dk = st_ref.shape[2] // 2
    dv = st_ref.shape[3]
    assert 2 * dk == LANE and dv == LANE
    n_heads = 2 * n_pairs
    nk = n_heads * dk
    nv = n_heads * dv
    n_chunks = len(chunks)
    tb = c_len * n_chunks
    row0 = chunks[0] * c_len
    cat = _cat

    p = pb_ref[0, row0:row0 + tb, :]
    q = p[:, 0:nk] * (dk ** -0.5)
    k = p[:, nk:2 * nk]
    v = p[:, 2 * nk:2 * nk + nv]
    og = p[:, 2 * nk + nv:2 * nk + 2 * nv]
    xgk = p[:, 2 * nk + 2 * nv:2 * nk + 2 * nv + LANE]
    z = _dot(xgk.astype(BF16), gkup_ref[...]) + gkb_ref[...]
    log_a = -_softplus(-z) / B_TAU

    sub = min(SUB, c_len)
    n_sub = c_len // sub
    b = _cumsum_rows(log_a, c_len)
    b_start = cat([jnp.zeros((sub, nk), F32) if r % c_len == 0 else jnp.broadcast_to(b[r - 1:r, :], (sub, nk))
                   for r in range(0, tb, sub)], 0)
    yield
    q_inter = q * jnp.exp(b)
    q_local = q * jnp.exp(b - b_start)
    rows_of = lambda x, i: x[i * c_len:(i + 1) * c_len]
    b_last = [b[(i + 1) * c_len - 1:(i + 1) * c_len, :] for i in range(n_chunks)]
    k_state = cat([rows_of(k, i) * jnp.exp(b_last[i] - rows_of(b, i)) for i in range(n_chunks)], 0)
    row1 = lax.broadcasted_iota(jnp.int32, (c_len, 1), 0)
    k_local = []
    for i in range(n_chunks):
        per_sub = []
        for s in range(n_sub):
            ref = b_start[i * c_len + s * sub:i * c_len + s * sub + 1, :]
            expo = jnp.where(row1 < (s + 1) * sub, ref - rows_of(b, i), -1e30)
            per_sub.append(rows_of(k, i) * jnp.exp(expo))
        k_local.append(per_sub)

    half0 = lax.broadcasted_iota(jnp.int32, (1, LANE), 1) < dk
    rc, cc = _iota2(c_len, c_len)
    causal = rc >= cc
    top_rows = lax.broadcasted_iota(jnp.int32, (LANE, 1), 0) < dk
    kt = lambda x, i, j: x[i * c_len:(i + 1) * c_len, j * LANE:(j + 1) * LANE]
    vt = lambda x, i, h: x[i * c_len:(i + 1) * c_len, h * dv:(h + 1) * dv]
    pairs = [(i, j) for i in range(n_chunks) for j in range(n_pairs)]
    heads = [(i, h) for i in range(n_chunks) for h in range(n_heads)]

    yield

    def a_rows(i, j, s):
        ql = kt(q_local, i, j)[s * sub:(s + 1) * sub]
        lhs = cat([jnp.where(half0, ql, 0.0), jnp.where(half0, 0.0, ql)], 0)
        return _mm(lhs, kt(k_local[i][s], 0, j), NT)
    a_blk = {(i, j): [a_rows(i, j, s) for s in range(n_sub)] for i, j in pairs}
    yield
    a_mat = {}
    for i, h in heads:
        blocks = [a_blk[(i, h // 2)][s][(h % 2) * sub:(h % 2 + 1) * sub] for s in range(n_sub)]
        a_mat[(i, h)] = jnp.where(causal, cat(blocks, 0) if n_sub > 1 else blocks[0], 0.0)
    o_intra = {(i, h): _mm(a_mat[(i, h)], vt(v, i, h)) for i, h in heads}
    yield
    upd = {}
    dec = {}
    for i, j in pairs:
        kv = _mm(kt(k_state, i, j), v[i * c_len:(i + 1) * c_len, 2 * j * dv:(2 * j + 2) * dv], TN)
        upd[(i, j)] = jnp.where(top_rows, kv[:, :dv], kv[:, dv:])
        dec[(i, j)] = jnp.exp(jnp.transpose(jnp.broadcast_to(kt(b_last[i], 0, j)[0:1], (LANE, LANE))))
    yield
    states = {}
    for j in range(n_pairs):
        if chained:
            s_cur = st_ref[0, j]
            for i in range(n_chunks):
                states[(i, j)] = s_cur
                s_cur = dec[(i, j)] * s_cur + upd[(i, j)]
            st_ref[0, j] = s_cur
        else:
            for i in range(n_chunks):
                states[(i, j)] = gla0_ref[chunks[i], j]
                st_ref[chunks[i], j] = dec[(i, j)] * states[(i, j)] + upd[(i, j)]
    for i in range(n_chunks):
        outs = []
        for h in range(n_heads):
            qi = kt(q_inter, i, h // 2)
            qi = jnp.where(half0, qi, 0.0) if h % 2 == 0 else jnp.where(half0, 0.0, qi)
            o_h = o_intra[(i, h)] + _mm(qi, states[(i, h // 2)])
            o_h = o_h * lax.rsqrt(jnp.mean(o_h * o_h, axis=-1, keepdims=True) + EPS) * ng_ref[...]
            og_h = vt(og, i, h)
            outs.append(o_h * (og_h * jax.nn.sigmoid(og_h)))
        o_ref[0, row0 + i * c_len:row0 + (i + 1) * c_len, o_col:o_col + nv] = cat(outs, 1)
        yield


def _chunk_len(t):
    return GLA_CHUNK if t % GLA_CHUNK == 0 else t


def _pad_rows(w, start, total):
    return jnp.zeros((total, w.shape[1]), w.dtype).at[start:start + w.shape[0]].set(w)


def _prep_layer(l, ln1_g, ffn1_wg, ffn1_wu, ffn1_wd, ln_mix_g, w_in, mu_shift, w0, w_lora_up, a0, a_lora_up,
                g_lora_up, k_k, k_a, r_k, lnx_w, lnx_b, gk_up, gk_b, gla_norm_g, w_out, ln2_g, ffn2_wg, ffn2_wu,
                ffn2_wd):
    d_a = w0.shape[1]
    a_cols = mu_shift.shape[1]
    nk = gk_b.shape[1]
    d_b = w_out.shape[1] - d_a
    row = lambda x: x[l].reshape(1, -1)
    wi = w_in[l]
    n_qkv = a_cols + 2 * nk + d_b
    pad = jnp.zeros((wi.shape[0], LANE - B_GATE_RANK), wi.dtype)
    win = jnp.concatenate([wi[:, :n_qkv], wi[:, n_qkv + B_GATE_RANK:], wi[:, n_qkv:n_qkv + B_GATE_RANK], pad],
                          axis=1).astype(BF16)
    return dict(
        pre=(row(ln1_g), ffn1_wg[l].astype(BF16), ffn1_wu[l].astype(BF16), ffn1_wd[l].astype(BF16), row(ln_mix_g),
             win, row(mu_shift), row(w0), _pad_rows(w_lora_up[l], 0, LANE).astype(BF16), row(a0),
             _pad_rows(a_lora_up[l], A_W_RANK, LANE).astype(BF16),
             _pad_rows(g_lora_up[l], A_W_RANK + A_A_RANK, LANE).astype(BF16), row(k_k), row(k_a), row(r_k)),
        rwkv=(row(lnx_w), row(lnx_b)),
        gla=(_pad_rows(gk_up[l], 0, LANE).astype(BF16), row(gk_b), row(gla_norm_g)),
        post_weights=(w_out[l], ffn2_wg[l], ffn2_wu[l], ffn2_wd[l]),
        ln2=row(ln2_g),
    )


def _trunk(x, shift, wkv, gla, layers, ln_f, post_bf16):
    b, t, d = x.shape
    c_len = _chunk_len(t)
    xf = x.reshape(b * t, d)
    new_shift, new_wkv, new_gla = [], [], []
    for l, lw in enumerate(layers):
        ride = lw["post_weights"] if post_bf16[l] is None else ()
        (x1, pb, ops, wend, last), cast = _pre_call(xf, shift[l], c_len, ride, *lw["pre"])
        if post_bf16[l] is None:
            post_bf16[l] = cast
        wo, wg, wu, wd = post_bf16[l]
        o, s_a, s_b = _mixer_call(ops.reshape(b, t, -1), wend.reshape(b, t // c_len, 1, -1), pb.reshape(b, t, -1),
                                  wkv[l], gla[l], c_len, *lw["rwkv"], *lw["gla"])
        xf = _post_call(x1, o.reshape(b * t, -1), wo, lw["ln2"], wg, wu, wd, ln_f, final_norm=(l == len(layers) - 1))
        new_shift.append(last.reshape(b, t // c_len, 1, -1)[:, -1])
        new_wkv.append(s_a)
        new_gla.append(s_b)
    return xf.reshape(b, t, d), jnp.stack(new_shift), jnp.stack(new_wkv), jnp.stack(new_gla)


def kernel(x_prompt, x_sample, state_shift, state_wkv, state_gla, ln1_g, ffn1_wg, ffn1_wu, ffn1_wd, ln_mix_g, w_in,
           mu_shift, w0, w_lora_up, a0, a_lora_up, g_lora_up, k_k, k_a, r_k, lnx_w, lnx_b, gk_up, gk_b, gla_norm_g,
           w_out, ln2_g, ffn2_wg, ffn2_wu, ffn2_wd, ln_f_g):
    depth = ln1_g.shape[0]
    per_layer = (ln1_g, ffn1_wg, ffn1_wu, ffn1_wd, ln_mix_g, w_in, mu_shift, w0, w_lora_up, a0, a_lora_up, g_lora_up,
                 k_k, k_a, r_k.reshape(depth, -1), lnx_w, lnx_b, gk_up, gk_b, gla_norm_g, w_out, ln2_g, ffn2_wg,
                 ffn2_wu, ffn2_wd)
    layers = [_prep_layer(l, *per_layer) for l in range(depth)]
    ln_f = ln_f_g.reshape(1, -1)
    bp = x_prompt.shape[0]
    shift0 = jnp.zeros((depth, bp) + state_shift.shape[2:], F32)
    wkv0 = jnp.zeros((depth, bp) + state_wkv.shape[2:], F32)
    gla0 = jnp.zeros((depth, bp) + state_gla.shape[2:], F32)
    post_bf16 = [None] * depth
    y_p, shift_p, wkv_p, gla_p = _trunk(x_prompt, shift0, wkv0, gla0, layers, ln_f, post_bf16)
    y_s, shift_s, wkv_s, gla_s = _trunk(x_sample, state_shift, state_wkv, state_gla, layers, ln_f, post_bf16)
    return (y_p, y_s, shift_p, wkv_p, gla_p, shift_s, wkv_s, gla_s)
```
